```python
import jax, jax.numpy as jnp
from jax import lax
import numpy as np

D_MODEL = 1024
BATCH = 8
SEQ = 2048
DEPTH = 4

EXPAND = 2
D_MIX = EXPAND * D_MODEL
D_FOURIER = D_MIX // 2
D_GLA = D_MIX - D_FOURIER
N_FOURIER_GROUPS = 4
FOURIER_GROUP = D_FOURIER // N_FOURIER_GROUPS
N_GLA_HEADS = 4
D_GLA_KEY = D_GLA // 2
GLA_HEAD_K = D_GLA_KEY // N_GLA_HEADS
GLA_HEAD_V = D_GLA // N_GLA_HEADS
GATE_RANK = 16
GATE_LOGIT_NORMALIZER = 16.0
CHUNK = 64
EPS = 1e-6
SPLIT_SIZES = (D_FOURIER, D_FOURIER, D_GLA_KEY, D_GLA_KEY, D_GLA, D_GLA, GATE_RANK, GATE_RANK)
D_IN_PROJ = D_FOURIER * 2 + D_GLA_KEY * 2 + D_GLA * 2 + GATE_RANK * 2

kernel_name = "hymba_fnet_bigla_adaln_encoder"


def _split_points():
    pts, acc = [], 0
    for s in SPLIT_SIZES[:-1]:
        acc += s
        pts.append(acc)
    return pts


def rmsnorm(x, g):
    xf = x.astype(jnp.float32)
    y = xf * lax.rsqrt(jnp.mean(xf * xf, axis=-1, keepdims=True) + EPS) * g.astype(jnp.float32)
    return y.astype(x.dtype)


def fourier_mix(u, w_map):
    b, l, _ = u.shape
    ug = u.reshape(b, l, N_FOURIER_GROUPS, FOURIER_GROUP).astype(jnp.float32)
    spec = jnp.fft.fft2(ug, axes=(1, 3), norm="ortho").real
    y = jnp.einsum("blgc,gcd->blgd", spec, w_map.astype(jnp.float32))
    return y.reshape(b, l, D_FOURIER).astype(u.dtype)


def gla_direction(q, k, v, log_a, strict):
    b, h, l, dk = q.shape
    dv = v.shape[-1]
    n = l // CHUNK

    def chunks(t):
        return t.reshape(b, h, n, CHUNK, t.shape[-1])

    q, k, v, log_a = chunks(q), chunks(k), chunks(v), chunks(log_a)
    cum = jnp.cumsum(log_a, axis=3)
    ref = cum[:, :, :, CHUNK // 2:CHUNK // 2 + 1]
    last = cum[:, :, :, -1:]
    qi = q * jnp.exp(cum - ref)
    ki = k * jnp.exp(ref - cum)
    scores = jnp.einsum("bhncd,bhnsd->bhncs", qi, ki)
    mask = jnp.tril(jnp.ones((CHUNK, CHUNK), dtype=bool), k=-1 if strict else 0)
    scores = jnp.where(mask, scores, 0.0)
    o_intra = jnp.einsum("bhncs,bhnsv->bhncv", scores, v)
    q_dec = q * jnp.exp(cum)
    k_dec = k * jnp.exp(last - cum)
    chunk_decay = jnp.exp(last[:, :, :, 0, :])

    def step(state, inp):
        qd, kd, vc, dec = inp
        o = jnp.einsum("bhcd,bhdv->bhcv", qd, state)
        state = dec[..., None] * state + jnp.einsum("bhcd,bhcv->bhdv", kd, vc)
        return state, o

    xs = (jnp.moveaxis(q_dec, 2, 0), jnp.moveaxis(k_dec, 2, 0),
          jnp.moveaxis(v, 2, 0), jnp.moveaxis(chunk_decay, 2, 0))
    s0 = jnp.zeros((b, h, dk, dv), jnp.float32)
    _, o_inter = lax.scan(step, s0, xs)
    o_inter = jnp.moveaxis(o_inter, 0, 2)
    return (o_intra + o_inter).reshape(b, h, l, dv)


def gla_mix(q, k, v, gf_low, gb_low, w_af, b_af, w_ab, b_ab, norm_g):
    b, l, _ = q.shape

    def heads(t, d):
        return t.astype(jnp.float32).reshape(b, l, N_GLA_HEADS, d).transpose(0, 2, 1, 3)

    qh = heads(q, GLA_HEAD_K) * (GLA_HEAD_K ** -0.5)
    kh = heads(k, GLA_HEAD_K)
    vh = heads(v, GLA_HEAD_V)
    log_af = jax.nn.log_sigmoid(gf_low.astype(jnp.float32) @ w_af.astype(jnp.float32)
                                + b_af.astype(jnp.float32)) / GATE_LOGIT_NORMALIZER
    log_ab = jax.nn.log_sigmoid(gb_low.astype(jnp.float32) @ w_ab.astype(jnp.float32)
                                + b_ab.astype(jnp.float32)) / GATE_LOGIT_NORMALIZER
    o_fwd = gla_direction(qh, kh, vh, heads(log_af, GLA_HEAD_K), strict=False)
    flip = lambda t: jnp.flip(t, axis=2)
    o_bwd = flip(gla_direction(flip(qh), flip(kh), flip(vh),
                               flip(heads(log_ab, GLA_HEAD_K)), strict=True))
    o = o_fwd + o_bwd
    o = o * lax.rsqrt(jnp.mean(o * o, axis=-1, keepdims=True) + EPS)
    o = o.transpose(0, 2, 1, 3).reshape(b, l, D_GLA) * norm_g.astype(jnp.float32)
    return o.astype(q.dtype)


def setup_inputs(seed: int = 0) -> dict:
    key = jax.random.key(seed)
    ks = jax.random.split(key, 16)
    f32 = jnp.float32
    nrm = lambda k, shape, s: jax.random.normal(k, shape, f32) * s
    return {
        "x": nrm(ks[0], (BATCH, SEQ, D_MODEL), 1.0),
        "c": nrm(ks[1], (BATCH, D_MODEL), 1.0),
        "norm_g": 1.0 + nrm(ks[2], (DEPTH, D_MODEL), 0.02),
        "w_ada": nrm(ks[3], (DEPTH, D_MODEL, 3 * D_MODEL), D_MODEL ** -0.5),
        "b_ada": nrm(ks[4], (DEPTH, 3 * D_MODEL), 0.02),
        "w_in": nrm(ks[5], (DEPTH, D_MODEL, D_IN_PROJ), D_MODEL ** -0.5),
        "w_fmap": nrm(ks[6], (DEPTH, N_FOURIER_GROUPS, FOURIER_GROUP, FOURIER_GROUP), FOURIER_GROUP ** -0.5),
        "w_af": nrm(ks[7], (DEPTH, GATE_RANK, D_GLA_KEY), GATE_RANK ** -0.5),
        "b_af": nrm(ks[8], (DEPTH, D_GLA_KEY), 0.1),
        "w_ab": nrm(ks[9], (DEPTH, GATE_RANK, D_GLA_KEY), GATE_RANK ** -0.5),
        "b_ab": nrm(ks[10], (DEPTH, D_GLA_KEY), 0.1),
        "gla_norm_g": 1.0 + nrm(ks[11], (DEPTH, D_GLA), 0.02),
        "w_out": nrm(ks[12], (DEPTH, D_MIX, D_MODEL), D_MIX ** -0.5),
        "final_g": 1.0 + nrm(ks[13], (D_MODEL,), 0.02),
    }


def reference(x, c, norm_g, w_ada, b_ada, w_in, w_fmap, w_af, b_af, w_ab, b_ab,
              gla_norm_g, w_out, final_g):
    pts = _split_points()
    c_act = jax.nn.silu(c)
    for layer in range(DEPTH):
        mod = c_act @ w_ada[layer] + b_ada[layer]
        shift, scale, gate = jnp.split(mod, 3, axis=-1)
        h = rmsnorm(x, norm_g[layer]) * (1.0 + scale[:, None, :]) + shift[:, None, :]
        proj = h @ w_in[layer]
        u_f, z_f, q, k, v, r, gf_low, gb_low = jnp.split(proj, pts, axis=-1)
        y_f = fourier_mix(u_f, w_fmap[layer]) * jax.nn.silu(z_f)
        y_g = gla_mix(q, k, v, gf_low, gb_low, w_af[layer], b_af[layer],
                      w_ab[layer], b_ab[layer], gla_norm_g[layer]) * jax.nn.silu(r)
        y = jnp.concatenate([y_f, y_g], axis=-1) @ w_out[layer]
        x = x + gate[:, None, :] * y
    return rmsnorm(x, final_g)
```

```python
import functools

import numpy as np
import jax
import jax.numpy as jnp
from jax import lax
from jax.experimental import pallas as pl
from jax.experimental.pallas import tpu as pltpu

D_MODEL = 1024
D_FOURIER = 1024
N_FOURIER_GROUPS = 4
FOURIER_GROUP = 256
N_GLA_HEADS = 4
D_GLA_KEY = 512
D_GLA = 1024
GLA_HEAD_K = 128
GLA_HEAD_V = 256
GATE_RANK = 16
GATE_LOGIT_NORMALIZER = 16.0
CHUNK = 64
EPS = 1e-6

_OFF_Z = D_FOURIER
_OFF_MAIN_END = 2 * D_FOURIER + 2 * D_GLA_KEY + 2 * D_GLA
D_MAIN = _OFF_MAIN_END - _OFF_Z

GLA_BLOCK = 256
CHUNKS_PER_BLOCK = GLA_BLOCK // CHUNK

VMEM_LIMIT_BYTES = 56 * 1024 * 1024

_f32 = jnp.float32
_bf16 = jnp.bfloat16


def _dot(a, b):
    return jnp.dot(a, b, preferred_element_type=_f32)


def _split_bf16(a):
    hi = a.astype(_bf16)
    lo = (a - hi.astype(_f32)).astype(_bf16)
    return hi, lo


def _silu(v):
    return v * (1.0 / (1.0 + jnp.exp(-v)))


def _ada_kernel(c_ref, w_ref, b_ref, o_ref):
    c_act = _silu(c_ref[...]).astype(_bf16)
    o_ref[0] = _dot(c_act, w_ref[0].astype(_bf16)) + b_ref[0]


def _ada_mod(c, w_ada, b_ada):
    depth = w_ada.shape[0]
    b = c.shape[0]
    n_tiles = 3
    return pl.pallas_call(
        _ada_kernel,
        grid=(depth, n_tiles),
        in_specs=[
            pl.BlockSpec((b, D_MODEL), lambda l, j: (0, 0)),
            pl.BlockSpec((1, D_MODEL, D_MODEL), lambda l, j: (l, 0, j)),
            pl.BlockSpec((1, 1, D_MODEL), lambda l, j: (l, 0, j)),
        ],
        out_specs=pl.BlockSpec((1, b, D_MODEL), lambda l, j: (l, 0, j)),
        out_shape=jax.ShapeDtypeStruct((depth, b, 3 * D_MODEL), _f32),
        compiler_params=pltpu.CompilerParams(
            dimension_semantics=("arbitrary", "arbitrary"),
            vmem_limit_bytes=VMEM_LIMIT_BYTES),
        name="ada_mod",
    )(c, w_ada, b_ada.reshape(depth, 1, 3 * D_MODEL))


def _fold_kernel(wu_ref, wf_ref, cc_ref, sc_ref, wp_ref, wq_ref, *, norm):
    hp = lax.Precision.HIGHEST
    wf = wf_ref[0, 0]
    a = jnp.dot(cc_ref[...], wf, precision=hp, preferred_element_type=_f32) * norm
    bm = jnp.dot(sc_ref[...], wf, precision=hp, preferred_element_type=_f32) * norm
    wu = wu_ref[0]
    wp_ref[0] = jnp.dot(wu, a, precision=hp, preferred_element_type=_f32).astype(_bf16)
    wq_ref[0] = jnp.dot(wu, bm, precision=hp, preferred_element_type=_f32).astype(_bf16)


def _fold_fourier_weights(w_in, w_fmap, seq_len):
    depth = w_in.shape[0]
    idx = np.arange(FOURIER_GROUP)
    ang = 2.0 * np.pi * ((idx[:, None] * idx[None, :]) % FOURIER_GROUP) / FOURIER_GROUP
    cc = jnp.asarray(np.cos(ang), _f32)
    sc = jnp.asarray(np.sin(ang), _f32)
    norm = float(1.0 / np.sqrt(seq_len * FOURIER_GROUP))
    gspec = pl.BlockSpec((FOURIER_GROUP, FOURIER_GROUP), lambda l, g: (0, 0))
    ospec = pl.BlockSpec((1, D_MODEL, FOURIER_GROUP), lambda l, g: (l, 0, g))
    oshape = jax.ShapeDtypeStruct((depth, D_MODEL, D_FOURIER), _bf16)
    return pl.pallas_call(
        functools.partial(_fold_kernel, norm=norm),
        grid=(depth, N_FOURIER_GROUPS),
        in_specs=[
            pl.BlockSpec((1, D_MODEL, FOURIER_GROUP), lambda l, g: (l, 0, g)),
            pl.BlockSpec((1, 1, FOURIER_GROUP, FOURIER_GROUP), lambda l, g: (l, g, 0, 0)),
            gspec, gspec,
        ],
        out_specs=[ospec, ospec],
        out_shape=[oshape, oshape],
        compiler_params=pltpu.CompilerParams(
            dimension_semantics=("arbitrary", "arbitrary"),
            vmem_limit_bytes=VMEM_LIMIT_BYTES),
        name="fold_fourier",
    )(w_in, w_fmap, cc, sc)


def _inproj_kernel(x_ref, mod_ref, g_ref, wp_ref, wq_ref, wm_ref, wg_ref, wc_ref, bc_ref,
                   p_ref, q_ref, z_ref, qk_ref, v_ref, r_ref, la_ref):
    x = x_ref[0]
    ms = jnp.mean(x * x, axis=-1, keepdims=True)
    y = x * lax.rsqrt(ms + EPS) * g_ref[0]
    mod = mod_ref[0]
    shift = mod[0:1, :]
    scale = mod[1:2, :]
    h = (y * (1.0 + scale) + shift).astype(_bf16)

    p_ref[0] = _dot(h, wp_ref[0]).astype(_bf16)
    q_ref[0] = _dot(h, wq_ref[0]).astype(_bf16)
    z_ref[0] = _dot(h, wm_ref[0, :, 0:1024]).astype(_bf16)
    qk_ref[0] = _dot(h, wm_ref[0, :, 1024:2048]).astype(_bf16)
    v_ref[0] = _dot(h, wm_ref[0, :, 2048:3072]).astype(_bf16)
    r_ref[0] = _dot(h, wm_ref[0, :, 3072:4096]).astype(_bf16)

    g_low = _dot(h, wg_ref[0].astype(_bf16))
    g_hi, g_lo = _split_bf16(g_low)
    w_hi, w_lo = _split_bf16(wc_ref[0])
    logits = _dot(g_hi, w_hi) + _dot(g_lo, w_hi) + _dot(g_hi, w_lo) + bc_ref[0]
    log_sig = jnp.minimum(logits, 0.0) - jnp.log1p(jnp.exp(-jnp.abs(logits)))
    la_ref[0] = log_sig * (1.0 / GATE_LOGIT_NORMALIZER)


def _inproj(x, mod3, norm_g_l, wp_l, wq_l, wm_l, wg_l, wc_l, bc_l, tm):
    b, seq, d = x.shape
    row = lambda i, j: (i, j, 0)
    const3 = lambda i, j: (0, 0, 0)

    def out(width, dtype=_bf16):
        return (pl.BlockSpec((1, tm, width), row),
                jax.ShapeDtypeStruct((b, seq, width), dtype))

    outs = [out(1024), out(1024), out(1024), out(1024), out(1024), out(1024), out(1024, _f32)]
    return pl.pallas_call(
        _inproj_kernel,
        grid=(b, seq // tm),
        in_specs=[
            pl.BlockSpec((1, tm, d), row),
            pl.BlockSpec((1, 3, d), lambda i, j: (i, 0, 0)),
            pl.BlockSpec((1, 1, d), const3),
            pl.BlockSpec((1, d, 1024), const3),
            pl.BlockSpec((1, d, 1024), const3),
            pl.BlockSpec((1, d, D_MAIN), const3),
            pl.BlockSpec((1, d, 2 * GATE_RANK), const3),
            pl.BlockSpec((1, 2 * GATE_RANK, 2 * D_GLA_KEY), const3),
            pl.BlockSpec((1, 1, 2 * D_GLA_KEY), const3),
        ],
        out_specs=[o[0] for o in outs],
        out_shape=[o[1] for o in outs],
        compiler_params=pltpu.CompilerParams(
            dimension_semantics=("arbitrary", "arbitrary"),
            vmem_limit_bytes=VMEM_LIMIT_BYTES),
        name="inproj",
    )(x, mod3, norm_g_l, wp_l, wq_l, wm_l, wg_l, wc_l, bc_l)


def _seqdft_kernel(c_ref, s_ref, p_ref, q_ref, z_ref, o_ref):
    y = _dot(c_ref[...], p_ref[0]) + _dot(s_ref[...], q_ref[0])
    o_ref[0] = (y * _silu(z_ref[0].astype(_f32))).astype(_bf16)


def _seqdft(cos_m, nsin_m, p, q, z, tm):
    b, seq, n = p.shape
    return pl.pallas_call(
        _seqdft_kernel,
        grid=(b, seq // tm),
        in_specs=[
            pl.BlockSpec((tm, seq), lambda i, j: (j, 0)),
            pl.BlockSpec((tm, seq), lambda i, j: (j, 0)),
            pl.BlockSpec((1, seq, n), lambda i, j: (i, 0, 0)),
            pl.BlockSpec((1, seq, n), lambda i, j: (i, 0, 0)),
            pl.BlockSpec((1, tm, n), lambda i, j: (i, j, 0)),
        ],
        out_specs=pl.BlockSpec((1, tm, n), lambda i, j: (i, j, 0)),
        out_shape=jax.ShapeDtypeStruct((b, seq, n), _bf16),
        compiler_params=pltpu.CompilerParams(
            dimension_semantics=("arbitrary", "arbitrary"),
            vmem_limit_bytes=VMEM_LIMIT_BYTES),
        name="seqdft",
    )(cos_m, nsin_m, p, q, z)


def _gla_kernel(tri_ref, q_ref, k_ref, v_ref, la_ref, r_ref, g_ref, o_ref,
                oacc_ref, u_ref, qd_ref, dec_ref, s_ref, *, seq):
    n_blocks = seq // GLA_BLOCK
    nb = CHUNKS_PER_BLOCK
    scale = GLA_HEAD_K ** -0.5

    def chunked(a):
        return a.reshape(nb, CHUNK, a.shape[-1])

    def phase_a(blk, carry):
        r0 = pl.multiple_of(blk * GLA_BLOCK, GLA_BLOCK)
        rows = pl.ds(r0, GLA_BLOCK)
        la = la_ref[0, rows, :]
        la_hi, la_lo = _split_bf16(la)
        tri = tri_ref[...]
        pre = _dot(tri, la_hi) + _dot(tri, la_lo)
        cum_f = chunked(pre[:, :GLA_HEAD_K])
        pre_b = chunked(pre[:, GLA_HEAD_K:])
        cum_b = pre_b[:, CHUNK - 1:CHUNK, :] - pre_b + chunked(la[:, GLA_HEAD_K:])

        qs = chunked(q_ref[0, rows, :].astype(_f32) * scale)
        kk = chunked(k_ref[0, rows, :].astype(_f32))
        v_blk = v_ref[0, rows, :]

        row_i = lax.broadcasted_iota(jnp.int32, (GLA_BLOCK, GLA_BLOCK), 0)
        col_i = lax.broadcasted_iota(jnp.int32, (GLA_BLOCK, GLA_BLOCK), 1)
        same = (row_i >> 6) == (col_i >> 6)
        lane_chunk = lax.broadcasted_iota(jnp.int32, (GLA_HEAD_K, GLA_BLOCK), 1) >> 6

        lhs_parts = []
        kd_parts = []
        dec_rows = []
        for d, (cum, ref_i, last_i) in enumerate(
                ((cum_f, CHUNK // 2, CHUNK - 1), (cum_b, CHUNK // 2 - 1, 0))):
            ref = cum[:, ref_i:ref_i + 1, :]
            last = cum[:, last_i:last_i + 1, :]
            qi = (qs * jnp.exp(cum - ref)).reshape(GLA_BLOCK, GLA_HEAD_K).astype(_bf16)
            ki = (kk * jnp.exp(ref - cum)).reshape(GLA_BLOCK, GLA_HEAD_K)
            qd = (qs * jnp.exp(cum)).reshape(GLA_BLOCK, GLA_HEAD_K).astype(_bf16)
            kd = (kk * jnp.exp(last - cum)).reshape(GLA_BLOCK, GLA_HEAD_K)
            qd_ref[d, rows, :] = qd
            scores = _dot(qi, ki.T.astype(_bf16))
            keep = same & ((col_i <= row_i) if d == 0 else (col_i > row_i))
            lhs_parts.append(jnp.where(keep, scores, 0.0).astype(_bf16))
            kd_t = kd.T
            for c in range(nb):
                kd_parts.append(jnp.where(lane_chunk == c, kd_t, 0.0).astype(_bf16))
            dec_rows.append(jnp.exp(last).reshape(nb, GLA_HEAD_K))

        lhs = jnp.concatenate(lhs_parts + kd_parts, axis=0)
        res = _dot(lhs, v_blk)
        oacc_ref[rows, :] = res[0:GLA_BLOCK] + res[GLA_BLOCK:2 * GLA_BLOCK]
        base = 2 * GLA_BLOCK
        for d in range(2):
            for c in range(nb):
                off = base + (d * nb + c) * GLA_HEAD_K
                u_ref[d, blk * nb + c] = res[off:off + GLA_HEAD_K]
        dec = jnp.concatenate(
            dec_rows + [jnp.zeros((GLA_HEAD_K - 2 * nb, GLA_HEAD_K), _f32)], axis=0)
        dec_ref[blk] = dec.T
        return carry

    lax.fori_loop(0, n_blocks, phase_a, 0)

    s_ref[...] = jnp.zeros_like(s_ref)

    def phase_b(i, carry):
        for d in range(2):
            blk = i if d == 0 else n_blocks - 1 - i
            for j in range(nb):
                c = j if d == 0 else nb - 1 - j
                chunk = blk * nb + c
                r0 = pl.multiple_of(chunk * CHUNK, CHUNK)
                rows = pl.ds(r0, CHUNK)
                s = s_ref[d]
                oacc_ref[rows, :] += _dot(qd_ref[d, rows, :], s.astype(_bf16))
                col = d * nb + c
                s_ref[d] = dec_ref[blk][:, col:col + 1] * s + u_ref[d, chunk]
        return carry

    lax.fori_loop(0, n_blocks, phase_b, 0)

    def finish(blk, carry):
        r0 = pl.multiple_of(blk * GLA_BLOCK, GLA_BLOCK)
        rows = pl.ds(r0, GLA_BLOCK)
        o = oacc_ref[rows, :]
        o = o * lax.rsqrt(jnp.mean(o * o, axis=-1, keepdims=True) + EPS) * g_ref[...]
        o_ref[0, rows, :] = (o * _silu(r_ref[0, rows, :].astype(_f32))).astype(_bf16)
        return carry

    lax.fori_loop(0, n_blocks, finish, 0)


def _gla(tri, qk, v, la, r, gnorm_l):
    b, seq, _ = v.shape
    h = N_GLA_HEADS
    n_chunks = seq // CHUNK
    return pl.pallas_call(
        functools.partial(_gla_kernel, seq=seq),
        grid=(b, h),
        in_specs=[
            pl.BlockSpec((GLA_BLOCK, GLA_BLOCK), lambda i, j: (0, 0)),
            pl.BlockSpec((1, seq, GLA_HEAD_K), lambda i, j: (i, 0, j)),
            pl.BlockSpec((1, seq, GLA_HEAD_K), lambda i, j: (i, 0, h + j)),
            pl.BlockSpec((1, seq, GLA_HEAD_V), lambda i, j: (i, 0, j)),
            pl.BlockSpec((1, seq, 2 * GLA_HEAD_K), lambda i, j: (i, 0, j)),
            pl.BlockSpec((1, seq, GLA_HEAD_V), lambda i, j: (i, 0, j)),
            pl.BlockSpec((1, GLA_HEAD_V), lambda i, j: (0, j)),
        ],
        out_specs=pl.BlockSpec((1, seq, GLA_HEAD_V), lambda i, j: (i, 0, j)),
        out_shape=jax.ShapeDtypeStruct((b, seq, D_GLA), _bf16),
        scratch_shapes=[
            pltpu.VMEM((seq, GLA_HEAD_V), _f32),
            pltpu.VMEM((2, n_chunks, GLA_HEAD_K, GLA_HEAD_V), _f32),
            pltpu.VMEM((2, seq, GLA_HEAD_K), _bf16),
            pltpu.VMEM((seq // GLA_BLOCK, GLA_HEAD_K, GLA_HEAD_K), _f32),
            pltpu.VMEM((2, GLA_HEAD_K, GLA_HEAD_V), _f32),
        ],
        compiler_params=pltpu.CompilerParams(
            dimension_semantics=("arbitrary", "arbitrary"),
            vmem_limit_bytes=VMEM_LIMIT_BYTES),
        name="gla",
    )(tri, qk, qk, v, la, r, gnorm_l)


def _outproj_kernel(x_ref, yf_ref, yg_ref, w_ref, mod_ref, fg_ref, o_ref, *, final):
    y = _dot(yf_ref[0], w_ref[0, 0:D_FOURIER, :]) + _dot(yg_ref[0], w_ref[0, D_FOURIER:, :])
    gate = mod_ref[0][2:3, :]
    xn = x_ref[0] + gate * y
    if final:
        ms = jnp.mean(xn * xn, axis=-1, keepdims=True)
        xn = xn * lax.rsqrt(ms + EPS) * fg_ref[...]
    o_ref[0] = xn


def _outproj(x, yf, yg, w_l, mod3, final_g, tm, final):
    b, seq, d = x.shape
    row = lambda i, j: (i, j, 0)
    return pl.pallas_call(
        functools.partial(_outproj_kernel, final=final),
        grid=(b, seq // tm),
        in_specs=[
            pl.BlockSpec((1, tm, d), row),
            pl.BlockSpec((1, tm, D_FOURIER), row),
            pl.BlockSpec((1, tm, D_GLA), row),
            pl.BlockSpec((1, D_FOURIER + D_GLA, d), lambda i, j: (0, 0, 0)),
            pl.BlockSpec((1, 3, d), lambda i, j: (i, 0, 0)),
            pl.BlockSpec((1, d), lambda i, j: (0, 0)),
        ],
        out_specs=pl.BlockSpec((1, tm, d), row),
        out_shape=jax.ShapeDtypeStruct((b, seq, d), _f32),
        compiler_params=pltpu.CompilerParams(
            dimension_semantics=("arbitrary", "arbitrary"),
            vmem_limit_bytes=VMEM_LIMIT_BYTES),
        name="outproj",
    )(x, yf, yg, w_l, mod3, final_g)


def _dft_matrices(seq):
    idx = np.arange(seq)
    ang = 2.0 * np.pi * ((idx[:, None] * idx[None, :]) % seq) / seq
    return jnp.asarray(np.cos(ang), _f32).astype(_bf16), jnp.asarray(-np.sin(ang), _f32).astype(_bf16)


def _block_tri():
    idx = np.arange(GLA_BLOCK)
    same = (idx[:, None] // CHUNK) == (idx[None, :] // CHUNK)
    return jnp.asarray((same & (idx[None, :] <= idx[:, None])).astype(np.float32), _bf16)


def kernel(x, c, norm_g, w_ada, b_ada, w_in, w_fmap, w_af, b_af, w_ab, b_ab, gla_norm_g, w_out, final_g):
    depth = w_in.shape[0]
    b, seq, d = x.shape
    assert d == D_MODEL and seq % GLA_BLOCK == 0

    mod = _ada_mod(c, w_ada, b_ada).reshape(depth, b, 3, d)
    wp, wq = _fold_fourier_weights(w_in, w_fmap, seq)
    w_main = w_in[:, :, _OFF_Z:_OFF_MAIN_END].astype(_bf16)
    w_gate = w_in[:, :, _OFF_MAIN_END:]
    w_out_b = w_out.astype(_bf16)

    zeros = jnp.zeros_like(w_af)
    wc = jnp.concatenate([jnp.concatenate([w_af, zeros], axis=1),
                          jnp.concatenate([zeros, w_ab], axis=1)], axis=-1)
    bc = jnp.concatenate([b_af, b_ab], axis=-1)

    def per_head(a):
        lead = a.shape[:-1]
        a = a.reshape(lead + (2, N_GLA_HEADS, GLA_HEAD_K))
        return jnp.swapaxes(a, -3, -2).reshape(lead + (2 * D_GLA_KEY,))

    wc = per_head(wc)
    bc = per_head(bc).reshape(depth, 1, 2 * D_GLA_KEY)

    cos_m, nsin_m = _dft_matrices(seq)
    tri = _block_tri()
    norm_g3 = norm_g.reshape(depth, 1, d)
    final_g2 = final_g.reshape(1, d)

    for l in range(depth):
        sl = slice(l, l + 1)
        p, q, z, qk, v, r, la = _inproj(x, mod[l], norm_g3[sl], wp[sl], wq[sl], w_main[sl],
                                        w_gate[sl], wc[sl], bc[sl], tm=256)
        y_f = _seqdft(cos_m, nsin_m, p, q, z, tm=512)
        y_g = _gla(tri, qk, v, la, r, gla_norm_g[sl])
        x = _outproj(x, y_f, y_g, w_out_b[sl], mod[l], final_g2, tm=512, final=(l == depth - 1))
    return x
```

```python
import functools

import numpy as np
import jax
import jax.numpy as jnp
from jax import lax
from jax.experimental import pallas as pl
from jax.experimental.pallas import tpu as pltpu

D_MODEL = 1024
D_FOURIER = 1024
N_FOURIER_GROUPS = 4
FOURIER_GROUP = 256
N_GLA_HEADS = 4
D_GLA_KEY = 512
D_GLA = 1024
GLA_HEAD_K = 128
GLA_HEAD_V = 256
GATE_RANK = 16
GATE_LOGIT_NORMALIZER = 16.0
CHUNK = 64
EPS = 1e-6
LOG2_E = 1.4426950408889634

_OFF_Z = D_FOURIER
_OFF_MAIN_END = 2 * D_FOURIER + 2 * D_GLA_KEY + 2 * D_GLA
D_MAIN = _OFF_MAIN_END - _OFF_Z

GLA_BLOCK = 256
CHUNKS_PER_BLOCK = GLA_BLOCK // CHUNK

VMEM_LIMIT_BYTES = 56 * 1024 * 1024

_f32 = jnp.float32
_bf16 = jnp.bfloat16


def _dot(a, b):
    return jnp.dot(a, b, preferred_element_type=_f32)


def _split_bf16(a):
    hi = a.astype(_bf16)
    lo = (a - hi.astype(_f32)).astype(_bf16)
    return hi, lo


def _silu(v):
    return v * (1.0 / (1.0 + jnp.exp(-v)))


def _ada_kernel(c_ref, w_ref, b_ref, o_ref):
    c_act = _silu(c_ref[...]).astype(_bf16)
    o_ref[0] = _dot(c_act, w_ref[0].astype(_bf16)) + b_ref[0]


def _ada_mod(c, w_ada, b_ada):
    depth = w_ada.shape[0]
    b = c.shape[0]
    n_tiles = 3
    return pl.pallas_call(
        _ada_kernel,
        grid=(depth, n_tiles),
        in_specs=[
            pl.BlockSpec((b, D_MODEL), lambda l, j: (0, 0)),
            pl.BlockSpec((1, D_MODEL, D_MODEL), lambda l, j: (l, 0, j)),
            pl.BlockSpec((1, 1, D_MODEL), lambda l, j: (l, 0, j)),
        ],
        out_specs=pl.BlockSpec((1, b, D_MODEL), lambda l, j: (l, 0, j)),
        out_shape=jax.ShapeDtypeStruct((depth, b, 3 * D_MODEL), _f32),
        compiler_params=pltpu.CompilerParams(
            dimension_semantics=("arbitrary", "arbitrary"),
            vmem_limit_bytes=VMEM_LIMIT_BYTES),
        name="ada_mod",
    )(c, w_ada, b_ada.reshape(depth, 1, 3 * D_MODEL))


def _fold_kernel(wu_ref, wf_ref, cc_ref, sc_ref, wp_ref, wq_ref, *, norm):
    hp = lax.Precision.HIGHEST
    wf = wf_ref[0, 0]
    a = jnp.dot(cc_ref[...], wf, precision=hp, preferred_element_type=_f32) * norm
    bm = jnp.dot(sc_ref[...], wf, precision=hp, preferred_element_type=_f32) * norm
    wu = wu_ref[0]
    wp_ref[0] = jnp.dot(wu, a, precision=hp, preferred_element_type=_f32).astype(_bf16)
    wq_ref[0] = jnp.dot(wu, bm, precision=hp, preferred_element_type=_f32).astype(_bf16)


def _fold_fourier_weights(w_in, w_fmap, seq_len):
    depth = w_in.shape[0]
    idx = np.arange(FOURIER_GROUP)
    ang = 2.0 * np.pi * ((idx[:, None] * idx[None, :]) % FOURIER_GROUP) / FOURIER_GROUP
    cc = jnp.asarray(np.cos(ang), _f32)
    sc = jnp.asarray(np.sin(ang), _f32)
    norm = float(1.0 / np.sqrt(seq_len * FOURIER_GROUP))
    gspec = pl.BlockSpec((FOURIER_GROUP, FOURIER_GROUP), lambda l, g: (0, 0))
    ospec = pl.BlockSpec((1, D_MODEL, FOURIER_GROUP), lambda l, g: (l, 0, g))
    oshape = jax.ShapeDtypeStruct((depth, D_MODEL, D_FOURIER), _bf16)
    return pl.pallas_call(
        functools.partial(_fold_kernel, norm=norm),
        grid=(depth, N_FOURIER_GROUPS),
        in_specs=[
            pl.BlockSpec((1, D_MODEL, FOURIER_GROUP), lambda l, g: (l, 0, g)),
            pl.BlockSpec((1, 1, FOURIER_GROUP, FOURIER_GROUP), lambda l, g: (l, g, 0, 0)),
            gspec, gspec,
        ],
        out_specs=[ospec, ospec],
        out_shape=[oshape, oshape],
        compiler_params=pltpu.CompilerParams(
            dimension_semantics=("arbitrary", "arbitrary"),
            vmem_limit_bytes=VMEM_LIMIT_BYTES),
        name="fold_fourier",
    )(w_in, w_fmap, cc, sc)


def _inproj_kernel(x_ref, mod_ref, g_ref, wp_ref, wq_ref, wm_ref, wg_ref, wc_ref, bc_ref,
                   p_ref, q_ref, z_ref, qk_ref, v_ref, r_ref, la_ref):
    x = x_ref[0]
    ms = jnp.mean(x * x, axis=-1, keepdims=True)
    y = x * lax.rsqrt(ms + EPS) * g_ref[0]
    mod = mod_ref[0]
    shift = mod[0:1, :]
    scale = mod[1:2, :]
    h = (y * (1.0 + scale) + shift).astype(_bf16)

    r = 2 * GATE_RANK
    g3 = _dot(h, wg_ref[0].astype(_bf16))
    g_hi, g_lo = _split_bf16(g3)
    lane = lax.broadcasted_iota(jnp.int32, g3.shape, 1)
    lhs = jnp.where((lane >= r) & (lane < 2 * r), g_lo, g_hi)
    w_hi, w_lo = _split_bf16(wc_ref[0])
    rhs = jnp.concatenate([w_hi, w_hi, w_lo, jnp.zeros_like(w_hi)], axis=0)
    logits = _dot(lhs, rhs) + bc_ref[0]
    log_sig = jnp.minimum(logits, 0.0) - jnp.log1p(jnp.exp(-jnp.abs(logits)))
    la_ref[0] = log_sig * (LOG2_E / GATE_LOGIT_NORMALIZER)

    p_ref[0] = _dot(h, wp_ref[0]).astype(_bf16)
    q_ref[0] = _dot(h, wq_ref[0]).astype(_bf16)
    z_ref[0] = _dot(h, wm_ref[0, :, 0:1024]).astype(_bf16)
    qk_ref[0] = _dot(h, wm_ref[0, :, 1024:2048]).astype(_bf16)
    v_ref[0] = _dot(h, wm_ref[0, :, 2048:3072]).astype(_bf16)
    r_ref[0] = _dot(h, wm_ref[0, :, 3072:4096]).astype(_bf16)


def _inproj(x, mod3, norm_g_l, wp_l, wq_l, wm_l, wg_l, wc_l, bc_l, tm):
    b, seq, d = x.shape
    row = lambda i, j: (i, j, 0)
    const3 = lambda i, j: (0, 0, 0)

    def out(width, dtype=_bf16):
        return (pl.BlockSpec((1, tm, width), row),
                jax.ShapeDtypeStruct((b, seq, width), dtype))

    outs = [out(1024), out(1024), out(1024), out(1024), out(1024), out(1024), out(1024, _f32)]
    return pl.pallas_call(
        _inproj_kernel,
        grid=(b, seq // tm),
        in_specs=[
            pl.BlockSpec((1, tm, d), row),
            pl.BlockSpec((1, 3, d), lambda i, j: (i, 0, 0)),
            pl.BlockSpec((1, 1, d), const3),
            pl.BlockSpec((1, d, 1024), const3, pipeline_mode=pl.Buffered(1)),
            pl.BlockSpec((1, d, 1024), const3, pipeline_mode=pl.Buffered(1)),
            pl.BlockSpec((1, d, D_MAIN), const3, pipeline_mode=pl.Buffered(1)),
            pl.BlockSpec((1, d, 8 * GATE_RANK), const3),
            pl.BlockSpec((1, 2 * GATE_RANK, 2 * D_GLA_KEY), const3),
            pl.BlockSpec((1, 1, 2 * D_GLA_KEY), const3),
        ],
        out_specs=[o[0] for o in outs],
        out_shape=[o[1] for o in outs],
        compiler_params=pltpu.CompilerParams(
            dimension_semantics=("arbitrary", "arbitrary"),
            vmem_limit_bytes=VMEM_LIMIT_BYTES),
        name="inproj",
    )(x, mod3, norm_g_l, wp_l, wq_l, wm_l, wg_l, wc_l, bc_l)


def _seqdft_kernel(c_ref, s_ref, p_ref, q_ref, z_ref, o_ref):
    y = _dot(c_ref[...], p_ref[0]) + _dot(s_ref[...], q_ref[0])
    o_ref[0] = (y * _silu(z_ref[0].astype(_f32))).astype(_bf16)


def _seqdft(cos_m, nsin_m, p, q, z, tm):
    b, seq, n = p.shape
    return pl.pallas_call(
        _seqdft_kernel,
        grid=(b, seq // tm),
        in_specs=[
            pl.BlockSpec((tm, seq), lambda i, j: (j, 0)),
            pl.BlockSpec((tm, seq), lambda i, j: (j, 0)),
            pl.BlockSpec((1, seq, n), lambda i, j: (i, 0, 0)),
            pl.BlockSpec((1, seq, n), lambda i, j: (i, 0, 0)),
            pl.BlockSpec((1, tm, n), lambda i, j: (i, j, 0)),
        ],
        out_specs=pl.BlockSpec((1, tm, n), lambda i, j: (i, j, 0)),
        out_shape=jax.ShapeDtypeStruct((b, seq, n), _bf16),
        compiler_params=pltpu.CompilerParams(
            dimension_semantics=("arbitrary", "arbitrary"),
            vmem_limit_bytes=VMEM_LIMIT_BYTES),
        name="seqdft",
    )(cos_m, nsin_m, p, q, z)


def _gla_kernel(tri_ref, q_ref, k_ref, v_ref, la_ref, r_ref, g_ref, o_ref,
                oacc_ref, u_ref, qd_ref, dec_ref, s_ref, *, seq):
    n_blocks = seq // GLA_BLOCK
    nb = CHUNKS_PER_BLOCK
    dk = GLA_HEAD_K
    scale = dk ** -0.5

    def chunked(a):
        return a.reshape(nb, CHUNK, a.shape[-1])

    def flat(a):
        return a.reshape(a.shape[0] * a.shape[1], a.shape[2])

    lane = lax.broadcasted_iota(jnp.int32, (CHUNK, 2 * CHUNK), 1)
    row_l = lax.broadcasted_iota(jnp.int32, (CHUNK, 2 * CHUNK), 0)
    keep = (((lane >= CHUNK) | (lane <= row_l), (lane < CHUNK) | (lane - CHUNK <= row_l)),
            ((lane >= CHUNK) | (lane > row_l), (lane < CHUNK) | (lane - CHUNK > row_l)))

    def operands(blk):
        r0 = pl.multiple_of(blk * GLA_BLOCK, GLA_BLOCK)
        rows = pl.ds(r0, GLA_BLOCK)
        la = la_ref[0, rows, :]
        la_hi, la_lo = _split_bf16(la)
        tri = tri_ref[...]
        pre = _dot(tri, la_hi) + _dot(tri, la_lo)
        pre_b = pre[:, dk:]
        g_f = chunked(pre[:, :dk])
        g_b = chunked(pre_b[GLA_BLOCK - 1:GLA_BLOCK, :] - pre_b + la[:, dk:])
        zero = jnp.zeros((1, 1, dk), _f32)
        edge_f = g_f[:, CHUNK - 1:CHUNK, :]
        edge_b = g_b[:, 0:1, :]
        dirs = ((g_f, edge_f, jnp.concatenate([zero, edge_f[:nb - 1]], axis=0),
                 CHUNK // 2, list(range(nb))),
                (g_b, edge_b, jnp.concatenate([edge_b[1:], zero], axis=0),
                 CHUNK // 2 - 1, list(range(nb - 1, -1, -1))))

        qs = chunked(q_ref[0, rows, :].astype(_f32) * scale)
        kk = chunked(k_ref[0, rows, :].astype(_f32))
        zeros_chunk = jnp.zeros((CHUNK, dk), _f32)
        per_dir = []
        for d, (g, edge, base, ref_i, order) in enumerate(dirs):
            tot = edge[order[-1]:order[-1] + 1]
            cum = g - base
            ref = cum[:, ref_i:ref_i + 1, :]
            a = cum - ref
            qi = qs * jnp.exp2(a)
            ki = kk * jnp.exp2(-a)
            last_ref = (edge - base) - ref
            qd_ref[d, rows, :] = flat(qi * jnp.exp2(ref + base)).astype(_bf16)
            kd_t = flat(ki * jnp.exp2(last_ref + (tot - edge))).T.astype(_bf16)
            dec_ref[d, blk] = jnp.broadcast_to(jnp.exp2(tot).reshape(1, dk), (dk, dk)).T
            rhs_t = []
            for i in range(nb):
                pieces = []
                for c in range(nb):
                    if c == i:
                        pieces.append(ki[c])
                    elif order.index(c) < order.index(i):
                        pieces.append(ki[c] * jnp.exp2(last_ref[c] + ref[i] + base[i] - edge[c]))
                    else:
                        pieces.append(zeros_chunk)
                rhs_t.append(jnp.concatenate(pieces, axis=0).T.astype(_bf16))
            per_dir.append((flat(qi).astype(_bf16), rhs_t, kd_t))
        return blk, rows, per_dir

    def scores(ops):
        _, _, per_dir = ops
        return [[_dot(qi[i * CHUNK:(i + 1) * CHUNK], rhs_t[i]) for i in range(nb)]
                for qi, rhs_t, _ in per_dir]

    def outputs(ops, mats):
        blk, rows, per_dir = ops
        score_parts = []
        for d, s_rows in enumerate(mats):
            for i, s_i in enumerate(s_rows):
                tile, half = divmod(i, 2)
                lo, hi = tile * 2 * CHUNK, (tile + 1) * 2 * CHUNK
                masked = jnp.where(keep[d][half], s_i[:, lo:hi], 0.0)
                s_i = jnp.concatenate([masked, s_i[:, hi:]] if tile == 0 else [s_i[:, :lo], masked], axis=1)
                score_parts.append(s_i.astype(_bf16))
        lhs = jnp.concatenate(score_parts + [p[2] for p in per_dir], axis=0)
        res = _dot(lhs, v_ref[0, rows, :])
        oacc_ref[rows, :] = res[0:GLA_BLOCK] + res[GLA_BLOCK:2 * GLA_BLOCK]
        u_ref[0, blk] = res[2 * GLA_BLOCK:2 * GLA_BLOCK + dk]
        u_ref[1, blk] = res[2 * GLA_BLOCK + dk:2 * GLA_BLOCK + 2 * dk]

    def phase_a(i, carry):
        ops0 = operands(2 * i)
        ops1 = operands(2 * i + 1)
        mats0 = scores(ops0)
        mats1 = scores(ops1)
        outputs(ops0, mats0)
        outputs(ops1, mats1)
        return carry

    lax.fori_loop(0, n_blocks // 2, phase_a, 0)

    s_ref[...] = jnp.zeros_like(s_ref)

    def phase_b(i, carry):
        for d in range(2):
            blk = i if d == 0 else n_blocks - 1 - i
            r0 = pl.multiple_of(blk * GLA_BLOCK, GLA_BLOCK)
            rows = pl.ds(r0, GLA_BLOCK)
            s = s_ref[d]
            oacc_ref[rows, :] += _dot(qd_ref[d, rows, :], s.astype(_bf16))
            dec = dec_ref[d, blk]
            s_ref[d] = jnp.concatenate([dec, dec], axis=1) * s + u_ref[d, blk]
        return carry

    lax.fori_loop(0, n_blocks, phase_b, 0, unroll=4)

    def finish(blk, carry):
        r0 = pl.multiple_of(blk * GLA_BLOCK, GLA_BLOCK)
        rows = pl.ds(r0, GLA_BLOCK)
        o = oacc_ref[rows, :]
        o = o * lax.rsqrt(jnp.mean(o * o, axis=-1, keepdims=True) + EPS) * g_ref[...]
        o_ref[0, rows, :] = (o * _silu(r_ref[0, rows, :].astype(_f32))).astype(_bf16)
        return carry

    lax.fori_loop(0, n_blocks, finish, 0)


def _gla(tri, qk, v, la, r, gnorm_l):
    b, seq, _ = v.shape
    h = N_GLA_HEADS
    n_blocks = seq // GLA_BLOCK
    return pl.pallas_call(
        functools.partial(_gla_kernel, seq=seq),
        grid=(b, h),
        in_specs=[
            pl.BlockSpec((GLA_BLOCK, GLA_BLOCK), lambda i, j: (0, 0)),
            pl.BlockSpec((1, seq, GLA_HEAD_K), lambda i, j: (i, 0, j)),
            pl.BlockSpec((1, seq, GLA_HEAD_K), lambda i, j: (i, 0, h + j)),
            pl.BlockSpec((1, seq, GLA_HEAD_V), lambda i, j: (i, 0, j)),
            pl.BlockSpec((1, seq, 2 * GLA_HEAD_K), lambda i, j: (i, 0, j)),
            pl.BlockSpec((1, seq, GLA_HEAD_V), lambda i, j: (i, 0, j)),
            pl.BlockSpec((1, GLA_HEAD_V), lambda i, j: (0, j)),
        ],
        out_specs=pl.BlockSpec((1, seq, GLA_HEAD_V), lambda i, j: (i, 0, j)),
        out_shape=jax.ShapeDtypeStruct((b, seq, D_GLA), _bf16),
        scratch_shapes=[
            pltpu.VMEM((seq, GLA_HEAD_V), _f32),
            pltpu.VMEM((2, n_blocks, GLA_HEAD_K, GLA_HEAD_V), _f32),
            pltpu.VMEM((2, seq, GLA_HEAD_K), _bf16),
            pltpu.VMEM((2, n_blocks, GLA_HEAD_K, GLA_HEAD_K), _f32),
            pltpu.VMEM((2, GLA_HEAD_K, GLA_HEAD_V), _f32),
        ],
        compiler_params=pltpu.CompilerParams(
            dimension_semantics=("arbitrary", "arbitrary"),
            vmem_limit_bytes=VMEM_LIMIT_BYTES),
        name="gla",
    )(tri, qk, qk, v, la, r, gnorm_l)


def _outproj_kernel(x_ref, yf_ref, yg_ref, w_ref, mod_ref, fg_ref, o_ref, *, final):
    y = _dot(yf_ref[0], w_ref[0, 0:D_FOURIER, :]) + _dot(yg_ref[0], w_ref[0, D_FOURIER:, :])
    gate = mod_ref[0][2:3, :]
    xn = x_ref[0] + gate * y
    if final:
        ms = jnp.mean(xn * xn, axis=-1, keepdims=True)
        xn = xn * lax.rsqrt(ms + EPS) * fg_ref[...]
    o_ref[0] = xn


def _outproj(x, yf, yg, w_l, mod3, final_g, tm, final):
    b, seq, d = x.shape
    row = lambda i, j: (i, j, 0)
    return pl.pallas_call(
        functools.partial(_outproj_kernel, final=final),
        grid=(b, seq // tm),
        in_specs=[
            pl.BlockSpec((1, tm, d), row),
            pl.BlockSpec((1, tm, D_FOURIER), row),
            pl.BlockSpec((1, tm, D_GLA), row),
            pl.BlockSpec((1, D_FOURIER + D_GLA, d), lambda i, j: (0, 0, 0)),
            pl.BlockSpec((1, 3, d), lambda i, j: (i, 0, 0)),
            pl.BlockSpec((1, d), lambda i, j: (0, 0)),
        ],
        out_specs=pl.BlockSpec((1, tm, d), row),
        out_shape=jax.ShapeDtypeStruct((b, seq, d), _f32),
        compiler_params=pltpu.CompilerParams(
            dimension_semantics=("arbitrary", "arbitrary"),
            vmem_limit_bytes=VMEM_LIMIT_BYTES),
        name="outproj",
    )(x, yf, yg, w_l, mod3, final_g)


def _dft_matrices(seq):
    idx = np.arange(seq)
    ang = 2.0 * np.pi * ((idx[:, None] * idx[None, :]) % seq) / seq
    return jnp.asarray(np.cos(ang), _f32).astype(_bf16), jnp.asarray(-np.sin(ang), _f32).astype(_bf16)


def _block_tri():
    idx = np.arange(GLA_BLOCK)
    return jnp.asarray((idx[None, :] <= idx[:, None]).astype(np.float32), _bf16)


def kernel(x, c, norm_g, w_ada, b_ada, w_in, w_fmap, w_af, b_af, w_ab, b_ab, gla_norm_g, w_out, final_g):
    depth = w_in.shape[0]
    b, seq, d = x.shape
    assert d == D_MODEL and seq % GLA_BLOCK == 0

    mod = _ada_mod(c, w_ada, b_ada).reshape(depth, b, 3, d)
    wp, wq = _fold_fourier_weights(w_in, w_fmap, seq)
    w_main = w_in[:, :, _OFF_Z:_OFF_MAIN_END].astype(_bf16)
    w_low = w_in[:, :, _OFF_MAIN_END:]
    w_gate = jnp.concatenate([w_low, w_low, w_low, jnp.zeros_like(w_low)], axis=-1)
    w_out_b = w_out.astype(_bf16)

    zeros = jnp.zeros_like(w_af)
    wc = jnp.concatenate([jnp.concatenate([w_af, zeros], axis=1),
                          jnp.concatenate([zeros, w_ab], axis=1)], axis=-1)
    bc = jnp.concatenate([b_af, b_ab], axis=-1)

    def per_head(a):
        lead = a.shape[:-1]
        a = a.reshape(lead + (2, N_GLA_HEADS, GLA_HEAD_K))
        return jnp.swapaxes(a, -3, -2).reshape(lead + (2 * D_GLA_KEY,))

    wc = per_head(wc)
    bc = per_head(bc).reshape(depth, 1, 2 * D_GLA_KEY)

    cos_m, nsin_m = _dft_matrices(seq)
    tri = _block_tri()
    norm_g3 = norm_g.reshape(depth, 1, d)
    final_g2 = final_g.reshape(1, d)

    for l in range(depth):
        sl = slice(l, l + 1)
        p, q, z, qk, v, r, la = _inproj(x, mod[l], norm_g3[sl], wp[sl], wq[sl], w_main[sl],
                                        w_gate[sl], wc[sl], bc[sl], tm=512)
        y_f = _seqdft(cos_m, nsin_m, p, q, z, tm=512)
        y_g = _gla(tri, qk, v, la, r, gla_norm_g[sl])
        x = _outproj(x, y_f, y_g, w_out_b[sl], mod[l], final_g2, tm=512, final=(l == depth - 1))
    return x
```

```python
import functools

import numpy as np
import jax
import jax.numpy as jnp
from jax import lax
from jax.experimental import pallas as pl
from jax.experimental.pallas import tpu as pltpu

D_MODEL = 1024
D_FOURIER = 1024
N_FOURIER_GROUPS = 4
FOURIER_GROUP = 256
N_GLA_HEADS = 4
D_GLA_KEY = 512
D_GLA = 1024
GLA_HEAD_K = 128
GLA_HEAD_V = 256
GATE_RANK = 16
GATE_LOGIT_NORMALIZER = 16.0
CHUNK = 64
EPS = 1e-6
LOG2_E = 1.4426950408889634

_OFF_Z = D_FOURIER
_OFF_QK = 2 * D_FOURIER
_OFF_V = _OFF_QK + 2 * D_GLA_KEY
_OFF_R = _OFF_V + D_GLA
D_MAIN = _OFF_R + D_GLA

GLA_BLOCK = 256
CHUNKS_PER_BLOCK = GLA_BLOCK // CHUNK

FFT_RADIX = 8
FFT_ROWS = 16
LANES = 128

VMEM_LIMIT_BYTES = 56 * 1024 * 1024

_f32 = jnp.float32
_bf16 = jnp.bfloat16


def _dot(a, b):
    return jnp.dot(a, b, preferred_element_type=_f32)


def _split_bf16(a):
    hi = a.astype(_bf16)
    lo = (a - hi.astype(_f32)).astype(_bf16)
    return hi, lo


def _silu(v):
    return v * (1.0 / (1.0 + jnp.exp(-v)))


def _params():
    return pltpu.CompilerParams(dimension_semantics=("arbitrary", "arbitrary"),
                                vmem_limit_bytes=VMEM_LIMIT_BYTES)


def _ada_kernel(c_ref, w_ref, b_ref, o_ref):
    c_act = _silu(c_ref[...]).astype(_bf16)
    o_ref[0] = _dot(c_act, w_ref[0].astype(_bf16)) + b_ref[0]


def _ada_mod(c, w_ada, b_ada):
    depth = w_ada.shape[0]
    b = c.shape[0]
    n_tiles = 3
    return pl.pallas_call(
        _ada_kernel,
        grid=(depth, n_tiles),
        in_specs=[
            pl.BlockSpec((b, D_MODEL), lambda l, j: (0, 0)),
            pl.BlockSpec((1, D_MODEL, D_MODEL), lambda l, j: (l, 0, j)),
            pl.BlockSpec((1, 1, D_MODEL), lambda l, j: (l, 0, j)),
        ],
        out_specs=pl.BlockSpec((1, b, D_MODEL), lambda l, j: (l, 0, j)),
        out_shape=jax.ShapeDtypeStruct((depth, b, 3 * D_MODEL), _f32),
        compiler_params=_params(),
        name="ada_mod",
    )(c, w_ada, b_ada.reshape(depth, 1, 3 * D_MODEL))


def _fold_kernel(wf_ref, cc_ref, sc_ref, ab_ref, *, norm):
    hp = lax.Precision.HIGHEST
    wf = wf_ref[0, 0]
    a = jnp.dot(cc_ref[...], wf, precision=hp, preferred_element_type=_f32) * norm
    bm = jnp.dot(sc_ref[...], wf, precision=hp, preferred_element_type=_f32) * norm
    ab_ref[0, 0] = jnp.concatenate([a, bm], axis=1).astype(_bf16)


def _fold_fourier_weights(w_fmap, seq_len):
    depth = w_fmap.shape[0]
    idx = np.arange(FOURIER_GROUP)
    ang = 2.0 * np.pi * ((idx[:, None] * idx[None, :]) % FOURIER_GROUP) / FOURIER_GROUP
    cc = jnp.asarray(np.cos(ang), _f32)
    sc = jnp.asarray(np.sin(ang), _f32)
    norm = float(1.0 / np.sqrt(seq_len * FOURIER_GROUP))
    gspec = pl.BlockSpec((FOURIER_GROUP, FOURIER_GROUP), lambda l, g: (0, 0))
    return pl.pallas_call(
        functools.partial(_fold_kernel, norm=norm),
        grid=(depth, N_FOURIER_GROUPS),
        in_specs=[
            pl.BlockSpec((1, 1, FOURIER_GROUP, FOURIER_GROUP), lambda l, g: (l, g, 0, 0)),
            gspec, gspec,
        ],
        out_specs=pl.BlockSpec((1, 1, FOURIER_GROUP, 2 * FOURIER_GROUP), lambda l, g: (l, g, 0, 0)),
        out_shape=jax.ShapeDtypeStruct((depth, N_FOURIER_GROUPS, FOURIER_GROUP, 2 * FOURIER_GROUP), _bf16),
        compiler_params=_params(),
        name="fold_fourier",
    )(w_fmap, cc, sc)


def _inproj_kernel(x_ref, mod_ref, g_ref, w_ref, ab_ref, wg_ref, wc_ref, bc_ref,
                   p_ref, q_ref, z_ref, qk_ref, v_ref, r_ref, la_ref):
    x = x_ref[0]
    ms = jnp.mean(x * x, axis=-1, keepdims=True)
    y = x * lax.rsqrt(ms + EPS) * g_ref[0]
    mod = mod_ref[0, 0]
    shift = mod[0:1, :]
    scale = mod[1:2, :]
    h = (y * (1.0 + scale) + shift).astype(_bf16)

    rank2 = 2 * GATE_RANK
    g3 = _dot(h, wg_ref[0].astype(_bf16))
    g_hi, g_lo = _split_bf16(g3)
    lane = lax.broadcasted_iota(jnp.int32, g3.shape, 1)
    lhs = jnp.where((lane >= rank2) & (lane < 2 * rank2), g_lo, g_hi)
    w_hi, w_lo = _split_bf16(wc_ref[0])
    rhs = jnp.concatenate([w_hi, w_hi, w_lo, jnp.zeros_like(w_hi)], axis=0)
    logits = _dot(lhs, rhs) + bc_ref[0]
    log_sig = jnp.minimum(logits, 0.0) - jnp.log1p(jnp.exp(-jnp.abs(logits)))
    la_ref[0] = log_sig * (LOG2_E / GATE_LOGIT_NORMALIZER)

    u = _dot(h, w_ref[0, :, 0:_OFF_Z]).astype(_bf16)
    z_ref[0] = _dot(h, w_ref[0, :, _OFF_Z:_OFF_QK]).astype(_bf16)
    for g in range(N_FOURIER_GROUPS):
        pq = _dot(u[:, g * FOURIER_GROUP:(g + 1) * FOURIER_GROUP], ab_ref[0, g])
        p_ref[0, g] = pq[:, :FOURIER_GROUP].astype(_bf16)
        q_ref[0, g] = pq[:, FOURIER_GROUP:].astype(_bf16)
    qk_ref[0] = _dot(h, w_ref[0, :, _OFF_QK:_OFF_V]).astype(_bf16)
    v_ref[0] = _dot(h, w_ref[0, :, _OFF_V:_OFF_R]).astype(_bf16)
    r_ref[0] = _dot(h, w_ref[0, :, _OFF_R:D_MAIN]).astype(_bf16)


def _inproj(x, mod, norm_g3, w_all, ab, w_gate, wc, bc, layer, tm):
    b, seq, d = x.shape
    row = lambda i, j: (i, j, 0)
    lay3 = lambda i, j: (layer, 0, 0)
    lay4 = lambda i, j: (layer, 0, 0, 0)
    ng = N_FOURIER_GROUPS

    def out(width, dtype=_bf16):
        return (pl.BlockSpec((1, tm, width), row),
                jax.ShapeDtypeStruct((b, seq, width), dtype))

    def grouped():
        return (pl.BlockSpec((1, ng, tm, FOURIER_GROUP), lambda i, j: (i, 0, j, 0)),
                jax.ShapeDtypeStruct((b, ng, seq, FOURIER_GROUP), _bf16))

    outs = [grouped(), grouped(), out(1024), out(1024), out(1024), out(1024), out(1024, _f32)]
    return pl.pallas_call(
        _inproj_kernel,
        grid=(b, seq // tm),
        in_specs=[
            pl.BlockSpec((1, tm, d), row),
            pl.BlockSpec((1, 1, 3, d), lambda i, j: (layer, i, 0, 0)),
            pl.BlockSpec((1, 1, d), lay3),
            pl.BlockSpec((1, d, D_MAIN), lay3, pipeline_mode=pl.Buffered(1)),
            pl.BlockSpec((1, ng, FOURIER_GROUP, 2 * FOURIER_GROUP), lay4),
            pl.BlockSpec((1, d, 8 * GATE_RANK), lay3),
            pl.BlockSpec((1, 2 * GATE_RANK, 2 * D_GLA_KEY), lay3),
            pl.BlockSpec((1, 1, 2 * D_GLA_KEY), lay3),
        ],
        out_specs=[o[0] for o in outs],
        out_shape=[o[1] for o in outs],
        compiler_params=_params(),
        name="inproj",
    )(x, mod, norm_g3, w_all, ab, w_gate, wc, bc)


def _fft_kernel(lm_ref, p_ref, q_ref, o_ref, b_ref):
    sub = p_ref.shape[2]
    grp = o_ref.shape[2]
    for n1 in range(FFT_RADIX):
        cols = slice(n1 * grp, (n1 + 1) * grp)
        b_ref[n1] = (_dot(lm_ref[n1, :, 0:sub], p_ref[0, 0, :, cols])
                     + _dot(lm_ref[n1, :, sub:2 * sub], q_ref[0, 0, :, cols]))

    rt = np.float32(np.sqrt(0.5))

    def combine(c, carry):
        r0 = pl.multiple_of(c * FFT_ROWS, FFT_ROWS)
        for t in range(grp // LANES):
            lanes = slice(t * LANES, (t + 1) * LANES)
            a = [b_ref[n, pl.ds(r0, FFT_ROWS), lanes] for n in range(FFT_RADIX)]
            bi = [b_ref[n, pl.ds(sub + r0, FFT_ROWS), lanes] for n in range(FFT_RADIX)]
            e0, e1 = a[0] + a[4], a[0] - a[4]
            p1, p2, p3 = a[1] + a[7], a[2] + a[6], a[3] + a[5]
            s13, rd, t0 = p1 + p3, rt * (p1 - p3), e0 + p2
            c0, c4, c2, c1, c3 = t0 + s13, t0 - s13, e0 - p2, e1 + rd, e1 - rd
            m1, m2, m3 = bi[1] - bi[7], bi[2] - bi[6], bi[3] - bi[5]
            rs = rt * (m1 + m3)
            s1, s3, s2 = m2 + rs, rs - m2, m1 - m3
            ys = (c0, c1 - s1, c2 - s2, c3 - s3, c4, c3 + s3, c2 + s2, c1 + s1)
            for k1, yk in enumerate(ys):
                o_ref[0, pl.ds(k1 * sub + r0, FFT_ROWS), lanes] = yk.astype(_bf16)
        return carry

    lax.fori_loop(0, sub // FFT_ROWS, combine, 0, unroll=2)


def _fft_matrices(seq):
    sub = seq // FFT_RADIX
    k2 = np.arange(sub)[:, None]
    n2 = np.arange(sub)[None, :]
    mats = []
    for n1 in range(FFT_RADIX):
        ang = 2.0 * np.pi * ((k2 * (n1 + FFT_RADIX * n2)) % seq) / seq
        c, s = np.cos(ang), np.sin(ang)
        mats.append(np.block([[c, -s], [s, c]]))
    return jnp.asarray(np.stack(mats), _f32).astype(_bf16)


def _seqfft(lm, p, q):
    b, ng, seq, grp = p.shape
    sub = seq // FFT_RADIX
    p = p.reshape(b, ng, sub, FFT_RADIX * grp)
    q = q.reshape(b, ng, sub, FFT_RADIX * grp)
    dec = pl.BlockSpec((1, 1, sub, FFT_RADIX * grp), lambda i, g: (i, g, 0, 0))
    return pl.pallas_call(
        _fft_kernel,
        grid=(b, ng),
        in_specs=[
            pl.BlockSpec((FFT_RADIX, 2 * sub, 2 * sub), lambda i, g: (0, 0, 0)),
            dec, dec,
        ],
        out_specs=pl.BlockSpec((1, seq, grp), lambda i, g: (i, 0, g)),
        out_shape=jax.ShapeDtypeStruct((b, seq, ng * grp), _bf16),
        scratch_shapes=[pltpu.VMEM((FFT_RADIX, 2 * sub, grp), _f32)],
        compiler_params=_params(),
        name="seqfft",
    )(lm, p, q)


def _gla_kernel(tri_ref, q_ref, k_ref, v_ref, la_ref, g_ref, o_ref,
                oacc_ref, u_ref, qd_ref, dec_ref, s_ref, *, seq):
    n_blocks = seq // GLA_BLOCK
    nb = CHUNKS_PER_BLOCK
    dk = GLA_HEAD_K
    scale = dk ** -0.5

    def chunked(a):
        return a.reshape(nb, CHUNK, a.shape[-1])

    def flat(a):
        return a.reshape(a.shape[0] * a.shape[1], a.shape[2])

    lane = lax.broadcasted_iota(jnp.int32, (CHUNK, 2 * CHUNK), 1)
    row_l = lax.broadcasted_iota(jnp.int32, (CHUNK, 2 * CHUNK), 0)
    keep = (((lane >= CHUNK) | (lane <= row_l), (lane < CHUNK) | (lane - CHUNK <= row_l)),
            ((lane >= CHUNK) | (lane > row_l), (lane < CHUNK) | (lane - CHUNK > row_l)))

    def operands(blk):
        r0 = pl.multiple_of(blk * GLA_BLOCK, GLA_BLOCK)
        rows = pl.ds(r0, GLA_BLOCK)
        la = la_ref[0, rows, :]
        la_hi, la_lo = _split_bf16(la)
        tri = tri_ref[...]
        pre = _dot(tri, la_hi) + _dot(tri, la_lo)
        pre_b = pre[:, dk:]
        g_f = chunked(pre[:, :dk])
        g_b = chunked(pre_b[GLA_BLOCK - 1:GLA_BLOCK, :] - pre_b + la[:, dk:])
        zero = jnp.zeros((1, 1, dk), _f32)
        edge_f = g_f[:, CHUNK - 1:CHUNK, :]
        edge_b = g_b[:, 0:1, :]
        dirs = ((g_f, edge_f, jnp.concatenate([zero, edge_f[:nb - 1]], axis=0),
                 CHUNK // 2, list(range(nb))),
                (g_b, edge_b, jnp.concatenate([edge_b[1:], zero], axis=0),
                 CHUNK // 2 - 1, list(range(nb - 1, -1, -1))))

        qs = chunked(q_ref[0, rows, :].astype(_f32) * scale)
        kk = chunked(k_ref[0, rows, :].astype(_f32))
        zeros_chunk = jnp.zeros((CHUNK, dk), _f32)
        per_dir = []
        for d, (g, edge, base, ref_i, order) in enumerate(dirs):
            tot = edge[order[-1]:order[-1] + 1]
            cum = g - base
            ref = cum[:, ref_i:ref_i + 1, :]
            a = cum - ref
            qi = qs * jnp.exp2(a)
            ki = kk * jnp.exp2(-a)
            last_ref = (edge - base) - ref
            qd_ref[d, rows, :] = flat(qi * jnp.exp2(ref + base)).astype(_bf16)
            kd_t = flat(ki * jnp.exp2(last_ref + (tot - edge))).T.astype(_bf16)
            dec_ref[d, blk] = jnp.broadcast_to(jnp.exp2(tot).reshape(1, dk), (dk, dk)).T
            rhs_t = []
            for i in range(nb):
                pieces = []
                for c in range(nb):
                    if c == i:
                        pieces.append(ki[c])
                    elif order.index(c) < order.index(i):
                        pieces.append(ki[c] * jnp.exp2(last_ref[c] + ref[i] + base[i] - edge[c]))
                    else:
                        pieces.append(zeros_chunk)
                rhs_t.append(jnp.concatenate(pieces, axis=0).T.astype(_bf16))
            per_dir.append((flat(qi).astype(_bf16), rhs_t, kd_t))
        return blk, rows, per_dir

    def scores(ops):
        _, _, per_dir = ops
        return [[_dot(qi[i * CHUNK:(i + 1) * CHUNK], rhs_t[i]) for i in range(nb)]
                for qi, rhs_t, _ in per_dir]

    def outputs(ops, mats):
        blk, rows, per_dir = ops
        score_parts = []
        for d, s_rows in enumerate(mats):
            for i, s_i in enumerate(s_rows):
                tile, half = divmod(i, 2)
                lo, hi = tile * 2 * CHUNK, (tile + 1) * 2 * CHUNK
                masked = jnp.where(keep[d][half], s_i[:, lo:hi], 0.0)
                s_i = jnp.concatenate([masked, s_i[:, hi:]] if tile == 0 else [s_i[:, :lo], masked], axis=1)
                score_parts.append(s_i.astype(_bf16))
        lhs = jnp.concatenate(score_parts + [p[2] for p in per_dir], axis=0)
        res = _dot(lhs, v_ref[0, rows, :])
        oacc_ref[rows, :] = res[0:GLA_BLOCK] + res[GLA_BLOCK:2 * GLA_BLOCK]
        u_ref[0, blk] = res[2 * GLA_BLOCK:2 * GLA_BLOCK + dk]
        u_ref[1, blk] = res[2 * GLA_BLOCK + dk:2 * GLA_BLOCK + 2 * dk]

    def phase_a(i, carry):
        ops0 = operands(2 * i)
        ops1 = operands(2 * i + 1)
        mats0 = scores(ops0)
        mats1 = scores(ops1)
        outputs(ops0, mats0)
        outputs(ops1, mats1)
        return carry

    lax.fori_loop(0, n_blocks // 2, phase_a, 0)

    s_ref[...] = jnp.zeros_like(s_ref)

    def phase_b(i, carry):
        for d in range(2):
            blk = i if d == 0 else n_blocks - 1 - i
            r0 = pl.multiple_of(blk * GLA_BLOCK, GLA_BLOCK)
            rows = pl.ds(r0, GLA_BLOCK)
            s = s_ref[d]
            oacc_ref[rows, :] += _dot(qd_ref[d, rows, :], s.astype(_bf16))
            dec = dec_ref[d, blk]
            s_ref[d] = jnp.concatenate([dec, dec], axis=1) * s + u_ref[d, blk]
        return carry

    lax.fori_loop(0, n_blocks, phase_b, 0, unroll=4)

    def finish(blk, carry):
        r0 = pl.multiple_of(blk * GLA_BLOCK, GLA_BLOCK)
        rows = pl.ds(r0, GLA_BLOCK)
        o = oacc_ref[rows, :]
        o = o * lax.rsqrt(jnp.mean(o * o, axis=-1, keepdims=True) + EPS) * g_ref[0]
        o_ref[0, rows, :] = o.astype(_bf16)
        return carry

    lax.fori_loop(0, n_blocks, finish, 0)


def _gla(tri, qk, v, la, gla_norm_g3, layer):
    b, seq, _ = v.shape
    h = N_GLA_HEADS
    n_blocks = seq // GLA_BLOCK
    return pl.pallas_call(
        functools.partial(_gla_kernel, seq=seq),
        grid=(b, h),
        in_specs=[
            pl.BlockSpec((GLA_BLOCK, GLA_BLOCK), lambda i, j: (0, 0)),
            pl.BlockSpec((1, seq, GLA_HEAD_K), lambda i, j: (i, 0, j)),
            pl.BlockSpec((1, seq, GLA_HEAD_K), lambda i, j: (i, 0, h + j)),
            pl.BlockSpec((1, seq, GLA_HEAD_V), lambda i, j: (i, 0, j)),
            pl.BlockSpec((1, seq, 2 * GLA_HEAD_K), lambda i, j: (i, 0, j)),
            pl.BlockSpec((1, 1, GLA_HEAD_V), lambda i, j: (layer, 0, j)),
        ],
        out_specs=pl.BlockSpec((1, seq, GLA_HEAD_V), lambda i, j: (i, 0, j)),
        out_shape=jax.ShapeDtypeStruct((b, seq, D_GLA), _bf16),
        scratch_shapes=[
            pltpu.VMEM((seq, GLA_HEAD_V), _f32),
            pltpu.VMEM((2, n_blocks, GLA_HEAD_K, GLA_HEAD_V), _f32),
            pltpu.VMEM((2, seq, GLA_HEAD_K), _bf16),
            pltpu.VMEM((2, n_blocks, GLA_HEAD_K, GLA_HEAD_K), _f32),
            pltpu.VMEM((2, GLA_HEAD_K, GLA_HEAD_V), _f32),
        ],
        compiler_params=_params(),
        name="gla",
    )(tri, qk, qk, v, la, gla_norm_g3)


def _outproj_kernel(x_ref, yf_ref, z_ref, yg_ref, r_ref, w_ref, mod_ref, fg_ref, o_ref, *, final):
    def gated(y_ref, gate_ref):
        return (y_ref[0].astype(_f32) * _silu(gate_ref[0].astype(_f32))).astype(_bf16)

    y = (_dot(gated(yf_ref, z_ref), w_ref[0, 0:D_FOURIER, :])
         + _dot(gated(yg_ref, r_ref), w_ref[0, D_FOURIER:, :]))
    gate = mod_ref[0, 0][2:3, :]
    xn = x_ref[0] + gate * y
    if final:
        ms = jnp.mean(xn * xn, axis=-1, keepdims=True)
        xn = xn * lax.rsqrt(ms + EPS) * fg_ref[...]
    o_ref[0] = xn


def _outproj(x, yf, z, yg, r, w_out_b, mod, final_g, layer, tm, final):
    b, seq, d = x.shape
    row = lambda i, j: (i, j, 0)
    return pl.pallas_call(
        functools.partial(_outproj_kernel, final=final),
        grid=(b, seq // tm),
        in_specs=[
            pl.BlockSpec((1, tm, d), row),
            pl.BlockSpec((1, tm, D_FOURIER), row),
            pl.BlockSpec((1, tm, D_FOURIER), row),
            pl.BlockSpec((1, tm, D_GLA), row),
            pl.BlockSpec((1, tm, D_GLA), row),
            pl.BlockSpec((1, D_FOURIER + D_GLA, d), lambda i, j: (layer, 0, 0)),
            pl.BlockSpec((1, 1, 3, d), lambda i, j: (layer, i, 0, 0)),
            pl.BlockSpec((1, d), lambda i, j: (0, 0)),
        ],
        out_specs=pl.BlockSpec((1, tm, d), row),
        out_shape=jax.ShapeDtypeStruct((b, seq, d), _f32),
        compiler_params=_params(),
        name="outproj",
    )(x, yf, z, yg, r, w_out_b, mod, final_g)


def _block_tri():
    idx = np.arange(GLA_BLOCK)
    return jnp.asarray((idx[None, :] <= idx[:, None]).astype(np.float32), _bf16)


def kernel(x, c, norm_g, w_ada, b_ada, w_in, w_fmap, w_af, b_af, w_ab, b_ab, gla_norm_g, w_out, final_g):
    depth = w_in.shape[0]
    b, seq, d = x.shape
    assert d == D_MODEL and seq % (2 * GLA_BLOCK) == 0 and seq % (FFT_RADIX * FFT_ROWS) == 0

    mod = _ada_mod(c, w_ada, b_ada).reshape(depth, b, 3, d)
    ab = _fold_fourier_weights(w_fmap, seq)
    w_in_t = jnp.swapaxes(w_in, 1, 2)
    w_all = jnp.swapaxes(w_in_t[:, :D_MAIN, :].astype(_bf16), 1, 2)
    w_low = jnp.swapaxes(w_in_t[:, D_MAIN:, :], 1, 2)
    w_gate = jnp.concatenate([w_low, w_low, w_low, jnp.zeros_like(w_low)], axis=-1)
    w_out_b = w_out.astype(_bf16)

    zeros = jnp.zeros_like(w_af)
    wc = jnp.concatenate([jnp.concatenate([w_af, zeros], axis=1),
                          jnp.concatenate([zeros, w_ab], axis=1)], axis=-1)
    bc = jnp.concatenate([b_af, b_ab], axis=-1)

    def per_head(a):
        lead = a.shape[:-1]
        a = a.reshape(lead + (2, N_GLA_HEADS, GLA_HEAD_K))
        return jnp.swapaxes(a, -3, -2).reshape(lead + (2 * D_GLA_KEY,))

    wc = per_head(wc)
    bc = per_head(bc).reshape(depth, 1, 2 * D_GLA_KEY)

    lm = _fft_matrices(seq)
    tri = _block_tri()
    norm_g3 = norm_g.reshape(depth, 1, d)
    gla_norm_g3 = gla_norm_g.reshape(depth, 1, D_GLA)
    final_g2 = final_g.reshape(1, d)

    for l in range(depth):
        p, q, z, qk, v, r, la = _inproj(x, mod, norm_g3, w_all, ab, w_gate, wc, bc, layer=l, tm=512)
        y_f = _seqfft(lm, p, q)
        y_g = _gla(tri, qk, v, la, gla_norm_g3, layer=l)
        x = _outproj(x, y_f, z, y_g, r, w_out_b, mod, final_g2, layer=l, tm=512,
                     final=(l == depth - 1))
    return x
```

```python
import functools

import numpy as np
import jax
import jax.numpy as jnp
from jax import lax
from jax.experimental import pallas as pl
from jax.experimental.pallas import tpu as pltpu

D_MODEL = 1024
D_FOURIER = 1024
N_FOURIER_GROUPS = 4
FOURIER_GROUP = 256
N_GLA_HEADS = 4
D_GLA_KEY = 512
D_GLA = 1024
GLA_HEAD_K = 128
GLA_HEAD_V = 256
GATE_RANK = 16
GATE_LOGIT_NORMALIZER = 16.0
CHUNK = 64
EPS = 1e-6
LOG2_E = 1.4426950408889634

_OFF_Z = D_FOURIER
_OFF_QK = 2 * D_FOURIER
_OFF_V = _OFF_QK + 2 * D_GLA_KEY
_OFF_R = _OFF_V + D_GLA
D_MAIN = _OFF_R + D_GLA

GLA_BLOCK = 256
CHUNKS_PER_BLOCK = GLA_BLOCK // CHUNK

FFT_RADIX = 8
FFT_ROWS = 16
LANES = 128

VMEM_LIMIT_BYTES = 56 * 1024 * 1024

_f32 = jnp.float32
_bf16 = jnp.bfloat16


def _dot(a, b):
    return jnp.dot(a, b, preferred_element_type=_f32)


def _split_bf16(a):
    hi = a.astype(_bf16)
    lo = (a - hi.astype(_f32)).astype(_bf16)
    return hi, lo


def _silu(v):
    return v * (1.0 / (1.0 + jnp.exp(-v)))


def _params():
    return pltpu.CompilerParams(dimension_semantics=("arbitrary", "arbitrary"),
                                vmem_limit_bytes=VMEM_LIMIT_BYTES)


def _ada_kernel(c_ref, w_ref, b_ref, o_ref):
    c_act = _silu(c_ref[...]).astype(_bf16)
    o_ref[0] = _dot(c_act, w_ref[0].astype(_bf16)) + b_ref[0]


def _ada_mod(c, w_ada, b_ada):
    depth = w_ada.shape[0]
    b = c.shape[0]
    n_tiles = 3
    return pl.pallas_call(
        _ada_kernel,
        grid=(depth, n_tiles),
        in_specs=[
            pl.BlockSpec((b, D_MODEL), lambda l, j: (0, 0)),
            pl.BlockSpec((1, D_MODEL, D_MODEL), lambda l, j: (l, 0, j)),
            pl.BlockSpec((1, 1, D_MODEL), lambda l, j: (l, 0, j)),
        ],
        out_specs=pl.BlockSpec((1, b, D_MODEL), lambda l, j: (l, 0, j)),
        out_shape=jax.ShapeDtypeStruct((depth, b, 3 * D_MODEL), _f32),
        compiler_params=_params(),
        name="ada_mod",
    )(c, w_ada, b_ada.reshape(depth, 1, 3 * D_MODEL))


def _fold_kernel(wf_ref, cc_ref, sc_ref, ab_ref, *, norm):
    hp = lax.Precision.HIGHEST
    wf = wf_ref[0, 0]
    a = jnp.dot(cc_ref[...], wf, precision=hp, preferred_element_type=_f32) * norm
    bm = jnp.dot(sc_ref[...], wf, precision=hp, preferred_element_type=_f32) * norm
    ab_ref[0, 0] = jnp.concatenate([a, bm], axis=1).astype(_bf16)


def _fold_fourier_weights(w_fmap, seq_len):
    depth = w_fmap.shape[0]
    idx = np.arange(FOURIER_GROUP)
    ang = 2.0 * np.pi * ((idx[:, None] * idx[None, :]) % FOURIER_GROUP) / FOURIER_GROUP
    cc = jnp.asarray(np.cos(ang), _f32)
    sc = jnp.asarray(np.sin(ang), _f32)
    norm = float(1.0 / np.sqrt(seq_len * FOURIER_GROUP))
    gspec = pl.BlockSpec((FOURIER_GROUP, FOURIER_GROUP), lambda l, g: (0, 0))
    return pl.pallas_call(
        functools.partial(_fold_kernel, norm=norm),
        grid=(depth, N_FOURIER_GROUPS),
        in_specs=[
            pl.BlockSpec((1, 1, FOURIER_GROUP, FOURIER_GROUP), lambda l, g: (l, g, 0, 0)),
            gspec, gspec,
        ],
        out_specs=pl.BlockSpec((1, 1, FOURIER_GROUP, 2 * FOURIER_GROUP), lambda l, g: (l, g, 0, 0)),
        out_shape=jax.ShapeDtypeStruct((depth, N_FOURIER_GROUPS, FOURIER_GROUP, 2 * FOURIER_GROUP), _bf16),
        compiler_params=_params(),
        name="fold_fourier",
    )(w_fmap, cc, sc)


PREP_ROWS = 512


def _prep_main_kernel(wt_ref, o_ref):
    o_ref[0] = wt_ref[0].T.astype(_bf16)


def _prep_gate_kernel(wt_ref, o_ref):
    w = wt_ref[0]
    o_ref[0] = jnp.concatenate([w, w, w, jnp.zeros_like(w)], axis=0).T


def _prep_weights(w_in_t):
    depth, _, d = w_in_t.shape
    rank2 = 2 * GATE_RANK
    w_all = pl.pallas_call(
        _prep_main_kernel,
        grid=(depth, D_MAIN // PREP_ROWS),
        in_specs=[pl.BlockSpec((1, PREP_ROWS, d), lambda l, j: (l, j, 0))],
        out_specs=pl.BlockSpec((1, d, PREP_ROWS), lambda l, j: (l, 0, j)),
        out_shape=jax.ShapeDtypeStruct((depth, d, D_MAIN), _bf16),
        compiler_params=_params(),
        name="prep_main",
    )(w_in_t)
    w_gate = pl.pallas_call(
        _prep_gate_kernel,
        grid=(depth,),
        in_specs=[pl.BlockSpec((1, rank2, d), lambda l: (l, D_MAIN // rank2, 0))],
        out_specs=pl.BlockSpec((1, d, 4 * rank2), lambda l: (l, 0, 0)),
        out_shape=jax.ShapeDtypeStruct((depth, d, 4 * rank2), _f32),
        compiler_params=pltpu.CompilerParams(dimension_semantics=("arbitrary",),
                                             vmem_limit_bytes=VMEM_LIMIT_BYTES),
        name="prep_gate",
    )(w_in_t)
    return w_all, w_gate


def _inproj_kernel(x_ref, mod_ref, g_ref, w_ref, ab_ref, wg_ref, wc_ref, bc_ref,
                   p_ref, q_ref, z_ref, qk_ref, v_ref, r_ref, la_ref, pq_sc):
    x = x_ref[0]
    ms = jnp.mean(x * x, axis=-1, keepdims=True)
    y = x * lax.rsqrt(ms + EPS) * g_ref[0]
    mod = mod_ref[0, 0]
    shift = mod[0:1, :]
    scale = mod[1:2, :]
    h = (y * (1.0 + scale) + shift).astype(_bf16)

    rank2 = 2 * GATE_RANK
    g3 = _dot(h, wg_ref[0].astype(_bf16))
    g_hi, g_lo = _split_bf16(g3)
    lane = lax.broadcasted_iota(jnp.int32, g3.shape, 1)
    lhs = jnp.where((lane >= rank2) & (lane < 2 * rank2), g_lo, g_hi)
    w_hi, w_lo = _split_bf16(wc_ref[0])
    rhs = jnp.concatenate([w_hi, w_hi, w_lo, jnp.zeros_like(w_hi)], axis=0)
    logits = _dot(lhs, rhs) + bc_ref[0]
    log_sig = jnp.minimum(logits, 0.0) - jnp.log1p(jnp.exp(-jnp.abs(logits)))
    la_ref[0] = log_sig * (LOG2_E / GATE_LOGIT_NORMALIZER)

    u = _dot(h, w_ref[0, :, 0:_OFF_Z]).astype(_bf16)
    z_ref[0] = _dot(h, w_ref[0, :, _OFF_Z:_OFF_QK]).astype(_bf16)
    n_slab = 2 * FOURIER_GROUP // LANES
    for g in range(N_FOURIER_GROUPS):
        pq = _dot(u[:, g * FOURIER_GROUP:(g + 1) * FOURIER_GROUP], ab_ref[0, g])
        for s in range(n_slab):
            pq_sc[g * n_slab + s] = pq[:, s * LANES:(s + 1) * LANES]
    qk_ref[0] = _dot(h, w_ref[0, :, _OFF_QK:_OFF_V]).astype(_bf16)
    v_ref[0] = _dot(h, w_ref[0, :, _OFF_V:_OFF_R]).astype(_bf16)
    r_ref[0] = _dot(h, w_ref[0, :, _OFF_R:D_MAIN]).astype(_bf16)

    sub_rows = pq_sc.shape[1] // FFT_RADIX
    for g in range(N_FOURIER_GROUPS):
        for s in range(n_slab):
            dst = p_ref if s < n_slab // 2 else q_ref
            c0 = (s % (n_slab // 2)) * LANES
            for n1 in range(FFT_RADIX):
                rows = pq_sc[g * n_slab + s, pl.ds(n1, sub_rows, stride=FFT_RADIX), :]
                lo = n1 * FOURIER_GROUP + c0
                dst[0, g, :, lo:lo + LANES] = rows.astype(_bf16)


def _inproj(x, mod, norm_g3, w_all, ab, w_gate, wc, bc, layer, tm):
    b, seq, d = x.shape
    row = lambda i, j: (i, j, 0)
    lay3 = lambda i, j: (layer, 0, 0)
    lay4 = lambda i, j: (layer, 0, 0, 0)
    ng = N_FOURIER_GROUPS

    def out(width, dtype=_bf16):
        return (pl.BlockSpec((1, tm, width), row),
                jax.ShapeDtypeStruct((b, seq, width), dtype))

    def grouped():
        width = FFT_RADIX * FOURIER_GROUP
        return (pl.BlockSpec((1, ng, tm // FFT_RADIX, width), lambda i, j: (i, 0, j, 0)),
                jax.ShapeDtypeStruct((b, ng, seq // FFT_RADIX, width), _bf16))

    outs = [grouped(), grouped(), out(1024), out(1024), out(1024), out(1024), out(1024, _f32)]
    return pl.pallas_call(
        _inproj_kernel,
        grid=(b, seq // tm),
        in_specs=[
            pl.BlockSpec((1, tm, d), row),
            pl.BlockSpec((1, 1, 3, d), lambda i, j: (layer, i, 0, 0)),
            pl.BlockSpec((1, 1, d), lay3),
            pl.BlockSpec((1, d, D_MAIN), lay3, pipeline_mode=pl.Buffered(1)),
            pl.BlockSpec((1, ng, FOURIER_GROUP, 2 * FOURIER_GROUP), lay4),
            pl.BlockSpec((1, d, 8 * GATE_RANK), lay3),
            pl.BlockSpec((1, 2 * GATE_RANK, 2 * D_GLA_KEY), lay3),
            pl.BlockSpec((1, 1, 2 * D_GLA_KEY), lay3),
        ],
        out_specs=[o[0] for o in outs],
        out_shape=[o[1] for o in outs],
        scratch_shapes=[pltpu.VMEM((ng * 2 * FOURIER_GROUP // LANES, tm, LANES), _f32)],
        compiler_params=_params(),
        name="inproj",
    )(x, mod, norm_g3, w_all, ab, w_gate, wc, bc)


def _fft_kernel(lm_ref, p_ref, q_ref, o_ref, b_ref):
    sub = p_ref.shape[2]
    grp = o_ref.shape[2]
    for n1 in range(FFT_RADIX):
        cols = slice(n1 * grp, (n1 + 1) * grp)
        b_ref[n1] = (_dot(lm_ref[n1, :, 0:sub], p_ref[0, 0, :, cols])
                     + _dot(lm_ref[n1, :, sub:2 * sub], q_ref[0, 0, :, cols]))

    rt = np.float32(np.sqrt(0.5))

    def combine(c, carry):
        r0 = pl.multiple_of(c * FFT_ROWS, FFT_ROWS)
        for t in range(grp // LANES):
            lanes = slice(t * LANES, (t + 1) * LANES)
            a = [b_ref[n, pl.ds(r0, FFT_ROWS), lanes] for n in range(FFT_RADIX)]
            bi = [b_ref[n, pl.ds(sub + r0, FFT_ROWS), lanes] for n in range(FFT_RADIX)]
            e0, e1 = a[0] + a[4], a[0] - a[4]
            p1, p2, p3 = a[1] + a[7], a[2] + a[6], a[3] + a[5]
            s13, rd, t0 = p1 + p3, rt * (p1 - p3), e0 + p2
            c0, c4, c2, c1, c3 = t0 + s13, t0 - s13, e0 - p2, e1 + rd, e1 - rd
            m1, m2, m3 = bi[1] - bi[7], bi[2] - bi[6], bi[3] - bi[5]
            rs = rt * (m1 + m3)
            s1, s3, s2 = m2 + rs, rs - m2, m1 - m3
            ys = (c0, c1 - s1, c2 - s2, c3 - s3, c4, c3 + s3, c2 + s2, c1 + s1)
            for k1, yk in enumerate(ys):
                o_ref[0, pl.ds(k1 * sub + r0, FFT_ROWS), lanes] = yk.astype(_bf16)
        return carry

    lax.fori_loop(0, sub // FFT_ROWS, combine, 0, unroll=2)


def _fft_matrices(seq):
    sub = seq // FFT_RADIX
    k2 = np.arange(sub)[:, None]
    n2 = np.arange(sub)[None, :]
    mats = []
    for n1 in range(FFT_RADIX):
        ang = 2.0 * np.pi * ((k2 * (n1 + FFT_RADIX * n2)) % seq) / seq
        c, s = np.cos(ang), np.sin(ang)
        mats.append(np.block([[c, -s], [s, c]]))
    return jnp.asarray(np.stack(mats), _f32).astype(_bf16)


def _seqfft(lm, p, q):
    b, ng, sub, width = p.shape
    grp = width // FFT_RADIX
    seq = sub * FFT_RADIX
    dec = pl.BlockSpec((1, 1, sub, FFT_RADIX * grp), lambda i, g: (i, g, 0, 0))
    return pl.pallas_call(
        _fft_kernel,
        grid=(b, ng),
        in_specs=[
            pl.BlockSpec((FFT_RADIX, 2 * sub, 2 * sub), lambda i, g: (0, 0, 0)),
            dec, dec,
        ],
        out_specs=pl.BlockSpec((1, seq, grp), lambda i, g: (i, 0, g)),
        out_shape=jax.ShapeDtypeStruct((b, seq, ng * grp), _bf16),
        scratch_shapes=[pltpu.VMEM((FFT_RADIX, 2 * sub, grp), _f32)],
        compiler_params=_params(),
        name="seqfft",
    )(lm, p, q)


def _gla_kernel(tri_ref, q_ref, k_ref, v_ref, la_ref, g_ref, o_ref,
                oacc_ref, u_ref, qd_ref, dec_ref, s_ref, *, seq):
    n_blocks = seq // GLA_BLOCK
    nb = CHUNKS_PER_BLOCK
    dk = GLA_HEAD_K
    scale = dk ** -0.5

    def chunked(a):
        return a.reshape(nb, CHUNK, a.shape[-1])

    def flat(a):
        return a.reshape(a.shape[0] * a.shape[1], a.shape[2])

    lane = lax.broadcasted_iota(jnp.int32, (CHUNK, 2 * CHUNK), 1)
    row_l = lax.broadcasted_iota(jnp.int32, (CHUNK, 2 * CHUNK), 0)
    keep = (((lane >= CHUNK) | (lane <= row_l), (lane < CHUNK) | (lane - CHUNK <= row_l)),
            ((lane >= CHUNK) | (lane > row_l), (lane < CHUNK) | (lane - CHUNK > row_l)))

    def operands(blk):
        r0 = pl.multiple_of(blk * GLA_BLOCK, GLA_BLOCK)
        rows = pl.ds(r0, GLA_BLOCK)
        la = la_ref[0, rows, :]
        la_hi, la_lo = _split_bf16(la)
        tri = tri_ref[...]
        pre = _dot(tri, la_hi) + _dot(tri, la_lo)
        pre_b = pre[:, dk:]
        g_f = chunked(pre[:, :dk])
        g_b = chunked(pre_b[GLA_BLOCK - 1:GLA_BLOCK, :] - pre_b + la[:, dk:])
        zero = jnp.zeros((1, 1, dk), _f32)
        edge_f = g_f[:, CHUNK - 1:CHUNK, :]
        edge_b = g_b[:, 0:1, :]
        dirs = ((g_f, edge_f, jnp.concatenate([zero, edge_f[:nb - 1]], axis=0),
                 CHUNK // 2, list(range(nb))),
                (g_b, edge_b, jnp.concatenate([edge_b[1:], zero], axis=0),
                 CHUNK // 2 - 1, list(range(nb - 1, -1, -1))))

        qs = chunked(q_ref[0, rows, :].astype(_f32) * scale)
        kk = chunked(k_ref[0, rows, :].astype(_f32))
        zeros_chunk = jnp.zeros((CHUNK, dk), _f32)
        per_dir = []
        for d, (g, edge, base, ref_i, order) in enumerate(dirs):
            tot = edge[order[-1]:order[-1] + 1]
            cum = g - base
            ref = cum[:, ref_i:ref_i + 1, :]
            a = cum - ref
            qi = qs * jnp.exp2(a)
            ki = kk * jnp.exp2(-a)
            last_ref = (edge - base) - ref
            qd_ref[d, rows, :] = flat(qi * jnp.exp2(ref + base)).astype(_bf16)
            kd_t = flat(ki * jnp.exp2(last_ref + (tot - edge))).T.astype(_bf16)
            dec_ref[d, blk] = jnp.broadcast_to(jnp.exp2(tot).reshape(1, dk), (dk, dk)).T
            rhs_t = []
            for i in range(nb):
                pieces = []
                for c in range(nb):
                    if c == i:
                        pieces.append(ki[c])
                    elif order.index(c) < order.index(i):
                        pieces.append(ki[c] * jnp.exp2(last_ref[c] + ref[i] + base[i] - edge[c]))
                    else:
                        pieces.append(zeros_chunk)
                rhs_t.append(jnp.concatenate(pieces, axis=0).T.astype(_bf16))
            per_dir.append((flat(qi).astype(_bf16), rhs_t, kd_t))
        return blk, rows, per_dir

    def scores(ops):
        _, _, per_dir = ops
        return [[_dot(qi[i * CHUNK:(i + 1) * CHUNK], rhs_t[i]) for i in range(nb)]
                for qi, rhs_t, _ in per_dir]

    def outputs(ops, mats):
        blk, rows, per_dir = ops
        score_parts = []
        for d, s_rows in enumerate(mats):
            for i, s_i in enumerate(s_rows):
                tile, half = divmod(i, 2)
                lo, hi = tile * 2 * CHUNK, (tile + 1) * 2 * CHUNK
                masked = jnp.where(keep[d][half], s_i[:, lo:hi], 0.0)
                s_i = jnp.concatenate([masked, s_i[:, hi:]] if tile == 0 else [s_i[:, :lo], masked], axis=1)
                score_parts.append(s_i.astype(_bf16))
        lhs = jnp.concatenate(score_parts + [p[2] for p in per_dir], axis=0)
        res = _dot(lhs, v_ref[0, rows, :])
        oacc_ref[rows, :] = res[0:GLA_BLOCK] + res[GLA_BLOCK:2 * GLA_BLOCK]
        u_ref[0, blk] = res[2 * GLA_BLOCK:2 * GLA_BLOCK + dk]
        u_ref[1, blk] = res[2 * GLA_BLOCK + dk:2 * GLA_BLOCK + 2 * dk]

    def phase_a(i, carry):
        ops0 = operands(2 * i)
        ops1 = operands(2 * i + 1)
        mats0 = scores(ops0)
        mats1 = scores(ops1)
        outputs(ops0, mats0)
        outputs(ops1, mats1)
        return carry

    lax.fori_loop(0, n_blocks // 2, phase_a, 0)

    s_ref[...] = jnp.zeros_like(s_ref)

    def phase_b(i, carry):
        for d in range(2):
            blk = i if d == 0 else n_blocks - 1 - i
            r0 = pl.multiple_of(blk * GLA_BLOCK, GLA_BLOCK)
            rows = pl.ds(r0, GLA_BLOCK)
            s = s_ref[d]
            oacc_ref[rows, :] += _dot(qd_ref[d, rows, :], s.astype(_bf16))
            dec = dec_ref[d, blk]
            s_ref[d] = jnp.concatenate([dec, dec], axis=1) * s + u_ref[d, blk]
        return carry

    lax.fori_loop(0, n_blocks, phase_b, 0, unroll=4)

    def finish(blk, carry):
        r0 = pl.multiple_of(blk * GLA_BLOCK, GLA_BLOCK)
        rows = pl.ds(r0, GLA_BLOCK)
        o = oacc_ref[rows, :]
        o = o * lax.rsqrt(jnp.mean(o * o, axis=-1, keepdims=True) + EPS) * g_ref[0]
        o_ref[0, rows, :] = o.astype(_bf16)
        return carry

    lax.fori_loop(0, n_blocks, finish, 0)


def _gla(tri, qk, v, la, gla_norm_g3, layer):
    b, seq, _ = v.shape
    h = N_GLA_HEADS
    n_blocks = seq // GLA_BLOCK
    return pl.pallas_call(
        functools.partial(_gla_kernel, seq=seq),
        grid=(b, h),
        in_specs=[
            pl.BlockSpec((GLA_BLOCK, GLA_BLOCK), lambda i, j: (0, 0)),
            pl.BlockSpec((1, seq, GLA_HEAD_K), lambda i, j: (i, 0, j)),
            pl.BlockSpec((1, seq, GLA_HEAD_K), lambda i, j: (i, 0, h + j)),
            pl.BlockSpec((1, seq, GLA_HEAD_V), lambda i, j: (i, 0, j)),
            pl.BlockSpec((1, seq, 2 * GLA_HEAD_K), lambda i, j: (i, 0, j)),
            pl.BlockSpec((1, 1, GLA_HEAD_V), lambda i, j: (layer, 0, j)),
        ],
        out_specs=pl.BlockSpec((1, seq, GLA_HEAD_V), lambda i, j: (i, 0, j)),
        out_shape=jax.ShapeDtypeStruct((b, seq, D_GLA), _bf16),
        scratch_shapes=[
            pltpu.VMEM((seq, GLA_HEAD_V), _f32),
            pltpu.VMEM((2, n_blocks, GLA_HEAD_K, GLA_HEAD_V), _f32),
            pltpu.VMEM((2, seq, GLA_HEAD_K), _bf16),
            pltpu.VMEM((2, n_blocks, GLA_HEAD_K, GLA_HEAD_K), _f32),
            pltpu.VMEM((2, GLA_HEAD_K, GLA_HEAD_V), _f32),
        ],
        compiler_params=_params(),
        name="gla",
    )(tri, qk, qk, v, la, gla_norm_g3)


def _outproj_kernel(x_ref, yf_ref, z_ref, yg_ref, r_ref, w_ref, mod_ref, fg_ref, o_ref, *, final):
    def gated(y_ref, gate_ref):
        return (y_ref[0].astype(_f32) * _silu(gate_ref[0].astype(_f32))).astype(_bf16)

    y = (_dot(gated(yf_ref, z_ref), w_ref[0, 0:D_FOURIER, :])
         + _dot(gated(yg_ref, r_ref), w_ref[0, D_FOURIER:, :]))
    gate = mod_ref[0, 0][2:3, :]
    xn = x_ref[0] + gate * y
    if final:
        ms = jnp.mean(xn * xn, axis=-1, keepdims=True)
        xn = xn * lax.rsqrt(ms + EPS) * fg_ref[...]
    o_ref[0] = xn


def _outproj(x, yf, z, yg, r, w_out_b, mod, final_g, layer, tm, final):
    b, seq, d = x.shape
    row = lambda i, j: (i, j, 0)
    return pl.pallas_call(
        functools.partial(_outproj_kernel, final=final),
        grid=(b, seq // tm),
        in_specs=[
            pl.BlockSpec((1, tm, d), row),
            pl.BlockSpec((1, tm, D_FOURIER), row),
            pl.BlockSpec((1, tm, D_FOURIER), row),
            pl.BlockSpec((1, tm, D_GLA), row),
            pl.BlockSpec((1, tm, D_GLA), row),
            pl.BlockSpec((1, D_FOURIER + D_GLA, d), lambda i, j: (layer, 0, 0)),
            pl.BlockSpec((1, 1, 3, d), lambda i, j: (layer, i, 0, 0)),
            pl.BlockSpec((1, d), lambda i, j: (0, 0)),
        ],
        out_specs=pl.BlockSpec((1, tm, d), row),
        out_shape=jax.ShapeDtypeStruct((b, seq, d), _f32),
        compiler_params=_params(),
        name="outproj",
    )(x, yf, z, yg, r, w_out_b, mod, final_g)


def _block_tri():
    idx = np.arange(GLA_BLOCK)
    return jnp.asarray((idx[None, :] <= idx[:, None]).astype(np.float32), _bf16)


def kernel(x, c, norm_g, w_ada, b_ada, w_in, w_fmap, w_af, b_af, w_ab, b_ab, gla_norm_g, w_out, final_g):
    depth = w_in.shape[0]
    b, seq, d = x.shape
    assert d == D_MODEL and seq % (2 * GLA_BLOCK) == 0 and seq % (FFT_RADIX * FFT_ROWS) == 0

    mod = _ada_mod(c, w_ada, b_ada).reshape(depth, b, 3, d)
    ab = _fold_fourier_weights(w_fmap, seq)
    w_in_t = jnp.swapaxes(w_in, 1, 2)
    w_all, w_gate = _prep_weights(w_in_t)
    w_out_b = w_out.astype(_bf16)

    zeros = jnp.zeros_like(w_af)
    wc = jnp.concatenate([jnp.concatenate([w_af, zeros], axis=1),
                          jnp.concatenate([zeros, w_ab], axis=1)], axis=-1)
    bc = jnp.concatenate([b_af, b_ab], axis=-1)

    def per_head(a):
        lead = a.shape[:-1]
        a = a.reshape(lead + (2, N_GLA_HEADS, GLA_HEAD_K))
        return jnp.swapaxes(a, -3, -2).reshape(lead + (2 * D_GLA_KEY,))

    wc = per_head(wc)
    bc = per_head(bc).reshape(depth, 1, 2 * D_GLA_KEY)

    lm = _fft_matrices(seq)
    tri = _block_tri()
    norm_g3 = norm_g.reshape(depth, 1, d)
    gla_norm_g3 = gla_norm_g.reshape(depth, 1, D_GLA)
    final_g2 = final_g.reshape(1, d)

    for l in range(depth):
        p, q, z, qk, v, r, la = _inproj(x, mod, norm_g3, w_all, ab, w_gate, wc, bc, layer=l, tm=512)
        y_f = _seqfft(lm, p, q)
        y_g = _gla(tri, qk, v, la, gla_norm_g3, layer=l)
        x = _outproj(x, y_f, z, y_g, r, w_out_b, mod, final_g2, layer=l, tm=512,
                     final=(l == depth - 1))
    return x
```

```python
import functools

import numpy as np
import jax
import jax.numpy as jnp
from jax import lax
from jax.experimental import pallas as pl
from jax.experimental.pallas import tpu as pltpu

D_MODEL = 1024
D_FOURIER = 1024
N_FOURIER_GROUPS = 4
FOURIER_GROUP = 256
N_GLA_HEADS = 4
D_GLA_KEY = 512
D_GLA = 1024
GLA_HEAD_K = 128
GLA_HEAD_V = 256
GATE_RANK = 16
GATE_LOGIT_NORMALIZER = 16.0
CHUNK = 64
EPS = 1e-6
LOG2_E = 1.4426950408889634

_OFF_Z = D_FOURIER
_OFF_QK = 2 * D_FOURIER
_OFF_V = _OFF_QK + 2 * D_GLA_KEY
_OFF_R = _OFF_V + D_GLA
D_MAIN = _OFF_R + D_GLA

GLA_BLOCK = 256
CHUNKS_PER_BLOCK = GLA_BLOCK // CHUNK

FFT_RADIX = 8
FFT_ROWS = 16
LANES = 128
OUTPROJ_K_CHUNK = 256
GATE_COLS = 256

VMEM_LIMIT_BYTES = 56 * 1024 * 1024

_f32 = jnp.float32
_bf16 = jnp.bfloat16


def _dot(a, b):
    return jnp.dot(a, b, preferred_element_type=_f32)


def _split_bf16(a):
    hi = a.astype(_bf16)
    lo = (a - hi.astype(_f32)).astype(_bf16)
    return hi, lo


def _silu(v):
    return v * (1.0 / (1.0 + jnp.exp(-v)))


def _params():
    return pltpu.CompilerParams(dimension_semantics=("arbitrary", "arbitrary"),
                                vmem_limit_bytes=VMEM_LIMIT_BYTES)


def _ada_kernel(c_ref, w_ref, b_ref, o_ref):
    c_act = _silu(c_ref[...]).astype(_bf16)
    o_ref[0] = _dot(c_act, w_ref[0].astype(_bf16)) + b_ref[0]


def _ada_mod(c, w_ada, b_ada):
    depth = w_ada.shape[0]
    b = c.shape[0]
    n_tiles = 3
    return pl.pallas_call(
        _ada_kernel,
        grid=(depth, n_tiles),
        in_specs=[
            pl.BlockSpec((b, D_MODEL), lambda l, j: (0, 0)),
            pl.BlockSpec((1, D_MODEL, D_MODEL), lambda l, j: (l, 0, j)),
            pl.BlockSpec((1, 1, D_MODEL), lambda l, j: (l, 0, j)),
        ],
        out_specs=pl.BlockSpec((1, b, D_MODEL), lambda l, j: (l, 0, j)),
        out_shape=jax.ShapeDtypeStruct((depth, b, 3 * D_MODEL), _f32),
        compiler_params=_params(),
        name="ada_mod",
    )(c, w_ada, b_ada.reshape(depth, 1, 3 * D_MODEL))


def _fold_kernel(wf_ref, cc_ref, sc_ref, ab_ref, *, norm):
    hp = lax.Precision.HIGHEST
    wf = wf_ref[0, 0]
    a = jnp.dot(cc_ref[...], wf, precision=hp, preferred_element_type=_f32) * norm
    bm = jnp.dot(sc_ref[...], wf, precision=hp, preferred_element_type=_f32) * norm
    ab_ref[0, 0] = jnp.concatenate([a, bm], axis=1).astype(_bf16)


def _fold_fourier_weights(w_fmap, seq_len):
    depth = w_fmap.shape[0]
    idx = np.arange(FOURIER_GROUP)
    ang = 2.0 * np.pi * ((idx[:, None] * idx[None, :]) % FOURIER_GROUP) / FOURIER_GROUP
    cc = jnp.asarray(np.cos(ang), _f32)
    sc = jnp.asarray(np.sin(ang), _f32)
    norm = float(1.0 / np.sqrt(seq_len * FOURIER_GROUP))
    gspec = pl.BlockSpec((FOURIER_GROUP, FOURIER_GROUP), lambda l, g: (0, 0))
    return pl.pallas_call(
        functools.partial(_fold_kernel, norm=norm),
        grid=(depth, N_FOURIER_GROUPS),
        in_specs=[
            pl.BlockSpec((1, 1, FOURIER_GROUP, FOURIER_GROUP), lambda l, g: (l, g, 0, 0)),
            gspec, gspec,
        ],
        out_specs=pl.BlockSpec((1, 1, FOURIER_GROUP, 2 * FOURIER_GROUP), lambda l, g: (l, g, 0, 0)),
        out_shape=jax.ShapeDtypeStruct((depth, N_FOURIER_GROUPS, FOURIER_GROUP, 2 * FOURIER_GROUP), _bf16),
        compiler_params=_params(),
        name="fold_fourier",
    )(w_fmap, cc, sc)


PREP_ROWS = 512


def _prep_main_kernel(wt_ref, o_ref):
    o_ref[0] = wt_ref[0].T.astype(_bf16)


def _prep_gate_kernel(wt_ref, o_ref):
    w = wt_ref[0]
    o_ref[0] = jnp.concatenate([w, w, w, jnp.zeros_like(w)], axis=0).T


def _prep_weights(w_in_t):
    depth, _, d = w_in_t.shape
    rank2 = 2 * GATE_RANK
    w_all = pl.pallas_call(
        _prep_main_kernel,
        grid=(depth, D_MAIN // PREP_ROWS),
        in_specs=[pl.BlockSpec((1, PREP_ROWS, d), lambda l, j: (l, j, 0))],
        out_specs=pl.BlockSpec((1, d, PREP_ROWS), lambda l, j: (l, 0, j)),
        out_shape=jax.ShapeDtypeStruct((depth, d, D_MAIN), _bf16),
        compiler_params=_params(),
        name="prep_main",
    )(w_in_t)
    w_gate = pl.pallas_call(
        _prep_gate_kernel,
        grid=(depth,),
        in_specs=[pl.BlockSpec((1, rank2, d), lambda l: (l, D_MAIN // rank2, 0))],
        out_specs=pl.BlockSpec((1, d, 4 * rank2), lambda l: (l, 0, 0)),
        out_shape=jax.ShapeDtypeStruct((depth, d, 4 * rank2), _f32),
        compiler_params=pltpu.CompilerParams(dimension_semantics=("arbitrary",),
                                             vmem_limit_bytes=VMEM_LIMIT_BYTES),
        name="prep_gate",
    )(w_in_t)
    return w_all, w_gate


def _inproj_kernel(x_ref, mod_ref, g_ref, w_ref, ab_ref, wg_ref, wc_ref, bc_ref,
                   p_ref, q_ref, z_ref, qk_ref, v_ref, r_ref, la_ref, pq_sc):
    x = x_ref[0]
    ms = jnp.mean(x * x, axis=-1, keepdims=True)
    y = x * lax.rsqrt(ms + EPS) * g_ref[0]
    mod = mod_ref[0, 0]
    shift = mod[0:1, :]
    scale = mod[1:2, :]
    h = (y * (1.0 + scale) + shift).astype(_bf16)

    rank2 = 2 * GATE_RANK
    g3 = _dot(h, wg_ref[0].astype(_bf16))
    u = _dot(h, w_ref[0, :, 0:_OFF_Z]).astype(_bf16)
    z_ref[0] = _dot(h, w_ref[0, :, _OFF_Z:_OFF_QK]).astype(_bf16)

    g_hi, g_lo = _split_bf16(g3)
    lane = lax.broadcasted_iota(jnp.int32, g3.shape, 1)
    lhs = jnp.where((lane >= rank2) & (lane < 2 * rank2), g_lo, g_hi)
    w_hi, w_lo = _split_bf16(wc_ref[0])
    rhs = jnp.concatenate([w_hi, w_hi, w_lo, jnp.zeros_like(w_hi)], axis=0)

    def log2_decay(k):
        cols = slice(k * GATE_COLS, (k + 1) * GATE_COLS)
        x_g = _dot(lhs, rhs[:, cols]) + bc_ref[0][:, cols]
        soft = jnp.log2(1.0 + jnp.exp2(jnp.abs(x_g) * -LOG2_E))
        la_ref[0, :, cols] = (jnp.minimum(x_g, 0.0) * (LOG2_E / GATE_LOGIT_NORMALIZER)
                              - soft * (1.0 / GATE_LOGIT_NORMALIZER))

    log2_decay(0)
    qk_ref[0] = _dot(h, w_ref[0, :, _OFF_QK:_OFF_V]).astype(_bf16)
    log2_decay(1)
    n_slab = 2 * FOURIER_GROUP // LANES
    for g in range(N_FOURIER_GROUPS):
        pq = _dot(u[:, g * FOURIER_GROUP:(g + 1) * FOURIER_GROUP], ab_ref[0, g])
        for s in range(n_slab):
            pq_sc[g * n_slab + s] = pq[:, s * LANES:(s + 1) * LANES]
    log2_decay(2)
    v_ref[0] = _dot(h, w_ref[0, :, _OFF_V:_OFF_R]).astype(_bf16)
    log2_decay(3)
    r_ref[0] = _dot(h, w_ref[0, :, _OFF_R:D_MAIN]).astype(_bf16)

    sub_rows = pq_sc.shape[1] // FFT_RADIX
    for g in range(N_FOURIER_GROUPS):
        for s in range(n_slab):
            dst = p_ref if s < n_slab // 2 else q_ref
            c0 = (s % (n_slab // 2)) * LANES
            for n1 in range(FFT_RADIX):
                rows = pq_sc[g * n_slab + s, pl.ds(n1, sub_rows, stride=FFT_RADIX), :]
                lo = n1 * FOURIER_GROUP + c0
                dst[0, g, :, lo:lo + LANES] = rows.astype(_bf16)


def _inproj(x, mod, norm_g3, w_all, ab, w_gate, wc, bc, layer, tm):
    b, seq, d = x.shape
    row = lambda i, j: (i, j, 0)
    lay3 = lambda i, j: (layer, 0, 0)
    lay4 = lambda i, j: (layer, 0, 0, 0)
    ng = N_FOURIER_GROUPS

    def out(width, dtype=_bf16):
        return (pl.BlockSpec((1, tm, width), row),
                jax.ShapeDtypeStruct((b, seq, width), dtype))

    def grouped():
        width = FFT_RADIX * FOURIER_GROUP
        return (pl.BlockSpec((1, ng, tm // FFT_RADIX, width), lambda i, j: (i, 0, j, 0)),
                jax.ShapeDtypeStruct((b, ng, seq // FFT_RADIX, width), _bf16))

    outs = [grouped(), grouped(), out(1024), out(1024), out(1024), out(1024), out(1024, _f32)]
    return pl.pallas_call(
        _inproj_kernel,
        grid=(b, seq // tm),
        in_specs=[
            pl.BlockSpec((1, tm, d), row),
            pl.BlockSpec((1, 1, 3, d), lambda i, j: (layer, i, 0, 0)),
            pl.BlockSpec((1, 1, d), lay3),
            pl.BlockSpec((1, d, D_MAIN), lay3, pipeline_mode=pl.Buffered(1)),
            pl.BlockSpec((1, ng, FOURIER_GROUP, 2 * FOURIER_GROUP), lay4),
            pl.BlockSpec((1, d, 8 * GATE_RANK), lay3),
            pl.BlockSpec((1, 2 * GATE_RANK, 2 * D_GLA_KEY), lay3),
            pl.BlockSpec((1, 1, 2 * D_GLA_KEY), lay3),
        ],
        out_specs=[o[0] for o in outs],
        out_shape=[o[1] for o in outs],
        scratch_shapes=[pltpu.VMEM((ng * 2 * FOURIER_GROUP // LANES, tm, LANES), _f32)],
        compiler_params=_params(),
        name="inproj",
    )(x, mod, norm_g3, w_all, ab, w_gate, wc, bc)


def _fft_kernel(lm_ref, p_ref, q_ref, o_ref, b_ref):
    sub = p_ref.shape[2]
    grp = o_ref.shape[2]
    for n1 in range(FFT_RADIX):
        cols = slice(n1 * grp, (n1 + 1) * grp)
        b_ref[n1] = (_dot(lm_ref[n1, :, 0:sub], p_ref[0, 0, :, cols])
                     + _dot(lm_ref[n1, :, sub:2 * sub], q_ref[0, 0, :, cols]))

    rt = np.float32(np.sqrt(0.5))

    def combine(c, carry):
        r0 = pl.multiple_of(c * FFT_ROWS, FFT_ROWS)
        for t in range(grp // LANES):
            lanes = slice(t * LANES, (t + 1) * LANES)
            a = [b_ref[n, pl.ds(r0, FFT_ROWS), lanes] for n in range(FFT_RADIX)]
            bi = [b_ref[n, pl.ds(sub + r0, FFT_ROWS), lanes] for n in range(FFT_RADIX)]
            e0, e1 = a[0] + a[4], a[0] - a[4]
            p1, p2, p3 = a[1] + a[7], a[2] + a[6], a[3] + a[5]
            s13, rd, t0 = p1 + p3, rt * (p1 - p3), e0 + p2
            c0, c4, c2, c1, c3 = t0 + s13, t0 - s13, e0 - p2, e1 + rd, e1 - rd
            m1, m2, m3 = bi[1] - bi[7], bi[2] - bi[6], bi[3] - bi[5]
            rs = rt * (m1 + m3)
            s1, s3, s2 = m2 + rs, rs - m2, m1 - m3
            ys = (c0, c1 - s1, c2 - s2, c3 - s3, c4, c3 + s3, c2 + s2, c1 + s1)
            for k1, yk in enumerate(ys):
                o_ref[0, pl.ds(k1 * sub + r0, FFT_ROWS), lanes] = yk.astype(_bf16)
        return carry

    lax.fori_loop(0, sub // FFT_ROWS, combine, 0, unroll=2)


def _fft_matrices(seq):
    sub = seq // FFT_RADIX
    k2 = np.arange(sub)[:, None]
    n2 = np.arange(sub)[None, :]
    mats = []
    for n1 in range(FFT_RADIX):
        ang = 2.0 * np.pi * ((k2 * (n1 + FFT_RADIX * n2)) % seq) / seq
        c, s = np.cos(ang), np.sin(ang)
        mats.append(np.block([[c, -s], [s, c]]))
    return jnp.asarray(np.stack(mats), _f32).astype(_bf16)


def _seqfft(lm, p, q):
    b, ng, sub, width = p.shape
    grp = width // FFT_RADIX
    seq = sub * FFT_RADIX
    dec = pl.BlockSpec((1, 1, sub, FFT_RADIX * grp), lambda i, g: (i, g, 0, 0))
    return pl.pallas_call(
        _fft_kernel,
        grid=(b, ng),
        in_specs=[
            pl.BlockSpec((FFT_RADIX, 2 * sub, 2 * sub), lambda i, g: (0, 0, 0)),
            dec, dec,
        ],
        out_specs=pl.BlockSpec((1, seq, grp), lambda i, g: (i, 0, g)),
        out_shape=jax.ShapeDtypeStruct((b, seq, ng * grp), _bf16),
        scratch_shapes=[pltpu.VMEM((FFT_RADIX, 2 * sub, grp), _f32)],
        compiler_params=_params(),
        name="seqfft",
    )(lm, p, q)


def _gla_kernel(tri_ref, q_ref, k_ref, v_ref, la_ref, g_ref, o_ref,
                oacc_ref, u_ref, qd_ref, dec_ref, s_ref, *, seq):
    n_blocks = seq // GLA_BLOCK
    nb = CHUNKS_PER_BLOCK
    dk = GLA_HEAD_K
    scale = dk ** -0.5

    def chunked(a):
        return a.reshape(nb, CHUNK, a.shape[-1])

    def flat(a):
        return a.reshape(a.shape[0] * a.shape[1], a.shape[2])

    lane = lax.broadcasted_iota(jnp.int32, (CHUNK, 2 * CHUNK), 1)
    row_l = lax.broadcasted_iota(jnp.int32, (CHUNK, 2 * CHUNK), 0)
    keep = (((lane >= CHUNK) | (lane <= row_l), (lane < CHUNK) | (lane - CHUNK <= row_l)),
            ((lane >= CHUNK) | (lane > row_l), (lane < CHUNK) | (lane - CHUNK > row_l)))

    def operands(blk):
        r0 = pl.multiple_of(blk * GLA_BLOCK, GLA_BLOCK)
        rows = pl.ds(r0, GLA_BLOCK)
        la = la_ref[0, rows, :]
        la_hi, la_lo = _split_bf16(la)
        tri = tri_ref[...]
        pre = _dot(tri, la_hi) + _dot(tri, la_lo)
        pre_b = pre[:, dk:]
        g_f = chunked(pre[:, :dk])
        g_b = chunked(pre_b[GLA_BLOCK - 1:GLA_BLOCK, :] - pre_b + la[:, dk:])
        zero = jnp.zeros((1, 1, dk), _f32)
        edge_f = g_f[:, CHUNK - 1:CHUNK, :]
        edge_b = g_b[:, 0:1, :]
        dirs = ((g_f, edge_f, jnp.concatenate([zero, edge_f[:nb - 1]], axis=0),
                 CHUNK // 2, list(range(nb))),
                (g_b, edge_b, jnp.concatenate([edge_b[1:], zero], axis=0),
                 CHUNK // 2 - 1, list(range(nb - 1, -1, -1))))

        qs = chunked(q_ref[0, rows, :].astype(_f32) * scale)
        kk = chunked(k_ref[0, rows, :].astype(_f32))
        zeros_chunk = jnp.zeros((CHUNK, dk), _f32)
        per_dir = []
        for d, (g, edge, base, ref_i, order) in enumerate(dirs):
            tot = edge[order[-1]:order[-1] + 1]
            cum = g - base
            ref = cum[:, ref_i:ref_i + 1, :]
            a = cum - ref
            qi = qs * jnp.exp2(a)
            ki = kk * jnp.exp2(-a)
            last_ref = (edge - base) - ref
            qd_ref[d, rows, :] = flat(qi * jnp.exp2(ref + base)).astype(_bf16)
            kd_t = flat(ki * jnp.exp2(last_ref + (tot - edge))).T.astype(_bf16)
            dec_ref[d, blk] = jnp.broadcast_to(jnp.exp2(tot).reshape(1, dk), (dk, dk)).T
            rhs_t = []
            for i in range(nb):
                pieces = []
                for c in range(nb):
                    if c == i:
                        pieces.append(ki[c])
                    elif order.index(c) < order.index(i):
                        pieces.append(ki[c] * jnp.exp2(last_ref[c] + ref[i] + base[i] - edge[c]))
                    else:
                        pieces.append(zeros_chunk)
                rhs_t.append(jnp.concatenate(pieces, axis=0).T.astype(_bf16))
            per_dir.append((flat(qi).astype(_bf16), rhs_t, kd_t))
        return blk, rows, per_dir

    def scores(ops):
        _, _, per_dir = ops
        return [[_dot(qi[i * CHUNK:(i + 1) * CHUNK], rhs_t[i]) for i in range(nb)]
                for qi, rhs_t, _ in per_dir]

    def outputs(ops, mats):
        blk, rows, per_dir = ops
        score_parts = []
        for d, s_rows in enumerate(mats):
            for i, s_i in enumerate(s_rows):
                tile, half = divmod(i, 2)
                lo, hi = tile * 2 * CHUNK, (tile + 1) * 2 * CHUNK
                masked = jnp.where(keep[d][half], s_i[:, lo:hi], 0.0)
                s_i = jnp.concatenate([masked, s_i[:, hi:]] if tile == 0 else [s_i[:, :lo], masked], axis=1)
                score_parts.append(s_i.astype(_bf16))
        lhs = jnp.concatenate(score_parts + [p[2] for p in per_dir], axis=0)
        res = _dot(lhs, v_ref[0, rows, :])
        oacc_ref[rows, :] = res[0:GLA_BLOCK] + res[GLA_BLOCK:2 * GLA_BLOCK]
        u_ref[0, blk] = res[2 * GLA_BLOCK:2 * GLA_BLOCK + dk]
        u_ref[1, blk] = res[2 * GLA_BLOCK + dk:2 * GLA_BLOCK + 2 * dk]

    def phase_a(i, carry):
        ops0 = operands(2 * i)
        ops1 = operands(2 * i + 1)
        mats0 = scores(ops0)
        mats1 = scores(ops1)
        outputs(ops0, mats0)
        outputs(ops1, mats1)
        return carry

    lax.fori_loop(0, n_blocks // 2, phase_a, 0)

    s_ref[...] = jnp.zeros_like(s_ref)

    def phase_b(i, carry):
        for d in range(2):
            blk = i if d == 0 else n_blocks - 1 - i
            r0 = pl.multiple_of(blk * GLA_BLOCK, GLA_BLOCK)
            rows = pl.ds(r0, GLA_BLOCK)
            s = s_ref[d]
            oacc_ref[rows, :] += _dot(qd_ref[d, rows, :], s.astype(_bf16))
            dec = dec_ref[d, blk]
            s_ref[d] = jnp.concatenate([dec, dec], axis=1) * s + u_ref[d, blk]
        return carry

    lax.fori_loop(0, n_blocks, phase_b, 0, unroll=4)

    def finish(blk, carry):
        r0 = pl.multiple_of(blk * GLA_BLOCK, GLA_BLOCK)
        rows = pl.ds(r0, GLA_BLOCK)
        o = oacc_ref[rows, :]
        o = o * lax.rsqrt(jnp.mean(o * o, axis=-1, keepdims=True) + EPS) * g_ref[0]
        o_ref[0, rows, :] = o.astype(_bf16)
        return carry

    lax.fori_loop(0, n_blocks, finish, 0, unroll=2)


def _gla(tri, qk, v, la, gla_norm_g3, layer):
    b, seq, _ = v.shape
    h = N_GLA_HEADS
    n_blocks = seq // GLA_BLOCK
    return pl.pallas_call(
        functools.partial(_gla_kernel, seq=seq),
        grid=(b, h),
        in_specs=[
            pl.BlockSpec((GLA_BLOCK, GLA_BLOCK), lambda i, j: (0, 0)),
            pl.BlockSpec((1, seq, GLA_HEAD_K), lambda i, j: (i, 0, j)),
            pl.BlockSpec((1, seq, GLA_HEAD_K), lambda i, j: (i, 0, h + j)),
            pl.BlockSpec((1, seq, GLA_HEAD_V), lambda i, j: (i, 0, j)),
            pl.BlockSpec((1, seq, 2 * GLA_HEAD_K), lambda i, j: (i, 0, j)),
            pl.BlockSpec((1, 1, GLA_HEAD_V), lambda i, j: (layer, 0, j)),
        ],
        out_specs=pl.BlockSpec((1, seq, GLA_HEAD_V), lambda i, j: (i, 0, j)),
        out_shape=jax.ShapeDtypeStruct((b, seq, D_GLA), _bf16),
        scratch_shapes=[
            pltpu.VMEM((seq, GLA_HEAD_V), _f32),
            pltpu.VMEM((2, n_blocks, GLA_HEAD_K, GLA_HEAD_V), _f32),
            pltpu.VMEM((2, seq, GLA_HEAD_K), _bf16),
            pltpu.VMEM((2, n_blocks, GLA_HEAD_K, GLA_HEAD_K), _f32),
            pltpu.VMEM((2, GLA_HEAD_K, GLA_HEAD_V), _f32),
        ],
        compiler_params=_params(),
        name="gla",
    )(tri, qk, qk, v, la, gla_norm_g3)


def _outproj_kernel(x_ref, yf_ref, z_ref, yg_ref, r_ref, w_ref, mod_ref, fg_ref, o_ref, *, final):
    y = None
    for y_ref, gate_ref, w0 in ((yf_ref, z_ref, 0), (yg_ref, r_ref, D_FOURIER)):
        for c0 in range(0, y_ref.shape[2], OUTPROJ_K_CHUNK):
            cols = slice(c0, c0 + OUTPROJ_K_CHUNK)
            a = (y_ref[0, :, cols].astype(_f32) * _silu(gate_ref[0, :, cols].astype(_f32))).astype(_bf16)
            part = _dot(a, w_ref[0, w0 + c0:w0 + c0 + OUTPROJ_K_CHUNK, :])
            y = part if y is None else y + part
    gate = mod_ref[0, 0][2:3, :]
    xn = x_ref[0] + gate * y
    if final:
        ms = jnp.mean(xn * xn, axis=-1, keepdims=True)
        xn = xn * lax.rsqrt(ms + EPS) * fg_ref[...]
    o_ref[0] = xn


def _outproj(x, yf, z, yg, r, w_out_b, mod, final_g, layer, tm, final):
    b, seq, d = x.shape
    row = lambda i, j: (i, j, 0)
    return pl.pallas_call(
        functools.partial(_outproj_kernel, final=final),
        grid=(b, seq // tm),
        in_specs=[
            pl.BlockSpec((1, tm, d), row),
            pl.BlockSpec((1, tm, D_FOURIER), row),
            pl.BlockSpec((1, tm, D_FOURIER), row),
            pl.BlockSpec((1, tm, D_GLA), row),
            pl.BlockSpec((1, tm, D_GLA), row),
            pl.BlockSpec((1, D_FOURIER + D_GLA, d), lambda i, j: (layer, 0, 0)),
            pl.BlockSpec((1, 1, 3, d), lambda i, j: (layer, i, 0, 0)),
            pl.BlockSpec((1, d), lambda i, j: (0, 0)),
        ],
        out_specs=pl.BlockSpec((1, tm, d), row),
        out_shape=jax.ShapeDtypeStruct((b, seq, d), _f32),
        compiler_params=_params(),
        name="outproj",
    )(x, yf, z, yg, r, w_out_b, mod, final_g)


def _block_tri():
    idx = np.arange(GLA_BLOCK)
    return jnp.asarray((idx[None, :] <= idx[:, None]).astype(np.float32), _bf16)


def kernel(x, c, norm_g, w_ada, b_ada, w_in, w_fmap, w_af, b_af, w_ab, b_ab, gla_norm_g, w_out, final_g):
    depth = w_in.shape[0]
    b, seq, d = x.shape
    assert d == D_MODEL and seq % (2 * GLA_BLOCK) == 0 and seq % (FFT_RADIX * FFT_ROWS) == 0

    mod = _ada_mod(c, w_ada, b_ada).reshape(depth, b, 3, d)
    ab = _fold_fourier_weights(w_fmap, seq)
    w_in_t = jnp.swapaxes(w_in, 1, 2)
    w_all, w_gate = _prep_weights(w_in_t)
    w_out_b = w_out.astype(_bf16)

    zeros = jnp.zeros_like(w_af)
    wc = jnp.concatenate([jnp.concatenate([w_af, zeros], axis=1),
                          jnp.concatenate([zeros, w_ab], axis=1)], axis=-1)
    bc = jnp.concatenate([b_af, b_ab], axis=-1)

    def per_head(a):
        lead = a.shape[:-1]
        a = a.reshape(lead + (2, N_GLA_HEADS, GLA_HEAD_K))
        return jnp.swapaxes(a, -3, -2).reshape(lead + (2 * D_GLA_KEY,))

    wc = per_head(wc)
    bc = per_head(bc).reshape(depth, 1, 2 * D_GLA_KEY)

    lm = _fft_matrices(seq)
    tri = _block_tri()
    norm_g3 = norm_g.reshape(depth, 1, d)
    gla_norm_g3 = gla_norm_g.reshape(depth, 1, D_GLA)
    final_g2 = final_g.reshape(1, d)

    for l in range(depth):
        p, q, z, qk, v, r, la = _inproj(x, mod, norm_g3, w_all, ab, w_gate, wc, bc, layer=l, tm=512)
        y_f = _seqfft(lm, p, q)
        y_g = _gla(tri, qk, v, la, gla_norm_g3, layer=l)
        x = _outproj(x, y_f, z, y_g, r, w_out_b, mod, final_g2, layer=l, tm=512,
                     final=(l == depth - 1))
    return x
```

```python
import functools

import numpy as np
import jax
import jax.numpy as jnp
from jax import lax
from jax.experimental import pallas as pl
from jax.experimental.pallas import tpu as pltpu

D_MODEL = 1024
D_FOURIER = 1024
N_FOURIER_GROUPS = 4
FOURIER_GROUP = 256
N_GLA_HEADS = 4
D_GLA_KEY = 512
D_GLA = 1024
GLA_HEAD_K = 128
GLA_HEAD_V = 256
GATE_RANK = 16
GATE_LOGIT_NORMALIZER = 16.0
CHUNK = 64
EPS = 1e-6
LOG2_E = 1.4426950408889634

_OFF_Z = D_FOURIER
_OFF_QK = 2 * D_FOURIER
_OFF_V = _OFF_QK + 2 * D_GLA_KEY
_OFF_R = _OFF_V + D_GLA
D_MAIN = _OFF_R + D_GLA

GLA_BLOCK = 256
CHUNKS_PER_BLOCK = GLA_BLOCK // CHUNK

FFT_RADIX = 8
FFT_ROWS = 16
LANES = 128
OUTPROJ_K_CHUNK = 256
GATE_COLS = 256

VMEM_LIMIT_BYTES = 56 * 1024 * 1024

_f32 = jnp.float32
_bf16 = jnp.bfloat16


def _dot(a, b):
    return jnp.dot(a, b, preferred_element_type=_f32)


def _split_bf16(a):
    hi = a.astype(_bf16)
    lo = (a - hi.astype(_f32)).astype(_bf16)
    return hi, lo


def _silu(v):
    return v * (1.0 / (1.0 + jnp.exp(-v)))


def _params():
    return pltpu.CompilerParams(dimension_semantics=("arbitrary", "arbitrary"),
                                vmem_limit_bytes=VMEM_LIMIT_BYTES)


def _ada_kernel(c_ref, w_ref, b_ref, o_ref):
    c_act = _silu(c_ref[...]).astype(_bf16)
    o_ref[0] = _dot(c_act, w_ref[0].astype(_bf16)) + b_ref[0]


def _ada_mod(c, w_ada, b_ada):
    depth = w_ada.shape[0]
    b = c.shape[0]
    n_tiles = 3
    return pl.pallas_call(
        _ada_kernel,
        grid=(depth, n_tiles),
        in_specs=[
            pl.BlockSpec((b, D_MODEL), lambda l, j: (0, 0)),
            pl.BlockSpec((1, D_MODEL, D_MODEL), lambda l, j: (l, 0, j)),
            pl.BlockSpec((1, 1, D_MODEL), lambda l, j: (l, 0, j)),
        ],
        out_specs=pl.BlockSpec((1, b, D_MODEL), lambda l, j: (l, 0, j)),
        out_shape=jax.ShapeDtypeStruct((depth, b, 3 * D_MODEL), _f32),
        compiler_params=_params(),
        name="ada_mod",
    )(c, w_ada, b_ada.reshape(depth, 1, 3 * D_MODEL))


def _fold_kernel(wf_ref, cc_ref, sc_ref, ab_ref, *, norm):
    hp = lax.Precision.HIGHEST
    wf = wf_ref[0, 0]
    a = jnp.dot(cc_ref[...], wf, precision=hp, preferred_element_type=_f32) * norm
    bm = jnp.dot(sc_ref[...], wf, precision=hp, preferred_element_type=_f32) * norm
    ab_ref[0, 0] = jnp.concatenate([a, bm], axis=1).astype(_bf16)


def _fold_fourier_weights(w_fmap, seq_len):
    depth = w_fmap.shape[0]
    idx = np.arange(FOURIER_GROUP)
    ang = 2.0 * np.pi * ((idx[:, None] * idx[None, :]) % FOURIER_GROUP) / FOURIER_GROUP
    cc = jnp.asarray(np.cos(ang), _f32)
    sc = jnp.asarray(np.sin(ang), _f32)
    norm = float(1.0 / np.sqrt(seq_len * FOURIER_GROUP))
    gspec = pl.BlockSpec((FOURIER_GROUP, FOURIER_GROUP), lambda l, g: (0, 0))
    return pl.pallas_call(
        functools.partial(_fold_kernel, norm=norm),
        grid=(depth, N_FOURIER_GROUPS),
        in_specs=[
            pl.BlockSpec((1, 1, FOURIER_GROUP, FOURIER_GROUP), lambda l, g: (l, g, 0, 0)),
            gspec, gspec,
        ],
        out_specs=pl.BlockSpec((1, 1, FOURIER_GROUP, 2 * FOURIER_GROUP), lambda l, g: (l, g, 0, 0)),
        out_shape=jax.ShapeDtypeStruct((depth, N_FOURIER_GROUPS, FOURIER_GROUP, 2 * FOURIER_GROUP), _bf16),
        compiler_params=_params(),
        name="fold_fourier",
    )(w_fmap, cc, sc)


PREP_ROWS = 512


def _prep_main_kernel(wt_ref, o_ref):
    o_ref[0] = wt_ref[0].T.astype(_bf16)


def _prep_gate_kernel(wt_ref, o_ref):
    w = wt_ref[0]
    o_ref[0] = jnp.concatenate([w, w, w, jnp.zeros_like(w)], axis=0).T


def _prep_weights(w_in_t):
    depth, _, d = w_in_t.shape
    rank2 = 2 * GATE_RANK
    w_all = pl.pallas_call(
        _prep_main_kernel,
        grid=(depth, D_MAIN // PREP_ROWS),
        in_specs=[pl.BlockSpec((1, PREP_ROWS, d), lambda l, j: (l, j, 0))],
        out_specs=pl.BlockSpec((1, d, PREP_ROWS), lambda l, j: (l, 0, j)),
        out_shape=jax.ShapeDtypeStruct((depth, d, D_MAIN), _bf16),
        compiler_params=_params(),
        name="prep_main",
    )(w_in_t)
    w_gate = pl.pallas_call(
        _prep_gate_kernel,
        grid=(depth,),
        in_specs=[pl.BlockSpec((1, rank2, d), lambda l: (l, D_MAIN // rank2, 0))],
        out_specs=pl.BlockSpec((1, d, 4 * rank2), lambda l: (l, 0, 0)),
        out_shape=jax.ShapeDtypeStruct((depth, d, 4 * rank2), _f32),
        compiler_params=pltpu.CompilerParams(dimension_semantics=("arbitrary",),
                                             vmem_limit_bytes=VMEM_LIMIT_BYTES),
        name="prep_gate",
    )(w_in_t)
    return w_all, w_gate


def _inproj_kernel(x_ref, mod_ref, g_ref, w_ref, ab_ref, wg_ref, wc_ref, bc_ref,
                   p_ref, q_ref, z_ref, qk_ref, v_ref, r_ref, la_ref, pq_sc):
    x = x_ref[0]
    ms = jnp.mean(x * x, axis=-1, keepdims=True)
    y = x * lax.rsqrt(ms + EPS) * g_ref[0]
    mod = mod_ref[0, 0]
    shift = mod[0:1, :]
    scale = mod[1:2, :]
    h = (y * (1.0 + scale) + shift).astype(_bf16)

    rank2 = 2 * GATE_RANK
    g3 = _dot(h, wg_ref[0].astype(_bf16))
    u = _dot(h, w_ref[0, :, 0:_OFF_Z]).astype(_bf16)
    z_ref[0] = _dot(h, w_ref[0, :, _OFF_Z:_OFF_QK]).astype(_bf16)

    g_hi, g_lo = _split_bf16(g3)
    lane = lax.broadcasted_iota(jnp.int32, g3.shape, 1)
    lhs = jnp.where((lane >= rank2) & (lane < 2 * rank2), g_lo, g_hi)
    w_hi, w_lo = _split_bf16(wc_ref[0])
    rhs = jnp.concatenate([w_hi, w_hi, w_lo, jnp.zeros_like(w_hi)], axis=0)

    def log2_decay(k):
        cols = slice(k * GATE_COLS, (k + 1) * GATE_COLS)
        x_g = _dot(lhs, rhs[:, cols]) + bc_ref[0][:, cols]
        soft = jnp.log2(1.0 + jnp.exp2(jnp.abs(x_g) * -LOG2_E))
        la_ref[0, :, cols] = (jnp.minimum(x_g, 0.0) * (LOG2_E / GATE_LOGIT_NORMALIZER)
                              - soft * (1.0 / GATE_LOGIT_NORMALIZER))

    log2_decay(0)
    qk_ref[0] = _dot(h, w_ref[0, :, _OFF_QK:_OFF_V]).astype(_bf16)
    log2_decay(1)
    n_slab = 2 * FOURIER_GROUP // LANES
    for g in range(N_FOURIER_GROUPS):
        pq = _dot(u[:, g * FOURIER_GROUP:(g + 1) * FOURIER_GROUP], ab_ref[0, g])
        for s in range(n_slab):
            pq_sc[g * n_slab + s] = pq[:, s * LANES:(s + 1) * LANES]
    log2_decay(2)
    v_ref[0] = _dot(h, w_ref[0, :, _OFF_V:_OFF_R]).astype(_bf16)
    log2_decay(3)
    r_ref[0] = _dot(h, w_ref[0, :, _OFF_R:D_MAIN]).astype(_bf16)

    sub_rows = pq_sc.shape[1] // FFT_RADIX
    for g in range(N_FOURIER_GROUPS):
        for s in range(n_slab):
            dst = p_ref if s < n_slab // 2 else q_ref
            c0 = (s % (n_slab // 2)) * LANES
            for n1 in range(FFT_RADIX):
                rows = pq_sc[g * n_slab + s, pl.ds(n1, sub_rows, stride=FFT_RADIX), :]
                lo = n1 * FOURIER_GROUP + c0
                dst[0, g, :, lo:lo + LANES] = rows.astype(_bf16)


def _inproj(x, mod, norm_g3, w_all, ab, w_gate, wc, bc, layer, tm):
    b, seq, d = x.shape
    row = lambda i, j: (i, j, 0)
    lay3 = lambda i, j: (layer, 0, 0)
    lay4 = lambda i, j: (layer, 0, 0, 0)
    ng = N_FOURIER_GROUPS

    def out(width, dtype=_bf16):
        return (pl.BlockSpec((1, tm, width), row),
                jax.ShapeDtypeStruct((b, seq, width), dtype))

    def grouped():
        width = FFT_RADIX * FOURIER_GROUP
        return (pl.BlockSpec((1, ng, tm // FFT_RADIX, width), lambda i, j: (i, 0, j, 0)),
                jax.ShapeDtypeStruct((b, ng, seq // FFT_RADIX, width), _bf16))

    outs = [grouped(), grouped(), out(1024), out(1024), out(1024), out(1024), out(1024, _f32)]
    return pl.pallas_call(
        _inproj_kernel,
        grid=(b, seq // tm),
        in_specs=[
            pl.BlockSpec((1, tm, d), row),
            pl.BlockSpec((1, 1, 3, d), lambda i, j: (layer, i, 0, 0)),
            pl.BlockSpec((1, 1, d), lay3),
            pl.BlockSpec((1, d, D_MAIN), lay3, pipeline_mode=pl.Buffered(1)),
            pl.BlockSpec((1, ng, FOURIER_GROUP, 2 * FOURIER_GROUP), lay4),
            pl.BlockSpec((1, d, 8 * GATE_RANK), lay3),
            pl.BlockSpec((1, 2 * GATE_RANK, 2 * D_GLA_KEY), lay3),
            pl.BlockSpec((1, 1, 2 * D_GLA_KEY), lay3),
        ],
        out_specs=[o[0] for o in outs],
        out_shape=[o[1] for o in outs],
        scratch_shapes=[pltpu.VMEM((ng * 2 * FOURIER_GROUP // LANES, tm, LANES), _f32)],
        compiler_params=_params(),
        name="inproj",
    )(x, mod, norm_g3, w_all, ab, w_gate, wc, bc)


def _fft_kernel(lm_ref, p_ref, q_ref, o_ref, b_ref):
    sub = p_ref.shape[2]
    grp = o_ref.shape[2]
    for n1 in range(FFT_RADIX):
        cols = slice(n1 * grp, (n1 + 1) * grp)
        b_ref[n1] = (_dot(lm_ref[n1, :, 0:sub], p_ref[0, 0, :, cols])
                     + _dot(lm_ref[n1, :, sub:2 * sub], q_ref[0, 0, :, cols]))

    rt = np.float32(np.sqrt(0.5))

    def combine(c, carry):
        r0 = pl.multiple_of(c * FFT_ROWS, FFT_ROWS)
        for t in range(grp // LANES):
            lanes = slice(t * LANES, (t + 1) * LANES)
            a = [b_ref[n, pl.ds(r0, FFT_ROWS), lanes] for n in range(FFT_RADIX)]
            bi = [b_ref[n, pl.ds(sub + r0, FFT_ROWS), lanes] for n in range(FFT_RADIX)]
            e0, e1 = a[0] + a[4], a[0] - a[4]
            p1, p2, p3 = a[1] + a[7], a[2] + a[6], a[3] + a[5]
            s13, rd, t0 = p1 + p3, rt * (p1 - p3), e0 + p2
            c0, c4, c2, c1, c3 = t0 + s13, t0 - s13, e0 - p2, e1 + rd, e1 - rd
            m1, m2, m3 = bi[1] - bi[7], bi[2] - bi[6], bi[3] - bi[5]
            rs = rt * (m1 + m3)
            s1, s3, s2 = m2 + rs, rs - m2, m1 - m3
            ys = (c0, c1 - s1, c2 - s2, c3 - s3, c4, c3 + s3, c2 + s2, c1 + s1)
            for k1, yk in enumerate(ys):
                o_ref[0, pl.ds(k1 * sub + r0, FFT_ROWS), lanes] = yk.astype(_bf16)
        return carry

    lax.fori_loop(0, sub // FFT_ROWS, combine, 0, unroll=2)


def _fft_matrices(seq):
    sub = seq // FFT_RADIX
    k2 = np.arange(sub)[:, None]
    n2 = np.arange(sub)[None, :]
    mats = []
    for n1 in range(FFT_RADIX):
        ang = 2.0 * np.pi * ((k2 * (n1 + FFT_RADIX * n2)) % seq) / seq
        c, s = np.cos(ang), np.sin(ang)
        mats.append(np.block([[c, -s], [s, c]]))
    return jnp.asarray(np.stack(mats), _f32).astype(_bf16)


def _seqfft(lm, p, q):
    b, ng, sub, width = p.shape
    grp = width // FFT_RADIX
    seq = sub * FFT_RADIX
    dec = pl.BlockSpec((1, 1, sub, FFT_RADIX * grp), lambda i, g: (i, g, 0, 0))
    return pl.pallas_call(
        _fft_kernel,
        grid=(b, ng),
        in_specs=[
            pl.BlockSpec((FFT_RADIX, 2 * sub, 2 * sub), lambda i, g: (0, 0, 0)),
            dec, dec,
        ],
        out_specs=pl.BlockSpec((1, seq, grp), lambda i, g: (i, 0, g)),
        out_shape=jax.ShapeDtypeStruct((b, seq, ng * grp), _bf16),
        scratch_shapes=[pltpu.VMEM((FFT_RADIX, 2 * sub, grp), _f32)],
        compiler_params=_params(),
        name="seqfft",
    )(lm, p, q)


def _gla_kernel(tri_ref, q_ref, k_ref, v_ref, la_ref, g_ref, o_ref,
                oacc_ref, u_ref, qd_ref, dec_ref, s_ref, *, seq):
    n_blocks = seq // GLA_BLOCK
    nb = CHUNKS_PER_BLOCK
    dk = GLA_HEAD_K
    scale = dk ** -0.5

    def chunked(a):
        return a.reshape(nb, CHUNK, a.shape[-1])

    def flat(a):
        return a.reshape(a.shape[0] * a.shape[1], a.shape[2])

    lane = lax.broadcasted_iota(jnp.int32, (CHUNK, 2 * CHUNK), 1)
    row_l = lax.broadcasted_iota(jnp.int32, (CHUNK, 2 * CHUNK), 0)
    keep = (((lane >= CHUNK) | (lane <= row_l), (lane < CHUNK) | (lane - CHUNK <= row_l)),
            ((lane >= CHUNK) | (lane > row_l), (lane < CHUNK) | (lane - CHUNK > row_l)))

    def prefix(blk):
        rows = pl.ds(blk * GLA_BLOCK, GLA_BLOCK)
        la = la_ref[0, rows, :]
        la_hi, la_lo = _split_bf16(la)
        tri = tri_ref[...]
        return blk, rows, la, _dot(tri, la_hi) + _dot(tri, la_lo)

    def operands(pre_stage):
        blk, rows, la, pre = pre_stage
        pre_b = pre[:, dk:]
        g_f = chunked(pre[:, :dk])
        g_b = chunked(pre_b[GLA_BLOCK - 1:GLA_BLOCK, :] - pre_b + la[:, dk:])
        zero = jnp.zeros((1, 1, dk), _f32)
        edge_f = g_f[:, CHUNK - 1:CHUNK, :]
        edge_b = g_b[:, 0:1, :]
        dirs = ((g_f, edge_f, jnp.concatenate([zero, edge_f[:nb - 1]], axis=0),
                 CHUNK // 2, list(range(nb))),
                (g_b, edge_b, jnp.concatenate([edge_b[1:], zero], axis=0),
                 CHUNK // 2 - 1, list(range(nb - 1, -1, -1))))

        qs = chunked(q_ref[0, rows, :].astype(_f32) * scale)
        kk = chunked(k_ref[0, rows, :].astype(_f32))
        zeros_chunk = jnp.zeros((CHUNK, dk), _f32)
        per_dir = []
        for d, (g, edge, base, ref_i, order) in enumerate(dirs):
            tot = edge[order[-1]:order[-1] + 1]
            cum = g - base
            ref = cum[:, ref_i:ref_i + 1, :]
            a = cum - ref
            qi = qs * jnp.exp2(a)
            ki = kk * jnp.exp2(-a)
            last_ref = (edge - base) - ref
            qd_ref[d, rows, :] = flat(qi * jnp.exp2(ref + base)).astype(_bf16)
            kd_t = flat(ki * jnp.exp2(last_ref + (tot - edge))).T.astype(_bf16)
            dec_ref[d, blk] = jnp.broadcast_to(jnp.exp2(tot).reshape(1, dk), (dk, dk)).T
            rhs_t = []
            for i in range(nb):
                pieces = []
                for c in range(nb):
                    if c == i:
                        pieces.append(ki[c])
                    elif order.index(c) < order.index(i):
                        pieces.append(ki[c] * jnp.exp2(last_ref[c] + ref[i] + base[i] - edge[c]))
                    else:
                        pieces.append(zeros_chunk)
                rhs_t.append(jnp.concatenate(pieces, axis=0).T.astype(_bf16))
            per_dir.append((flat(qi).astype(_bf16), rhs_t, kd_t))
        return blk, rows, per_dir

    def scores(ops):
        _, _, per_dir = ops
        return [[_dot(qi[i * CHUNK:(i + 1) * CHUNK], rhs_t[i]) for i in range(nb)]
                for qi, rhs_t, _ in per_dir]

    def outputs(ops, mats):
        blk, rows, per_dir = ops
        score_parts = []
        for d, s_rows in enumerate(mats):
            for i, s_i in enumerate(s_rows):
                tile, half = divmod(i, 2)
                lo, hi = tile * 2 * CHUNK, (tile + 1) * 2 * CHUNK
                masked = jnp.where(keep[d][half], s_i[:, lo:hi], 0.0)
                s_i = jnp.concatenate([masked, s_i[:, hi:]] if tile == 0 else [s_i[:, :lo], masked], axis=1)
                score_parts.append(s_i.astype(_bf16))
        lhs = jnp.concatenate(score_parts + [p[2] for p in per_dir], axis=0)
        res = _dot(lhs, v_ref[0, rows, :])
        oacc_ref[rows, :] = res[0:GLA_BLOCK] + res[GLA_BLOCK:2 * GLA_BLOCK]
        u_ref[0, blk] = res[2 * GLA_BLOCK:2 * GLA_BLOCK + dk]
        u_ref[1, blk] = res[2 * GLA_BLOCK + dk:2 * GLA_BLOCK + 2 * dk]

    n_pairs = n_blocks // 2
    pre_q, ops_q = {}, {}
    for t in range(n_pairs + 2):
        if t < n_pairs:
            pre_q[t] = [prefix(2 * t), prefix(2 * t + 1)]
        if 0 <= t - 1 < n_pairs:
            ops_q[t - 1] = [operands(p) for p in pre_q.pop(t - 1)]
        if 0 <= t - 2 < n_pairs:
            pair = ops_q.pop(t - 2)
            mats = [scores(o) for o in pair]
            for o, m in zip(pair, mats):
                outputs(o, m)

    s_ref[...] = jnp.zeros_like(s_ref)

    def phase_b(i, carry):
        for d in range(2):
            blk = i if d == 0 else n_blocks - 1 - i
            r0 = pl.multiple_of(blk * GLA_BLOCK, GLA_BLOCK)
            rows = pl.ds(r0, GLA_BLOCK)
            s = s_ref[d]
            oacc_ref[rows, :] += _dot(qd_ref[d, rows, :], s.astype(_bf16))
            dec = dec_ref[d, blk]
            s_ref[d] = jnp.concatenate([dec, dec], axis=1) * s + u_ref[d, blk]
        return carry

    lax.fori_loop(0, n_blocks, phase_b, 0, unroll=4)

    def finish(blk, carry):
        r0 = pl.multiple_of(blk * GLA_BLOCK, GLA_BLOCK)
        rows = pl.ds(r0, GLA_BLOCK)
        o = oacc_ref[rows, :]
        o = o * lax.rsqrt(jnp.mean(o * o, axis=-1, keepdims=True) + EPS) * g_ref[0]
        o_ref[0, rows, :] = o.astype(_bf16)
        return carry

    lax.fori_loop(0, n_blocks, finish, 0, unroll=2)


def _gla(tri, qk, v, la, gla_norm_g3, layer):
    b, seq, _ = v.shape
    h = N_GLA_HEADS
    n_blocks = seq // GLA_BLOCK
    return pl.pallas_call(
        functools.partial(_gla_kernel, seq=seq),
        grid=(b, h),
        in_specs=[
            pl.BlockSpec((GLA_BLOCK, GLA_BLOCK), lambda i, j: (0, 0)),
            pl.BlockSpec((1, seq, GLA_HEAD_K), lambda i, j: (i, 0, j)),
            pl.BlockSpec((1, seq, GLA_HEAD_K), lambda i, j: (i, 0, h + j)),
            pl.BlockSpec((1, seq, GLA_HEAD_V), lambda i, j: (i, 0, j)),
            pl.BlockSpec((1, seq, 2 * GLA_HEAD_K), lambda i, j: (i, 0, j)),
            pl.BlockSpec((1, 1, GLA_HEAD_V), lambda i, j: (layer, 0, j)),
        ],
        out_specs=pl.BlockSpec((1, seq, GLA_HEAD_V), lambda i, j: (i, 0, j)),
        out_shape=jax.ShapeDtypeStruct((b, seq, D_GLA), _bf16),
        scratch_shapes=[
            pltpu.VMEM((seq, GLA_HEAD_V), _f32),
            pltpu.VMEM((2, n_blocks, GLA_HEAD_K, GLA_HEAD_V), _f32),
            pltpu.VMEM((2, seq, GLA_HEAD_K), _bf16),
            pltpu.VMEM((2, n_blocks, GLA_HEAD_K, GLA_HEAD_K), _f32),
            pltpu.VMEM((2, GLA_HEAD_K, GLA_HEAD_V), _f32),
        ],
        compiler_params=_params(),
        name="gla",
    )(tri, qk, qk, v, la, gla_norm_g3)


def _outproj_kernel(x_ref, yf_ref, z_ref, yg_ref, r_ref, w_ref, mod_ref, fg_ref, o_ref, *, final):
    y = None
    for y_ref, gate_ref, w0 in ((yf_ref, z_ref, 0), (yg_ref, r_ref, D_FOURIER)):
        for c0 in range(0, y_ref.shape[2], OUTPROJ_K_CHUNK):
            cols = slice(c0, c0 + OUTPROJ_K_CHUNK)
            a = (y_ref[0, :, cols].astype(_f32) * _silu(gate_ref[0, :, cols].astype(_f32))).astype(_bf16)
            part = _dot(a, w_ref[0, w0 + c0:w0 + c0 + OUTPROJ_K_CHUNK, :])
            y = part if y is None else y + part
    gate = mod_ref[0, 0][2:3, :]
    xn = x_ref[0] + gate * y
    if final:
        ms = jnp.mean(xn * xn, axis=-1, keepdims=True)
        xn = xn * lax.rsqrt(ms + EPS) * fg_ref[...]
    o_ref[0] = xn


def _outproj(x, yf, z, yg, r, w_out_b, mod, final_g, layer, tm, final):
    b, seq, d = x.shape
    row = lambda i, j: (i, j, 0)
    return pl.pallas_call(
        functools.partial(_outproj_kernel, final=final),
        grid=(b, seq // tm),
        in_specs=[
            pl.BlockSpec((1, tm, d), row),
            pl.BlockSpec((1, tm, D_FOURIER), row),
            pl.BlockSpec((1, tm, D_FOURIER), row),
            pl.BlockSpec((1, tm, D_GLA), row),
            pl.BlockSpec((1, tm, D_GLA), row),
            pl.BlockSpec((1, D_FOURIER + D_GLA, d), lambda i, j: (layer, 0, 0)),
            pl.BlockSpec((1, 1, 3, d), lambda i, j: (layer, i, 0, 0)),
            pl.BlockSpec((1, d), lambda i, j: (0, 0)),
        ],
        out_specs=pl.BlockSpec((1, tm, d), row),
        out_shape=jax.ShapeDtypeStruct((b, seq, d), _f32),
        compiler_params=_params(),
        name="outproj",
    )(x, yf, z, yg, r, w_out_b, mod, final_g)


def _block_tri():
    idx = np.arange(GLA_BLOCK)
    return jnp.asarray((idx[None, :] <= idx[:, None]).astype(np.float32), _bf16)


def kernel(x, c, norm_g, w_ada, b_ada, w_in, w_fmap, w_af, b_af, w_ab, b_ab, gla_norm_g, w_out, final_g):
    depth = w_in.shape[0]
    b, seq, d = x.shape
    assert d == D_MODEL and seq % (2 * GLA_BLOCK) == 0 and seq % (FFT_RADIX * FFT_ROWS) == 0

    mod = _ada_mod(c, w_ada, b_ada).reshape(depth, b, 3, d)
    ab = _fold_fourier_weights(w_fmap, seq)
    w_in_t = jnp.swapaxes(w_in, 1, 2)
    w_all, w_gate = _prep_weights(w_in_t)
    w_out_b = w_out.astype(_bf16)

    zeros = jnp.zeros_like(w_af)
    wc = jnp.concatenate([jnp.concatenate([w_af, zeros], axis=1),
                          jnp.concatenate([zeros, w_ab], axis=1)], axis=-1)
    bc = jnp.concatenate([b_af, b_ab], axis=-1)

    def per_head(a):
        lead = a.shape[:-1]
        a = a.reshape(lead + (2, N_GLA_HEADS, GLA_HEAD_K))
        return jnp.swapaxes(a, -3, -2).reshape(lead + (2 * D_GLA_KEY,))

    wc = per_head(wc)
    bc = per_head(bc).reshape(depth, 1, 2 * D_GLA_KEY)

    lm = _fft_matrices(seq)
    tri = _block_tri()
    norm_g3 = norm_g.reshape(depth, 1, d)
    gla_norm_g3 = gla_norm_g.reshape(depth, 1, D_GLA)
    final_g2 = final_g.reshape(1, d)

    for l in range(depth):
        p, q, z, qk, v, r, la = _inproj(x, mod, norm_g3, w_all, ab, w_gate, wc, bc, layer=l, tm=512)
        y_f = _seqfft(lm, p, q)
        y_g = _gla(tri, qk, v, la, gla_norm_g3, layer=l)
        x = _outproj(x, y_f, z, y_g, r, w_out_b, mod, final_g2, layer=l, tm=512,
                     final=(l == depth - 1))
    return x
```

```python
import functools

import numpy as np
import jax
import jax.numpy as jnp
from jax import lax
from jax.experimental import pallas as pl
from jax.experimental.pallas import tpu as pltpu

D_MODEL = 1024
D_FOURIER = 1024
N_FOURIER_GROUPS = 4
FOURIER_GROUP = 256
N_GLA_HEADS = 4
D_GLA_KEY = 512
D_GLA = 1024
GLA_HEAD_K = 128
GLA_HEAD_V = 256
GATE_RANK = 16
GATE_LOGIT_NORMALIZER = 16.0
CHUNK = 64
EPS = 1e-6
LOG2_E = 1.4426950408889634

_OFF_Z = D_FOURIER
_OFF_QK = 2 * D_FOURIER
_OFF_V = _OFF_QK + 2 * D_GLA_KEY
_OFF_R = _OFF_V + D_GLA
D_MAIN = _OFF_R + D_GLA

GLA_BLOCK = 256
CHUNKS_PER_BLOCK = GLA_BLOCK // CHUNK

FFT_RADIX = 8
FFT_ROWS = 16
LANES = 128
OUTPROJ_K_CHUNK = 256
GATE_COLS = 256

VMEM_LIMIT_BYTES = 56 * 1024 * 1024

_f32 = jnp.float32
_bf16 = jnp.bfloat16


def _dot(a, b):
    return jnp.dot(a, b, preferred_element_type=_f32)


def _split_bf16(a):
    hi = a.astype(_bf16)
    lo = (a - hi.astype(_f32)).astype(_bf16)
    return hi, lo


def _silu(v):
    return v * (1.0 / (1.0 + jnp.exp(-v)))


def _params():
    return pltpu.CompilerParams(dimension_semantics=("arbitrary", "arbitrary"),
                                vmem_limit_bytes=VMEM_LIMIT_BYTES)


def _ada_kernel(c_ref, w_ref, b_ref, o_ref):
    c_act = _silu(c_ref[...]).astype(_bf16)
    o_ref[0] = _dot(c_act, w_ref[0].astype(_bf16)) + b_ref[0]


def _ada_mod(c, w_ada, b_ada):
    depth = w_ada.shape[0]
    b = c.shape[0]
    n_tiles = 3
    return pl.pallas_call(
        _ada_kernel,
        grid=(depth, n_tiles),
        in_specs=[
            pl.BlockSpec((b, D_MODEL), lambda l, j: (0, 0)),
            pl.BlockSpec((1, D_MODEL, D_MODEL), lambda l, j: (l, 0, j)),
            pl.BlockSpec((1, 1, D_MODEL), lambda l, j: (l, 0, j)),
        ],
        out_specs=pl.BlockSpec((1, b, D_MODEL), lambda l, j: (l, 0, j)),
        out_shape=jax.ShapeDtypeStruct((depth, b, 3 * D_MODEL), _f32),
        compiler_params=_params(),
        name="ada_mod",
    )(c, w_ada, b_ada.reshape(depth, 1, 3 * D_MODEL))


def _fold_kernel(wf_ref, cc_ref, sc_ref, ab_ref, *, norm):
    hp = lax.Precision.HIGHEST
    for g in range(N_FOURIER_GROUPS):
        wf = wf_ref[0, g]
        a = jnp.dot(cc_ref[...], wf, precision=hp, preferred_element_type=_f32) * norm
        bm = jnp.dot(sc_ref[...], wf, precision=hp, preferred_element_type=_f32) * norm
        ab_ref[0, g] = jnp.concatenate([a, bm], axis=1).astype(_bf16)


def _fold_fourier_weights(w_fmap, seq_len):
    depth = w_fmap.shape[0]
    idx = np.arange(FOURIER_GROUP)
    ang = 2.0 * np.pi * ((idx[:, None] * idx[None, :]) % FOURIER_GROUP) / FOURIER_GROUP
    cc = jnp.asarray(np.cos(ang), _f32)
    sc = jnp.asarray(np.sin(ang), _f32)
    norm = float(1.0 / np.sqrt(seq_len * FOURIER_GROUP))
    ng = N_FOURIER_GROUPS
    gspec = pl.BlockSpec((FOURIER_GROUP, FOURIER_GROUP), lambda l: (0, 0))
    return pl.pallas_call(
        functools.partial(_fold_kernel, norm=norm),
        grid=(depth,),
        in_specs=[
            pl.BlockSpec((1, ng, FOURIER_GROUP, FOURIER_GROUP), lambda l: (l, 0, 0, 0)),
            gspec, gspec,
        ],
        out_specs=pl.BlockSpec((1, ng, FOURIER_GROUP, 2 * FOURIER_GROUP), lambda l: (l, 0, 0, 0)),
        out_shape=jax.ShapeDtypeStruct((depth, ng, FOURIER_GROUP, 2 * FOURIER_GROUP), _bf16),
        compiler_params=pltpu.CompilerParams(dimension_semantics=("arbitrary",),
                                             vmem_limit_bytes=VMEM_LIMIT_BYTES),
        name="fold_fourier",
    )(w_fmap, cc, sc)


PREP_ROWS = 512


def _prep_main_kernel(wt_ref, o_ref):
    o_ref[0] = wt_ref[0].T.astype(_bf16)


def _prep_gate_kernel(wt_ref, o_ref):
    w = wt_ref[0]
    o_ref[0] = jnp.concatenate([w, w, w, jnp.zeros_like(w)], axis=0).T


def _prep_weights(w_in_t):
    depth, _, d = w_in_t.shape
    rank2 = 2 * GATE_RANK
    w_all = pl.pallas_call(
        _prep_main_kernel,
        grid=(depth, D_MAIN // PREP_ROWS),
        in_specs=[pl.BlockSpec((1, PREP_ROWS, d), lambda l, j: (l, j, 0))],
        out_specs=pl.BlockSpec((1, d, PREP_ROWS), lambda l, j: (l, 0, j)),
        out_shape=jax.ShapeDtypeStruct((depth, d, D_MAIN), _bf16),
        compiler_params=_params(),
        name="prep_main",
    )(w_in_t)
    w_gate = pl.pallas_call(
        _prep_gate_kernel,
        grid=(depth,),
        in_specs=[pl.BlockSpec((1, rank2, d), lambda l: (l, D_MAIN // rank2, 0))],
        out_specs=pl.BlockSpec((1, d, 4 * rank2), lambda l: (l, 0, 0)),
        out_shape=jax.ShapeDtypeStruct((depth, d, 4 * rank2), _f32),
        compiler_params=pltpu.CompilerParams(dimension_semantics=("arbitrary",),
                                             vmem_limit_bytes=VMEM_LIMIT_BYTES),
        name="prep_gate",
    )(w_in_t)
    return w_all, w_gate


def _inproj_kernel(x_ref, mod_ref, g_ref, w_ref, ab_ref, wg_ref, wc_ref, bc_ref,
                   p_ref, q_ref, z_ref, qk_ref, v_ref, r_ref, la_ref, pq_sc):
    x = x_ref[0]
    ms = jnp.mean(x * x, axis=-1, keepdims=True)
    y = x * lax.rsqrt(ms + EPS) * g_ref[0]
    mod = mod_ref[0, 0]
    shift = mod[0:1, :]
    scale = mod[1:2, :]
    h = (y * (1.0 + scale) + shift).astype(_bf16)

    rank2 = 2 * GATE_RANK
    g3 = _dot(h, wg_ref[0].astype(_bf16))
    u = _dot(h, w_ref[0, :, 0:_OFF_Z]).astype(_bf16)
    z_ref[0] = _dot(h, w_ref[0, :, _OFF_Z:_OFF_QK]).astype(_bf16)

    g_hi, g_lo = _split_bf16(g3)
    lane = lax.broadcasted_iota(jnp.int32, g3.shape, 1)
    lhs = jnp.where((lane >= rank2) & (lane < 2 * rank2), g_lo, g_hi)
    w_hi, w_lo = _split_bf16(wc_ref[0])
    rhs = jnp.concatenate([w_hi, w_hi, w_lo, jnp.zeros_like(w_hi)], axis=0)

    def log2_decay(k):
        cols = slice(k * GATE_COLS, (k + 1) * GATE_COLS)
        x_g = _dot(lhs, rhs[:, cols]) + bc_ref[0][:, cols]
        soft = jnp.log2(1.0 + jnp.exp2(jnp.abs(x_g) * -LOG2_E))
        la_ref[0, :, cols] = (jnp.minimum(x_g, 0.0) * (LOG2_E / GATE_LOGIT_NORMALIZER)
                              - soft * (1.0 / GATE_LOGIT_NORMALIZER))

    log2_decay(0)
    qk_ref[0] = _dot(h, w_ref[0, :, _OFF_QK:_OFF_V]).astype(_bf16)
    log2_decay(1)
    n_slab = 2 * FOURIER_GROUP // LANES
    for g in range(N_FOURIER_GROUPS):
        pq = _dot(u[:, g * FOURIER_GROUP:(g + 1) * FOURIER_GROUP], ab_ref[0, g])
        for s in range(n_slab):
            pq_sc[g * n_slab + s] = pq[:, s * LANES:(s + 1) * LANES]
    log2_decay(2)
    v_ref[0] = _dot(h, w_ref[0, :, _OFF_V:_OFF_R]).astype(_bf16)
    log2_decay(3)
    r_ref[0] = _dot(h, w_ref[0, :, _OFF_R:D_MAIN]).astype(_bf16)

    sub_rows = pq_sc.shape[1] // FFT_RADIX
    for g in range(N_FOURIER_GROUPS):
        for s in range(n_slab):
            dst = p_ref if s < n_slab // 2 else q_ref
            c0 = (s % (n_slab // 2)) * LANES
            for n1 in range(FFT_RADIX):
                rows = pq_sc[g * n_slab + s, pl.ds(n1, sub_rows, stride=FFT_RADIX), :]
                lo = n1 * FOURIER_GROUP + c0
                dst[0, g, :, lo:lo + LANES] = rows.astype(_bf16)


def _inproj(x, mod, norm_g3, w_all, ab, w_gate, wc, bc, layer, tm):
    b, seq, d = x.shape
    row = lambda i, j: (i, j, 0)
    lay3 = lambda i, j: (layer, 0, 0)
    lay4 = lambda i, j: (layer, 0, 0, 0)
    ng = N_FOURIER_GROUPS

    def out(width, dtype=_bf16):
        return (pl.BlockSpec((1, tm, width), row),
                jax.ShapeDtypeStruct((b, seq, width), dtype))

    def grouped():
        width = FFT_RADIX * FOURIER_GROUP
        return (pl.BlockSpec((1, ng, tm // FFT_RADIX, width), lambda i, j: (i, 0, j, 0)),
                jax.ShapeDtypeStruct((b, ng, seq // FFT_RADIX, width), _bf16))

    outs = [grouped(), grouped(), out(1024), out(1024), out(1024), out(1024), out(1024, _f32)]
    return pl.pallas_call(
        _inproj_kernel,
        grid=(b, seq // tm),
        in_specs=[
            pl.BlockSpec((1, tm, d), row),
            pl.BlockSpec((1, 1, 3, d), lambda i, j: (layer, i, 0, 0)),
            pl.BlockSpec((1, 1, d), lay3),
            pl.BlockSpec((1, d, D_MAIN), lay3, pipeline_mode=pl.Buffered(1)),
            pl.BlockSpec((1, ng, FOURIER_GROUP, 2 * FOURIER_GROUP), lay4),
            pl.BlockSpec((1, d, 8 * GATE_RANK), lay3),
            pl.BlockSpec((1, 2 * GATE_RANK, 2 * D_GLA_KEY), lay3),
            pl.BlockSpec((1, 1, 2 * D_GLA_KEY), lay3),
        ],
        out_specs=[o[0] for o in outs],
        out_shape=[o[1] for o in outs],
        scratch_shapes=[pltpu.VMEM((ng * 2 * FOURIER_GROUP // LANES, tm, LANES), _f32)],
        compiler_params=_params(),
        name="inproj",
    )(x, mod, norm_g3, w_all, ab, w_gate, wc, bc)


def _fft_kernel(lm_ref, p_ref, q_ref, o_ref, b_ref):
    sub = p_ref.shape[2]
    grp = o_ref.shape[2]
    for n1 in range(FFT_RADIX):
        cols = slice(n1 * grp, (n1 + 1) * grp)
        b_ref[n1] = (_dot(lm_ref[n1, :, 0:sub], p_ref[0, 0, :, cols])
                     + _dot(lm_ref[n1, :, sub:2 * sub], q_ref[0, 0, :, cols]))

    rt = np.float32(np.sqrt(0.5))

    def combine(c, carry):
        r0 = pl.multiple_of(c * FFT_ROWS, FFT_ROWS)
        for t in range(grp // LANES):
            lanes = slice(t * LANES, (t + 1) * LANES)
            a = [b_ref[n, pl.ds(r0, FFT_ROWS), lanes] for n in range(FFT_RADIX)]
            bi = [b_ref[n, pl.ds(sub + r0, FFT_ROWS), lanes] for n in range(FFT_RADIX)]
            e0, e1 = a[0] + a[4], a[0] - a[4]
            p1, p2, p3 = a[1] + a[7], a[2] + a[6], a[3] + a[5]
            s13, rd, t0 = p1 + p3, rt * (p1 - p3), e0 + p2
            c0, c4, c2, c1, c3 = t0 + s13, t0 - s13, e0 - p2, e1 + rd, e1 - rd
            m1, m2, m3 = bi[1] - bi[7], bi[2] - bi[6], bi[3] - bi[5]
            rs = rt * (m1 + m3)
            s1, s3, s2 = m2 + rs, rs - m2, m1 - m3
            ys = (c0, c1 - s1, c2 - s2, c3 - s3, c4, c3 + s3, c2 + s2, c1 + s1)
            for k1, yk in enumerate(ys):
                o_ref[0, pl.ds(k1 * sub + r0, FFT_ROWS), lanes] = yk.astype(_bf16)
        return carry

    lax.fori_loop(0, sub // FFT_ROWS, combine, 0, unroll=2)


def _fft_matrices(seq):
    sub = seq // FFT_RADIX
    k2 = np.arange(sub)[:, None]
    n2 = np.arange(sub)[None, :]
    mats = []
    for n1 in range(FFT_RADIX):
        ang = 2.0 * np.pi * ((k2 * (n1 + FFT_RADIX * n2)) % seq) / seq
        c, s = np.cos(ang), np.sin(ang)
        mats.append(np.block([[c, -s], [s, c]]))
    return jnp.asarray(np.stack(mats), _f32).astype(_bf16)


def _seqfft(lm, p, q):
    b, ng, sub, width = p.shape
    grp = width // FFT_RADIX
    seq = sub * FFT_RADIX
    dec = pl.BlockSpec((1, 1, sub, FFT_RADIX * grp), lambda i, g: (i, g, 0, 0))
    return pl.pallas_call(
        _fft_kernel,
        grid=(b, ng),
        in_specs=[
            pl.BlockSpec((FFT_RADIX, 2 * sub, 2 * sub), lambda i, g: (0, 0, 0)),
            dec, dec,
        ],
        out_specs=pl.BlockSpec((1, seq, grp), lambda i, g: (i, 0, g)),
        out_shape=jax.ShapeDtypeStruct((b, seq, ng * grp), _bf16),
        scratch_shapes=[pltpu.VMEM((FFT_RADIX, 2 * sub, grp), _f32)],
        compiler_params=_params(),
        name="seqfft",
    )(lm, p, q)


def _gla_kernel(tri_ref, q_ref, k_ref, v_ref, la_ref, g_ref, o_ref,
                oacc_ref, u_ref, qd_ref, dec_ref, *, seq):
    n_blocks = seq // GLA_BLOCK
    nb = CHUNKS_PER_BLOCK
    dk = GLA_HEAD_K
    scale = dk ** -0.5

    def chunked(a):
        return a.reshape(nb, CHUNK, a.shape[-1])

    def flat(a):
        return a.reshape(a.shape[0] * a.shape[1], a.shape[2])

    lane = lax.broadcasted_iota(jnp.int32, (CHUNK, 2 * CHUNK), 1)
    row_l = lax.broadcasted_iota(jnp.int32, (CHUNK, 2 * CHUNK), 0)
    keep = (((lane >= CHUNK) | (lane <= row_l), (lane < CHUNK) | (lane - CHUNK <= row_l)),
            ((lane >= CHUNK) | (lane > row_l), (lane < CHUNK) | (lane - CHUNK > row_l)))

    def prefix(blk):
        rows = pl.ds(blk * GLA_BLOCK, GLA_BLOCK)
        la = la_ref[0, rows, :]
        la_hi, la_lo = _split_bf16(la)
        tri = tri_ref[...]
        return blk, rows, la, _dot(tri, la_hi) + _dot(tri, la_lo)

    def operands(pre_stage):
        blk, rows, la, pre = pre_stage
        pre_b = pre[:, dk:]
        g_f = chunked(pre[:, :dk])
        g_b = chunked(pre_b[GLA_BLOCK - 1:GLA_BLOCK, :] - pre_b + la[:, dk:])
        zero = jnp.zeros((1, 1, dk), _f32)
        edge_f = g_f[:, CHUNK - 1:CHUNK, :]
        edge_b = g_b[:, 0:1, :]
        dirs = ((g_f, edge_f, jnp.concatenate([zero, edge_f[:nb - 1]], axis=0),
                 CHUNK // 2, list(range(nb))),
                (g_b, edge_b, jnp.concatenate([edge_b[1:], zero], axis=0),
                 CHUNK // 2 - 1, list(range(nb - 1, -1, -1))))

        qs = chunked(q_ref[0, rows, :].astype(_f32) * scale)
        kk = chunked(k_ref[0, rows, :].astype(_f32))
        zeros_chunk = jnp.zeros((CHUNK, dk), _f32)
        per_dir = []
        for d, (g, edge, base, ref_i, order) in enumerate(dirs):
            tot = edge[order[-1]:order[-1] + 1]
            cum = g - base
            ref = cum[:, ref_i:ref_i + 1, :]
            a = cum - ref
            qi = qs * jnp.exp2(a)
            ki = kk * jnp.exp2(-a)
            last_ref = (edge - base) - ref
            qd_ref[d, rows, :] = flat(qi * jnp.exp2(ref + base)).astype(_bf16)
            kd_t = flat(ki * jnp.exp2(last_ref + (tot - edge))).T.astype(_bf16)
            dec_ref[d, blk] = jnp.broadcast_to(jnp.exp2(tot).reshape(1, dk), (dk, dk)).T
            rhs_t = []
            for i in range(nb):
                pieces = []
                for c in range(nb):
                    if c == i:
                        pieces.append(ki[c])
                    elif order.index(c) < order.index(i):
                        pieces.append(ki[c] * jnp.exp2(last_ref[c] + ref[i] + base[i] - edge[c]))
                    else:
                        pieces.append(zeros_chunk)
                rhs_t.append(jnp.concatenate(pieces, axis=0).T.astype(_bf16))
            per_dir.append((flat(qi).astype(_bf16), rhs_t, kd_t))
        return blk, rows, per_dir

    def scores(ops):
        _, _, per_dir = ops
        return [[_dot(qi[i * CHUNK:(i + 1) * CHUNK], rhs_t[i]) for i in range(nb)]
                for qi, rhs_t, _ in per_dir]

    def outputs(ops, mats):
        blk, rows, per_dir = ops
        score_parts = []
        for d, s_rows in enumerate(mats):
            for i, s_i in enumerate(s_rows):
                tile, half = divmod(i, 2)
                lo, hi = tile * 2 * CHUNK, (tile + 1) * 2 * CHUNK
                masked = jnp.where(keep[d][half], s_i[:, lo:hi], 0.0)
                s_i = jnp.concatenate([masked, s_i[:, hi:]] if tile == 0 else [s_i[:, :lo], masked], axis=1)
                score_parts.append(s_i.astype(_bf16))
        lhs = jnp.concatenate(score_parts + [p[2] for p in per_dir], axis=0)
        res = _dot(lhs, v_ref[0, rows, :])
        oacc_ref[rows, :] = res[0:GLA_BLOCK] + res[GLA_BLOCK:2 * GLA_BLOCK]
        u_ref[0, blk] = res[2 * GLA_BLOCK:2 * GLA_BLOCK + dk]
        u_ref[1, blk] = res[2 * GLA_BLOCK + dk:2 * GLA_BLOCK + 2 * dk]

    n_pairs = n_blocks // 2
    pre_q, ops_q = {}, {}
    for t in range(n_pairs + 2):
        if t < n_pairs:
            pre_q[t] = [prefix(2 * t), prefix(2 * t + 1)]
        if 0 <= t - 1 < n_pairs:
            ops_q[t - 1] = [operands(p) for p in pre_q.pop(t - 1)]
        if 0 <= t - 2 < n_pairs:
            pair = ops_q.pop(t - 2)
            mats = [scores(o) for o in pair]
            for o, m in zip(pair, mats):
                outputs(o, m)

    def finish(blk):
        rows = pl.ds(blk * GLA_BLOCK, GLA_BLOCK)
        o = oacc_ref[rows, :]
        o = o * lax.rsqrt(jnp.mean(o * o, axis=-1, keepdims=True) + EPS) * g_ref[0]
        o_ref[0, rows, :] = o.astype(_bf16)

    states = [None, None]
    for i in range(n_blocks):
        for d in range(2):
            blk = i if d == 0 else n_blocks - 1 - i
            rows = pl.ds(blk * GLA_BLOCK, GLA_BLOCK)
            s = states[d]
            if s is None:
                states[d] = u_ref[d, blk]
                continue
            oacc_ref[rows, :] += _dot(qd_ref[d, rows, :], s.astype(_bf16))
            if i + 1 < n_blocks:
                dec = dec_ref[d, blk]
                states[d] = jnp.concatenate([dec, dec], axis=1) * s + u_ref[d, blk]
        if 2 * i + 1 >= n_blocks:
            for blk in sorted({i, n_blocks - 1 - i}):
                finish(blk)


def _gla(tri, qk, v, la, gla_norm_g3, layer):
    b, seq, _ = v.shape
    h = N_GLA_HEADS
    n_blocks = seq // GLA_BLOCK
    return pl.pallas_call(
        functools.partial(_gla_kernel, seq=seq),
        grid=(b, h),
        in_specs=[
            pl.BlockSpec((GLA_BLOCK, GLA_BLOCK), lambda i, j: (0, 0)),
            pl.BlockSpec((1, seq, GLA_HEAD_K), lambda i, j: (i, 0, j)),
            pl.BlockSpec((1, seq, GLA_HEAD_K), lambda i, j: (i, 0, h + j)),
            pl.BlockSpec((1, seq, GLA_HEAD_V), lambda i, j: (i, 0, j)),
            pl.BlockSpec((1, seq, 2 * GLA_HEAD_K), lambda i, j: (i, 0, j)),
            pl.BlockSpec((1, 1, GLA_HEAD_V), lambda i, j: (layer, 0, j)),
        ],
        out_specs=pl.BlockSpec((1, seq, GLA_HEAD_V), lambda i, j: (i, 0, j)),
        out_shape=jax.ShapeDtypeStruct((b, seq, D_GLA), _bf16),
        scratch_shapes=[
            pltpu.VMEM((seq, GLA_HEAD_V), _f32),
            pltpu.VMEM((2, n_blocks, GLA_HEAD_K, GLA_HEAD_V), _f32),
            pltpu.VMEM((2, seq, GLA_HEAD_K), _bf16),
            pltpu.VMEM((2, n_blocks, GLA_HEAD_K, GLA_HEAD_K), _f32),
        ],
        compiler_params=_params(),
        name="gla",
    )(tri, qk, qk, v, la, gla_norm_g3)


def _outproj_kernel(x_ref, yf_ref, z_ref, yg_ref, r_ref, w_ref, mod_ref, fg_ref, o_ref, *, final):
    y = None
    for y_ref, gate_ref, w0 in ((yf_ref, z_ref, 0), (yg_ref, r_ref, D_FOURIER)):
        for c0 in range(0, y_ref.shape[2], OUTPROJ_K_CHUNK):
            cols = slice(c0, c0 + OUTPROJ_K_CHUNK)
            a = (y_ref[0, :, cols].astype(_f32) * _silu(gate_ref[0, :, cols].astype(_f32))).astype(_bf16)
            part = _dot(a, w_ref[0, w0 + c0:w0 + c0 + OUTPROJ_K_CHUNK, :].astype(_bf16))
            y = part if y is None else y + part
    gate = mod_ref[0, 0][2:3, :]
    xn = x_ref[0] + gate * y
    if final:
        ms = jnp.mean(xn * xn, axis=-1, keepdims=True)
        xn = xn * lax.rsqrt(ms + EPS) * fg_ref[...]
    o_ref[0] = xn


def _outproj(x, yf, z, yg, r, w_out, mod, final_g, layer, tm, final):
    b, seq, d = x.shape
    row = lambda i, j: (i, j, 0)
    return pl.pallas_call(
        functools.partial(_outproj_kernel, final=final),
        grid=(b, seq // tm),
        in_specs=[
            pl.BlockSpec((1, tm, d), row),
            pl.BlockSpec((1, tm, D_FOURIER), row),
            pl.BlockSpec((1, tm, D_FOURIER), row),
            pl.BlockSpec((1, tm, D_GLA), row),
            pl.BlockSpec((1, tm, D_GLA), row),
            pl.BlockSpec((1, D_FOURIER + D_GLA, d), lambda i, j: (layer, 0, 0),
                         pipeline_mode=pl.Buffered(1)),
            pl.BlockSpec((1, 1, 3, d), lambda i, j: (layer, i, 0, 0)),
            pl.BlockSpec((1, d), lambda i, j: (0, 0)),
        ],
        out_specs=pl.BlockSpec((1, tm, d), row),
        out_shape=jax.ShapeDtypeStruct((b, seq, d), _f32),
        compiler_params=_params(),
        name="outproj",
    )(x, yf, z, yg, r, w_out, mod, final_g)


def _block_tri():
    idx = np.arange(GLA_BLOCK)
    return jnp.asarray((idx[None, :] <= idx[:, None]).astype(np.float32), _bf16)


def kernel(x, c, norm_g, w_ada, b_ada, w_in, w_fmap, w_af, b_af, w_ab, b_ab, gla_norm_g, w_out, final_g):
    depth = w_in.shape[0]
    b, seq, d = x.shape
    assert d == D_MODEL and seq % (2 * GLA_BLOCK) == 0 and seq % (FFT_RADIX * FFT_ROWS) == 0

    mod = _ada_mod(c, w_ada, b_ada).reshape(depth, b, 3, d)
    ab = _fold_fourier_weights(w_fmap, seq)
    w_in_t = jnp.swapaxes(w_in, 1, 2)
    w_all, w_gate = _prep_weights(w_in_t)

    zeros = jnp.zeros_like(w_af)
    wc = jnp.concatenate([jnp.concatenate([w_af, zeros], axis=1),
                          jnp.concatenate([zeros, w_ab], axis=1)], axis=-1)
    bc = jnp.concatenate([b_af, b_ab], axis=-1)

    def per_head(a):
        lead = a.shape[:-1]
        a = a.reshape(lead + (2, N_GLA_HEADS, GLA_HEAD_K))
        return jnp.swapaxes(a, -3, -2).reshape(lead + (2 * D_GLA_KEY,))

    wc = per_head(wc)
    bc = per_head(bc).reshape(depth, 1, 2 * D_GLA_KEY)

    lm = _fft_matrices(seq)
    tri = _block_tri()
    norm_g3 = norm_g.reshape(depth, 1, d)
    gla_norm_g3 = gla_norm_g.reshape(depth, 1, D_GLA)
    final_g2 = final_g.reshape(1, d)

    for l in range(depth):
        p, q, z, qk, v, r, la = _inproj(x, mod, norm_g3, w_all, ab, w_gate, wc, bc, layer=l, tm=512)
        y_f = _seqfft(lm, p, q)
        y_g = _gla(tri, qk, v, la, gla_norm_g3, layer=l)
        x = _outproj(x, y_f, z, y_g, r, w_out, mod, final_g2, layer=l, tm=512,
                     final=(l == depth - 1))
    return x
```

```python
import functools

import numpy as np
import jax
import jax.numpy as jnp
from jax import lax
from jax.experimental import pallas as pl
from jax.experimental.pallas import tpu as pltpu

D_MODEL = 1024
D_FOURIER = 1024
N_FOURIER_GROUPS = 4
FOURIER_GROUP = 256
N_GLA_HEADS = 4
D_GLA_KEY = 512
D_GLA = 1024
GLA_HEAD_K = 128
GLA_HEAD_V = 256
GATE_RANK = 16
GATE_LOGIT_NORMALIZER = 16.0
CHUNK = 64
EPS = 1e-6
LOG2_E = 1.4426950408889634

_OFF_Z = D_FOURIER
_OFF_QK = 2 * D_FOURIER
_OFF_V = _OFF_QK + 2 * D_GLA_KEY
_OFF_R = _OFF_V + D_GLA
D_MAIN = _OFF_R + D_GLA

GLA_BLOCK = 256
CHUNKS_PER_BLOCK = GLA_BLOCK // CHUNK

FFT_RADIX = 8
FFT_ROWS = 16
FFT_HALVES = 2
LANES = 128
OUTPROJ_K_CHUNK = 256
GATE_COLS = 256

VMEM_LIMIT_BYTES = 56 * 1024 * 1024

_f32 = jnp.float32
_bf16 = jnp.bfloat16


def _dot(a, b):
    return jnp.dot(a, b, preferred_element_type=_f32)


def _split_bf16(a):
    hi = a.astype(_bf16)
    lo = (a - hi.astype(_f32)).astype(_bf16)
    return hi, lo


def _silu(v):
    return v * (1.0 / (1.0 + jnp.exp(-v)))


def _params():
    return pltpu.CompilerParams(dimension_semantics=("arbitrary", "arbitrary"),
                                vmem_limit_bytes=VMEM_LIMIT_BYTES)


def _ada_kernel(c_ref, w_ref, b_ref, o_ref):
    c_act = _silu(c_ref[...]).astype(_bf16)
    o_ref[0] = _dot(c_act, w_ref[0].astype(_bf16)) + b_ref[0]


def _ada_mod(c, w_ada, b_ada):
    depth = w_ada.shape[0]
    b = c.shape[0]
    n_tiles = 3
    return pl.pallas_call(
        _ada_kernel,
        grid=(depth, n_tiles),
        in_specs=[
            pl.BlockSpec((b, D_MODEL), lambda l, j: (0, 0)),
            pl.BlockSpec((1, D_MODEL, D_MODEL), lambda l, j: (l, 0, j)),
            pl.BlockSpec((1, 1, D_MODEL), lambda l, j: (l, 0, j)),
        ],
        out_specs=pl.BlockSpec((1, b, D_MODEL), lambda l, j: (l, 0, j)),
        out_shape=jax.ShapeDtypeStruct((depth, b, 3 * D_MODEL), _f32),
        compiler_params=_params(),
        name="ada_mod",
    )(c, w_ada, b_ada.reshape(depth, 1, 3 * D_MODEL))


def _fold_kernel(wf_ref, cc_ref, sc_ref, ab_ref, *, norm):
    hp = lax.Precision.HIGHEST
    for g in range(N_FOURIER_GROUPS):
        wf = wf_ref[0, g]
        a = jnp.dot(cc_ref[...], wf, precision=hp, preferred_element_type=_f32) * norm
        bm = jnp.dot(sc_ref[...], wf, precision=hp, preferred_element_type=_f32) * norm
        ab_ref[0, g] = jnp.concatenate([a, bm], axis=1).astype(_bf16)


def _fold_fourier_weights(w_fmap, seq_len):
    depth = w_fmap.shape[0]
    idx = np.arange(FOURIER_GROUP)
    ang = 2.0 * np.pi * ((idx[:, None] * idx[None, :]) % FOURIER_GROUP) / FOURIER_GROUP
    cc = jnp.asarray(np.cos(ang), _f32)
    sc = jnp.asarray(np.sin(ang), _f32)
    norm = float(1.0 / np.sqrt(seq_len * FOURIER_GROUP))
    ng = N_FOURIER_GROUPS
    gspec = pl.BlockSpec((FOURIER_GROUP, FOURIER_GROUP), lambda l: (0, 0))
    return pl.pallas_call(
        functools.partial(_fold_kernel, norm=norm),
        grid=(depth,),
        in_specs=[
            pl.BlockSpec((1, ng, FOURIER_GROUP, FOURIER_GROUP), lambda l: (l, 0, 0, 0)),
            gspec, gspec,
        ],
        out_specs=pl.BlockSpec((1, ng, FOURIER_GROUP, 2 * FOURIER_GROUP), lambda l: (l, 0, 0, 0)),
        out_shape=jax.ShapeDtypeStruct((depth, ng, FOURIER_GROUP, 2 * FOURIER_GROUP), _bf16),
        compiler_params=pltpu.CompilerParams(dimension_semantics=("arbitrary",),
                                             vmem_limit_bytes=VMEM_LIMIT_BYTES),
        name="fold_fourier",
    )(w_fmap, cc, sc)


PREP_ROWS = 512


def _prep_main_kernel(wt_ref, o_ref):
    o_ref[0] = wt_ref[0].T.astype(_bf16)


def _prep_gate_kernel(wt_ref, o_ref):
    w = wt_ref[0]
    o_ref[0] = jnp.concatenate([w, w, w, jnp.zeros_like(w)], axis=0).T


def _prep_weights(w_in_t):
    depth, _, d = w_in_t.shape
    rank2 = 2 * GATE_RANK
    w_all = pl.pallas_call(
        _prep_main_kernel,
        grid=(depth, D_MAIN // PREP_ROWS),
        in_specs=[pl.BlockSpec((1, PREP_ROWS, d), lambda l, j: (l, j, 0))],
        out_specs=pl.BlockSpec((1, d, PREP_ROWS), lambda l, j: (l, 0, j)),
        out_shape=jax.ShapeDtypeStruct((depth, d, D_MAIN), _bf16),
        compiler_params=_params(),
        name="prep_main",
    )(w_in_t)
    w_gate = pl.pallas_call(
        _prep_gate_kernel,
        grid=(depth,),
        in_specs=[pl.BlockSpec((1, rank2, d), lambda l: (l, D_MAIN // rank2, 0))],
        out_specs=pl.BlockSpec((1, d, 4 * rank2), lambda l: (l, 0, 0)),
        out_shape=jax.ShapeDtypeStruct((depth, d, 4 * rank2), _f32),
        compiler_params=pltpu.CompilerParams(dimension_semantics=("arbitrary",),
                                             vmem_limit_bytes=VMEM_LIMIT_BYTES),
        name="prep_gate",
    )(w_in_t)
    return w_all, w_gate


def _inproj_kernel(x_ref, mod_ref, g_ref, w_ref, ab_ref, wg_ref, wc_ref, bc_ref,
                   p_ref, q_ref, z_ref, qk_ref, v_ref, r_ref, la_ref, pq_sc):
    x = x_ref[0]
    ms = jnp.mean(x * x, axis=-1, keepdims=True)
    y = x * lax.rsqrt(ms + EPS) * g_ref[0]
    mod = mod_ref[0, 0]
    shift = mod[0:1, :]
    scale = mod[1:2, :]
    h = (y * (1.0 + scale) + shift).astype(_bf16)

    rank2 = 2 * GATE_RANK
    g3 = _dot(h, wg_ref[0].astype(_bf16))
    u = _dot(h, w_ref[0, :, 0:_OFF_Z]).astype(_bf16)
    z_ref[0] = _dot(h, w_ref[0, :, _OFF_Z:_OFF_QK]).astype(_bf16)

    g_hi, g_lo = _split_bf16(g3)
    lane = lax.broadcasted_iota(jnp.int32, g3.shape, 1)
    lhs = jnp.where((lane >= rank2) & (lane < 2 * rank2), g_lo, g_hi)
    w_hi, w_lo = _split_bf16(wc_ref[0])
    rhs = jnp.concatenate([w_hi, w_hi, w_lo, jnp.zeros_like(w_hi)], axis=0)

    def log2_decay(k):
        cols = slice(k * GATE_COLS, (k + 1) * GATE_COLS)
        x_g = _dot(lhs, rhs[:, cols]) + bc_ref[0][:, cols]
        soft = jnp.log2(1.0 + jnp.exp2(jnp.abs(x_g) * -LOG2_E))
        la_ref[0, :, cols] = (jnp.minimum(x_g, 0.0) * (LOG2_E / GATE_LOGIT_NORMALIZER)
                              - soft * (1.0 / GATE_LOGIT_NORMALIZER)).astype(_bf16)

    log2_decay(0)
    qk_ref[0] = _dot(h, w_ref[0, :, _OFF_QK:_OFF_V]).astype(_bf16)
    log2_decay(1)
    n_slab = 2 * FOURIER_GROUP // LANES
    for g in range(N_FOURIER_GROUPS):
        pq = _dot(u[:, g * FOURIER_GROUP:(g + 1) * FOURIER_GROUP], ab_ref[0, g])
        for s in range(n_slab):
            pq_sc[g * n_slab + s] = pq[:, s * LANES:(s + 1) * LANES]
    log2_decay(2)
    v_ref[0] = _dot(h, w_ref[0, :, _OFF_V:_OFF_R]).astype(_bf16)
    log2_decay(3)
    r_ref[0] = _dot(h, w_ref[0, :, _OFF_R:D_MAIN]).astype(_bf16)

    sub_rows = pq_sc.shape[1] // FFT_RADIX
    for g in range(N_FOURIER_GROUPS):
        for s in range(n_slab):
            dst = p_ref if s < n_slab // 2 else q_ref
            c0 = (s % (n_slab // 2)) * LANES
            for n1 in range(FFT_RADIX):
                rows = pq_sc[g * n_slab + s, pl.ds(n1, sub_rows, stride=FFT_RADIX), :]
                lo = n1 * FOURIER_GROUP + c0
                dst[0, g, :, lo:lo + LANES] = rows.astype(_bf16)


def _inproj(x, mod, norm_g3, w_all, ab, w_gate, wc, bc, layer, tm):
    b, seq, d = x.shape
    row = lambda i, j: (i, j, 0)
    lay3 = lambda i, j: (layer, 0, 0)
    lay4 = lambda i, j: (layer, 0, 0, 0)
    ng = N_FOURIER_GROUPS

    def out(width, dtype=_bf16):
        return (pl.BlockSpec((1, tm, width), row),
                jax.ShapeDtypeStruct((b, seq, width), dtype))

    def grouped():
        width = FFT_RADIX * FOURIER_GROUP
        return (pl.BlockSpec((1, ng, tm // FFT_RADIX, width), lambda i, j: (i, 0, j, 0)),
                jax.ShapeDtypeStruct((b, ng, seq // FFT_RADIX, width), _bf16))

    outs = [grouped(), grouped(), out(1024), out(1024), out(1024), out(1024), out(1024)]
    return pl.pallas_call(
        _inproj_kernel,
        grid=(b, seq // tm),
        in_specs=[
            pl.BlockSpec((1, tm, d), row),
            pl.BlockSpec((1, 1, 3, d), lambda i, j: (layer, i, 0, 0)),
            pl.BlockSpec((1, 1, d), lay3),
            pl.BlockSpec((1, d, D_MAIN), lay3, pipeline_mode=pl.Buffered(1)),
            pl.BlockSpec((1, ng, FOURIER_GROUP, 2 * FOURIER_GROUP), lay4),
            pl.BlockSpec((1, d, 8 * GATE_RANK), lay3),
            pl.BlockSpec((1, 2 * GATE_RANK, 2 * D_GLA_KEY), lay3),
            pl.BlockSpec((1, 1, 2 * D_GLA_KEY), lay3),
        ],
        out_specs=[o[0] for o in outs],
        out_shape=[o[1] for o in outs],
        scratch_shapes=[pltpu.VMEM((ng * 2 * FOURIER_GROUP // LANES, tm, LANES), _f32)],
        compiler_params=_params(),
        name="inproj",
    )(x, mod, norm_g3, w_all, ab, w_gate, wc, bc)


def _fft_kernel(lm_ref, p_ref, q_ref, o_ref, b_ref):
    sub = p_ref.shape[2]
    grp = o_ref.shape[2]
    half = sub // FFT_HALVES
    for h in range(FFT_HALVES):
        for n1 in range(FFT_RADIX):
            cols = slice(n1 * grp, (n1 + 1) * grp)
            b_ref[n1, h] = (_dot(lm_ref[n1, h, :, 0:sub], p_ref[0, 0, :, cols])
                            + _dot(lm_ref[n1, h, :, sub:2 * sub], q_ref[0, 0, :, cols]))

    rt = np.float32(np.sqrt(0.5))

    def combine(h, c):
        r0 = c * FFT_ROWS
        for t in range(grp // LANES):
            lanes = slice(t * LANES, (t + 1) * LANES)
            a = [b_ref[n, h, r0:r0 + FFT_ROWS, lanes] for n in range(FFT_RADIX)]
            bi = [b_ref[n, h, half + r0:half + r0 + FFT_ROWS, lanes] for n in range(FFT_RADIX)]
            e0, e1 = a[0] + a[4], a[0] - a[4]
            p1, p2, p3 = a[1] + a[7], a[2] + a[6], a[3] + a[5]
            s13, rd, t0 = p1 + p3, rt * (p1 - p3), e0 + p2
            c0, c4, c2, c1, c3 = t0 + s13, t0 - s13, e0 - p2, e1 + rd, e1 - rd
            m1, m2, m3 = bi[1] - bi[7], bi[2] - bi[6], bi[3] - bi[5]
            rs = rt * (m1 + m3)
            s1, s3, s2 = m2 + rs, rs - m2, m1 - m3
            ys = (c0, c1 - s1, c2 - s2, c3 - s3, c4, c3 + s3, c2 + s2, c1 + s1)
            for k1, yk in enumerate(ys):
                row = k1 * sub + h * half + r0
                o_ref[0, row:row + FFT_ROWS, lanes] = yk.astype(_bf16)

    for h in range(FFT_HALVES):
        for c in range(half // FFT_ROWS):
            combine(h, c)


def _fft_matrices(seq):
    sub = seq // FFT_RADIX
    half = sub // FFT_HALVES
    n2 = np.arange(sub)[None, :]
    mats = []
    for n1 in range(FFT_RADIX):
        halves = []
        for h in range(FFT_HALVES):
            k2 = np.arange(h * half, (h + 1) * half)[:, None]
            ang = 2.0 * np.pi * ((k2 * (n1 + FFT_RADIX * n2)) % seq) / seq
            c, s = np.cos(ang), np.sin(ang)
            halves.append(np.block([[c, -s], [s, c]]))
        mats.append(np.stack(halves))
    return jnp.asarray(np.stack(mats), _f32).astype(_bf16)


def _seqfft(lm, p, q):
    b, ng, sub, width = p.shape
    grp = width // FFT_RADIX
    seq = sub * FFT_RADIX
    dec = pl.BlockSpec((1, 1, sub, FFT_RADIX * grp), lambda i, g: (i, g, 0, 0))
    return pl.pallas_call(
        _fft_kernel,
        grid=(b, ng),
        in_specs=[
            pl.BlockSpec((FFT_RADIX, FFT_HALVES, 2 * sub // FFT_HALVES, 2 * sub),
                         lambda i, g: (0, 0, 0, 0)),
            dec, dec,
        ],
        out_specs=pl.BlockSpec((1, seq, grp), lambda i, g: (i, 0, g)),
        out_shape=jax.ShapeDtypeStruct((b, seq, ng * grp), _bf16),
        scratch_shapes=[pltpu.VMEM((FFT_RADIX, FFT_HALVES, 2 * sub // FFT_HALVES, grp), _f32)],
        compiler_params=_params(),
        name="seqfft",
    )(lm, p, q)


def _gla_kernel(tri_ref, q_ref, k_ref, v_ref, la_ref, g_ref, o_ref,
                oacc_ref, u_ref, qd_ref, dec_ref, *, seq):
    n_blocks = seq // GLA_BLOCK
    nb = CHUNKS_PER_BLOCK
    dk = GLA_HEAD_K
    scale = dk ** -0.5

    def chunked(a):
        return a.reshape(nb, CHUNK, a.shape[-1])

    def flat(a):
        return a.reshape(a.shape[0] * a.shape[1], a.shape[2])

    lane = lax.broadcasted_iota(jnp.int32, (CHUNK, 2 * CHUNK), 1)
    row_l = lax.broadcasted_iota(jnp.int32, (CHUNK, 2 * CHUNK), 0)
    keep = (((lane >= CHUNK) | (lane <= row_l), (lane < CHUNK) | (lane - CHUNK <= row_l)),
            ((lane >= CHUNK) | (lane > row_l), (lane < CHUNK) | (lane - CHUNK > row_l)))

    def prefix(blk):
        rows = pl.ds(blk * GLA_BLOCK, GLA_BLOCK)
        la = la_ref[0, rows, :]
        return blk, rows, la[:, dk:].astype(_f32), _dot(tri_ref[...], la)

    def operands(pre_stage):
        blk, rows, la_b, pre = pre_stage
        pre_b = pre[:, dk:]
        g_f = chunked(pre[:, :dk])
        g_b = chunked(pre_b[GLA_BLOCK - 1:GLA_BLOCK, :] - pre_b + la_b)
        zero = jnp.zeros((1, 1, dk), _f32)
        edge_f = g_f[:, CHUNK - 1:CHUNK, :]
        edge_b = g_b[:, 0:1, :]
        dirs = ((g_f, edge_f, jnp.concatenate([zero, edge_f[:nb - 1]], axis=0),
                 CHUNK // 2, list(range(nb))),
                (g_b, edge_b, jnp.concatenate([edge_b[1:], zero], axis=0),
                 CHUNK // 2 - 1, list(range(nb - 1, -1, -1))))

        qs = chunked(q_ref[0, rows, :].astype(_f32) * scale)
        kk = chunked(k_ref[0, rows, :].astype(_f32))
        zeros_chunk = jnp.zeros((CHUNK, dk), _f32)
        per_dir = []
        for d, (g, edge, base, ref_i, order) in enumerate(dirs):
            tot = edge[order[-1]:order[-1] + 1]
            cum = g - base
            ref = cum[:, ref_i:ref_i + 1, :]
            a = cum - ref
            qi = qs * jnp.exp2(a)
            ki = kk * jnp.exp2(-a)
            last_ref = (edge - base) - ref
            qd_ref[d, rows, :] = flat(qi * jnp.exp2(ref + base)).astype(_bf16)
            kd_t = flat(ki * jnp.exp2(last_ref + (tot - edge))).T.astype(_bf16)
            dec_ref[d, blk] = jnp.broadcast_to(jnp.exp2(tot).reshape(1, dk), (dk, dk)).T
            rhs_t = []
            for i in range(nb):
                pieces = []
                for c in range(nb):
                    if c == i:
                        pieces.append(ki[c])
                    elif order.index(c) < order.index(i):
                        pieces.append(ki[c] * jnp.exp2(last_ref[c] + ref[i] + base[i] - edge[c]))
                    else:
                        pieces.append(zeros_chunk)
                rhs_t.append(jnp.concatenate(pieces, axis=0).T.astype(_bf16))
            per_dir.append((flat(qi).astype(_bf16), rhs_t, kd_t))
        return blk, rows, per_dir

    def scores(ops):
        _, _, per_dir = ops
        return [[_dot(qi[i * CHUNK:(i + 1) * CHUNK], rhs_t[i]) for i in range(nb)]
                for qi, rhs_t, _ in per_dir]

    def outputs(ops, mats):
        blk, rows, per_dir = ops
        score_parts = []
        for d, s_rows in enumerate(mats):
            for i, s_i in enumerate(s_rows):
                tile, half = divmod(i, 2)
                lo, hi = tile * 2 * CHUNK, (tile + 1) * 2 * CHUNK
                masked = jnp.where(keep[d][half], s_i[:, lo:hi], 0.0)
                s_i = jnp.concatenate([masked, s_i[:, hi:]] if tile == 0 else [s_i[:, :lo], masked], axis=1)
                score_parts.append(s_i.astype(_bf16))
        lhs = jnp.concatenate(score_parts + [p[2] for p in per_dir], axis=0)
        res = _dot(lhs, v_ref[0, rows, :])
        oacc_ref[rows, :] = res[0:GLA_BLOCK] + res[GLA_BLOCK:2 * GLA_BLOCK]
        u_ref[0, blk] = res[2 * GLA_BLOCK:2 * GLA_BLOCK + dk]
        u_ref[1, blk] = res[2 * GLA_BLOCK + dk:2 * GLA_BLOCK + 2 * dk]

    n_pairs = n_blocks // 2
    pre_q, ops_q = {}, {}
    for t in range(n_pairs + 2):
        if t < n_pairs:
            pre_q[t] = [prefix(2 * t), prefix(2 * t + 1)]
        if 0 <= t - 1 < n_pairs:
            ops_q[t - 1] = [operands(p) for p in pre_q.pop(t - 1)]
        if 0 <= t - 2 < n_pairs:
            pair = ops_q.pop(t - 2)
            mats = [scores(o) for o in pair]
            for o, m in zip(pair, mats):
                outputs(o, m)

    def finish(blk):
        rows = pl.ds(blk * GLA_BLOCK, GLA_BLOCK)
        o = oacc_ref[rows, :]
        o = o * lax.rsqrt(jnp.mean(o * o, axis=-1, keepdims=True) + EPS) * g_ref[0]
        o_ref[0, rows, :] = o.astype(_bf16)

    states = [None, None]
    for i in range(n_blocks):
        for d in range(2):
            blk = i if d == 0 else n_blocks - 1 - i
            rows = pl.ds(blk * GLA_BLOCK, GLA_BLOCK)
            s = states[d]
            if s is None:
                states[d] = u_ref[d, blk]
                continue
            oacc_ref[rows, :] += _dot(qd_ref[d, rows, :], s.astype(_bf16))
            if i + 1 < n_blocks:
                dec = dec_ref[d, blk]
                states[d] = jnp.concatenate([dec, dec], axis=1) * s + u_ref[d, blk]
        if 2 * i + 1 >= n_blocks:
            for blk in sorted({i, n_blocks - 1 - i}):
                finish(blk)


def _gla(tri, qk, v, la, gla_norm_g3, layer):
    b, seq, _ = v.shape
    h = N_GLA_HEADS
    n_blocks = seq // GLA_BLOCK
    return pl.pallas_call(
        functools.partial(_gla_kernel, seq=seq),
        grid=(b, h),
        in_specs=[
            pl.BlockSpec((GLA_BLOCK, GLA_BLOCK), lambda i, j: (0, 0)),
            pl.BlockSpec((1, seq, GLA_HEAD_K), lambda i, j: (i, 0, j)),
            pl.BlockSpec((1, seq, GLA_HEAD_K), lambda i, j: (i, 0, h + j)),
            pl.BlockSpec((1, seq, GLA_HEAD_V), lambda i, j: (i, 0, j)),
            pl.BlockSpec((1, seq, 2 * GLA_HEAD_K), lambda i, j: (i, 0, j)),
            pl.BlockSpec((1, 1, GLA_HEAD_V), lambda i, j: (layer, 0, j)),
        ],
        out_specs=pl.BlockSpec((1, seq, GLA_HEAD_V), lambda i, j: (i, 0, j)),
        out_shape=jax.ShapeDtypeStruct((b, seq, D_GLA), _bf16),
        scratch_shapes=[
            pltpu.VMEM((seq, GLA_HEAD_V), _f32),
            pltpu.VMEM((2, n_blocks, GLA_HEAD_K, GLA_HEAD_V), _f32),
            pltpu.VMEM((2, seq, GLA_HEAD_K), _bf16),
            pltpu.VMEM((2, n_blocks, GLA_HEAD_K, GLA_HEAD_K), _f32),
        ],
        compiler_params=_params(),
        name="gla",
    )(tri, qk, qk, v, la, gla_norm_g3)


def _outproj_kernel(x_ref, yf_ref, z_ref, yg_ref, r_ref, w_ref, mod_ref, fg_ref, o_ref, *, final):
    y = None
    for y_ref, gate_ref, w0 in ((yf_ref, z_ref, 0), (yg_ref, r_ref, D_FOURIER)):
        for c0 in range(0, y_ref.shape[2], OUTPROJ_K_CHUNK):
            cols = slice(c0, c0 + OUTPROJ_K_CHUNK)
            a = (y_ref[0, :, cols].astype(_f32) * _silu(gate_ref[0, :, cols].astype(_f32))).astype(_bf16)
            part = _dot(a, w_ref[0, w0 + c0:w0 + c0 + OUTPROJ_K_CHUNK, :].astype(_bf16))
            y = part if y is None else y + part
    gate = mod_ref[0, 0][2:3, :]
    xn = x_ref[0] + gate * y
    if final:
        ms = jnp.mean(xn * xn, axis=-1, keepdims=True)
        xn = xn * lax.rsqrt(ms + EPS) * fg_ref[...]
    o_ref[0] = xn


def _outproj(x, yf, z, yg, r, w_out, mod, final_g, layer, tm, final):
    b, seq, d = x.shape
    row = lambda i, j: (i, j, 0)
    return pl.pallas_call(
        functools.partial(_outproj_kernel, final=final),
        grid=(b, seq // tm),
        in_specs=[
            pl.BlockSpec((1, tm, d), row),
            pl.BlockSpec((1, tm, D_FOURIER), row),
            pl.BlockSpec((1, tm, D_FOURIER), row),
            pl.BlockSpec((1, tm, D_GLA), row),
            pl.BlockSpec((1, tm, D_GLA), row),
            pl.BlockSpec((1, D_FOURIER + D_GLA, d), lambda i, j: (layer, 0, 0),
                         pipeline_mode=pl.Buffered(1)),
            pl.BlockSpec((1, 1, 3, d), lambda i, j: (layer, i, 0, 0)),
            pl.BlockSpec((1, d), lambda i, j: (0, 0)),
        ],
        out_specs=pl.BlockSpec((1, tm, d), row),
        out_shape=jax.ShapeDtypeStruct((b, seq, d), _f32),
        compiler_params=_params(),
        name="outproj",
    )(x, yf, z, yg, r, w_out, mod, final_g)


def _block_tri():
    idx = np.arange(GLA_BLOCK)
    return jnp.asarray((idx[None, :] <= idx[:, None]).astype(np.float32), _bf16)


def kernel(x, c, norm_g, w_ada, b_ada, w_in, w_fmap, w_af, b_af, w_ab, b_ab, gla_norm_g, w_out, final_g):
    depth = w_in.shape[0]
    b, seq, d = x.shape
    assert d == D_MODEL and seq % (2 * GLA_BLOCK) == 0 and seq % (FFT_RADIX * FFT_ROWS) == 0

    mod = _ada_mod(c, w_ada, b_ada).reshape(depth, b, 3, d)
    ab = _fold_fourier_weights(w_fmap, seq)
    w_in_t = jnp.swapaxes(w_in, 1, 2)
    w_all, w_gate = _prep_weights(w_in_t)

    zeros = jnp.zeros_like(w_af)
    wc = jnp.concatenate([jnp.concatenate([w_af, zeros], axis=1),
                          jnp.concatenate([zeros, w_ab], axis=1)], axis=-1)
    bc = jnp.concatenate([b_af, b_ab], axis=-1)

    def per_head(a):
        lead = a.shape[:-1]
        a = a.reshape(lead + (2, N_GLA_HEADS, GLA_HEAD_K))
        return jnp.swapaxes(a, -3, -2).reshape(lead + (2 * D_GLA_KEY,))

    wc = per_head(wc)
    bc = per_head(bc).reshape(depth, 1, 2 * D_GLA_KEY)

    lm = _fft_matrices(seq)
    tri = _block_tri()
    norm_g3 = norm_g.reshape(depth, 1, d)
    gla_norm_g3 = gla_norm_g.reshape(depth, 1, D_GLA)
    final_g2 = final_g.reshape(1, d)

    for l in range(depth):
        p, q, z, qk, v, r, la = _inproj(x, mod, norm_g3, w_all, ab, w_gate, wc, bc, layer=l, tm=512)
        y_f = _seqfft(lm, p, q)
        y_g = _gla(tri, qk, v, la, gla_norm_g3, layer=l)
        x = _outproj(x, y_f, z, y_g, r, w_out, mod, final_g2, layer=l, tm=512,
                     final=(l == depth - 1))
    return x
```

```python
import functools

import numpy as np
import jax
import jax.numpy as jnp
from jax import lax
from jax.experimental import pallas as pl
from jax.experimental.pallas import tpu as pltpu

D_MODEL = 1024
D_FOURIER = 1024
N_FOURIER_GROUPS = 4
FOURIER_GROUP = 256
N_GLA_HEADS = 4
D_GLA_KEY = 512
D_GLA = 1024
GLA_HEAD_K = 128
GLA_HEAD_V = 256
GATE_RANK = 16
GATE_LOGIT_NORMALIZER = 16.0
CHUNK = 64
EPS = 1e-6
LOG2_E = 1.4426950408889634

_OFF_Z = D_FOURIER
_OFF_QK = 2 * D_FOURIER
_OFF_V = _OFF_QK + 2 * D_GLA_KEY
_OFF_R = _OFF_V + D_GLA
D_MAIN = _OFF_R + D_GLA

GLA_BLOCK = 256
CHUNKS_PER_BLOCK = GLA_BLOCK // CHUNK

FFT_RADIX = 8
FFT_ROWS = 16
FFT_HALVES = 2
LANES = 128
OUTPROJ_K_CHUNK = 256
GATE_COLS = 256

VMEM_LIMIT_BYTES = 56 * 1024 * 1024

_f32 = jnp.float32
_bf16 = jnp.bfloat16


def _dot(a, b):
    return jnp.dot(a, b, preferred_element_type=_f32)


def _split_bf16(a):
    hi = a.astype(_bf16)
    lo = (a - hi.astype(_f32)).astype(_bf16)
    return hi, lo


def _silu(v):
    return v * (1.0 / (1.0 + jnp.exp(-v)))


def _params():
    return pltpu.CompilerParams(dimension_semantics=("arbitrary", "arbitrary"),
                                vmem_limit_bytes=VMEM_LIMIT_BYTES)


def _ada_kernel(c_ref, w_ref, b_ref, o_ref):
    c_act = _silu(c_ref[...]).astype(_bf16)
    o_ref[0] = _dot(c_act, w_ref[0].astype(_bf16)) + b_ref[0]


def _ada_mod(c, w_ada, b_ada):
    depth = w_ada.shape[0]
    b = c.shape[0]
    n_tiles = 3
    return pl.pallas_call(
        _ada_kernel,
        grid=(depth, n_tiles),
        in_specs=[
            pl.BlockSpec((b, D_MODEL), lambda l, j: (0, 0)),
            pl.BlockSpec((1, D_MODEL, D_MODEL), lambda l, j: (l, 0, j)),
            pl.BlockSpec((1, 1, D_MODEL), lambda l, j: (l, 0, j)),
        ],
        out_specs=pl.BlockSpec((1, b, D_MODEL), lambda l, j: (l, 0, j)),
        out_shape=jax.ShapeDtypeStruct((depth, b, 3 * D_MODEL), _f32),
        compiler_params=_params(),
        name="ada_mod",
    )(c, w_ada, b_ada.reshape(depth, 1, 3 * D_MODEL))


def _fold_kernel(wf_ref, cc_ref, sc_ref, ab_ref, *, norm):
    hp = lax.Precision.HIGHEST
    for g in range(N_FOURIER_GROUPS):
        wf = wf_ref[0, g]
        a = jnp.dot(cc_ref[...], wf, precision=hp, preferred_element_type=_f32) * norm
        bm = jnp.dot(sc_ref[...], wf, precision=hp, preferred_element_type=_f32) * norm
        ab_ref[0, g] = jnp.concatenate([a, bm], axis=1).astype(_bf16)


def _fold_fourier_weights(w_fmap, seq_len):
    depth = w_fmap.shape[0]
    idx = np.arange(FOURIER_GROUP)
    ang = 2.0 * np.pi * ((idx[:, None] * idx[None, :]) % FOURIER_GROUP) / FOURIER_GROUP
    cc = jnp.asarray(np.cos(ang), _f32)
    sc = jnp.asarray(np.sin(ang), _f32)
    norm = float(1.0 / np.sqrt(seq_len * FOURIER_GROUP))
    ng = N_FOURIER_GROUPS
    gspec = pl.BlockSpec((FOURIER_GROUP, FOURIER_GROUP), lambda l: (0, 0))
    return pl.pallas_call(
        functools.partial(_fold_kernel, norm=norm),
        grid=(depth,),
        in_specs=[
            pl.BlockSpec((1, ng, FOURIER_GROUP, FOURIER_GROUP), lambda l: (l, 0, 0, 0)),
            gspec, gspec,
        ],
        out_specs=pl.BlockSpec((1, ng, FOURIER_GROUP, 2 * FOURIER_GROUP), lambda l: (l, 0, 0, 0)),
        out_shape=jax.ShapeDtypeStruct((depth, ng, FOURIER_GROUP, 2 * FOURIER_GROUP), _bf16),
        compiler_params=pltpu.CompilerParams(dimension_semantics=("arbitrary",),
                                             vmem_limit_bytes=VMEM_LIMIT_BYTES),
        name="fold_fourier",
    )(w_fmap, cc, sc)


PREP_ROWS = 512


def _prep_main_kernel(wt_ref, o_ref):
    o_ref[0] = wt_ref[0].T.astype(_bf16)


def _prep_gate_kernel(wt_ref, o_ref):
    w = wt_ref[0]
    o_ref[0] = jnp.concatenate([w, w, w, jnp.zeros_like(w)], axis=0).T


def _prep_weights(w_in_t):
    depth, _, d = w_in_t.shape
    rank2 = 2 * GATE_RANK
    w_all = pl.pallas_call(
        _prep_main_kernel,
        grid=(depth, D_MAIN // PREP_ROWS),
        in_specs=[pl.BlockSpec((1, PREP_ROWS, d), lambda l, j: (l, j, 0))],
        out_specs=pl.BlockSpec((1, d, PREP_ROWS), lambda l, j: (l, 0, j)),
        out_shape=jax.ShapeDtypeStruct((depth, d, D_MAIN), _bf16),
        compiler_params=_params(),
        name="prep_main",
    )(w_in_t)
    w_gate = pl.pallas_call(
        _prep_gate_kernel,
        grid=(depth,),
        in_specs=[pl.BlockSpec((1, rank2, d), lambda l: (l, D_MAIN // rank2, 0))],
        out_specs=pl.BlockSpec((1, d, 4 * rank2), lambda l: (l, 0, 0)),
        out_shape=jax.ShapeDtypeStruct((depth, d, 4 * rank2), _f32),
        compiler_params=pltpu.CompilerParams(dimension_semantics=("arbitrary",),
                                             vmem_limit_bytes=VMEM_LIMIT_BYTES),
        name="prep_gate",
    )(w_in_t)
    return w_all, w_gate


def _inproj_kernel(x_ref, mod_ref, g_ref, w_ref, ab_ref, wg_ref, wc_ref, bc_ref,
                   p_ref, q_ref, z_ref, qk_ref, v_ref, r_ref, la_ref, pq_sc):
    x = x_ref[0]
    ms = jnp.mean(x * x, axis=-1, keepdims=True)
    y = x * lax.rsqrt(ms + EPS) * g_ref[0]
    mod = mod_ref[0, 0]
    shift = mod[0:1, :]
    scale = mod[1:2, :]
    h = (y * (1.0 + scale) + shift).astype(_bf16)

    rank2 = 2 * GATE_RANK
    g3 = _dot(h, wg_ref[0].astype(_bf16))
    u = _dot(h, w_ref[0, :, 0:_OFF_Z]).astype(_bf16)
    z_ref[0] = _dot(h, w_ref[0, :, _OFF_Z:_OFF_QK]).astype(_bf16)

    g_hi, g_lo = _split_bf16(g3)
    lane = lax.broadcasted_iota(jnp.int32, g3.shape, 1)
    lhs = jnp.where((lane >= rank2) & (lane < 2 * rank2), g_lo, g_hi)
    w_hi, w_lo = _split_bf16(wc_ref[0])
    rhs = jnp.concatenate([w_hi, w_hi, w_lo, jnp.zeros_like(w_hi)], axis=0)

    def log2_decay(k):
        cols = slice(k * GATE_COLS, (k + 1) * GATE_COLS)
        x_g = _dot(lhs, rhs[:, cols]) + bc_ref[0][:, cols]
        soft = jnp.log2(1.0 + jnp.exp2(jnp.abs(x_g) * -LOG2_E))
        la_ref[0, :, cols] = (jnp.minimum(x_g, 0.0) * (LOG2_E / GATE_LOGIT_NORMALIZER)
                              - soft * (1.0 / GATE_LOGIT_NORMALIZER)).astype(_bf16)

    log2_decay(0)
    qk_ref[0] = _dot(h, w_ref[0, :, _OFF_QK:_OFF_V]).astype(_bf16)
    log2_decay(1)
    n_slab = 2 * FOURIER_GROUP // LANES
    sub_rows = pq_sc.shape[1] // FFT_RADIX
    for g in range(N_FOURIER_GROUPS):
        pq = _dot(u[:, g * FOURIER_GROUP:(g + 1) * FOURIER_GROUP], ab_ref[0, g])
        for s in range(n_slab):
            pq_sc[s] = pq[:, s * LANES:(s + 1) * LANES]
        for s in range(n_slab):
            dst = p_ref if s < n_slab // 2 else q_ref
            c0 = (s % (n_slab // 2)) * LANES
            for n1 in range(FFT_RADIX):
                rows = pq_sc[s, pl.ds(n1, sub_rows, stride=FFT_RADIX), :]
                lo = n1 * FOURIER_GROUP + c0
                dst[0, g, :, lo:lo + LANES] = rows.astype(_bf16)
    log2_decay(2)
    v_ref[0] = _dot(h, w_ref[0, :, _OFF_V:_OFF_R]).astype(_bf16)
    log2_decay(3)
    r_ref[0] = _dot(h, w_ref[0, :, _OFF_R:D_MAIN]).astype(_bf16)


def _inproj(x, mod, norm_g3, w_all, ab, w_gate, wc, bc, layer, tm):
    b, seq, d = x.shape
    row = lambda i, j: (i, j, 0)
    lay3 = lambda i, j: (layer, 0, 0)
    lay4 = lambda i, j: (layer, 0, 0, 0)
    ng = N_FOURIER_GROUPS

    def out(width, dtype=_bf16):
        return (pl.BlockSpec((1, tm, width), row),
                jax.ShapeDtypeStruct((b, seq, width), dtype))

    def grouped():
        width = FFT_RADIX * FOURIER_GROUP
        return (pl.BlockSpec((1, ng, tm // FFT_RADIX, width), lambda i, j: (i, 0, j, 0)),
                jax.ShapeDtypeStruct((b, ng, seq // FFT_RADIX, width), _bf16))

    outs = [grouped(), grouped(), out(1024), out(1024), out(1024), out(1024), out(1024)]
    return pl.pallas_call(
        _inproj_kernel,
        grid=(b, seq // tm),
        in_specs=[
            pl.BlockSpec((1, tm, d), row),
            pl.BlockSpec((1, 1, 3, d), lambda i, j: (layer, i, 0, 0)),
            pl.BlockSpec((1, 1, d), lay3),
            pl.BlockSpec((1, d, D_MAIN), lay3, pipeline_mode=pl.Buffered(1)),
            pl.BlockSpec((1, ng, FOURIER_GROUP, 2 * FOURIER_GROUP), lay4),
            pl.BlockSpec((1, d, 8 * GATE_RANK), lay3),
            pl.BlockSpec((1, 2 * GATE_RANK, 2 * D_GLA_KEY), lay3),
            pl.BlockSpec((1, 1, 2 * D_GLA_KEY), lay3),
        ],
        out_specs=[o[0] for o in outs],
        out_shape=[o[1] for o in outs],
        scratch_shapes=[pltpu.VMEM((2 * FOURIER_GROUP // LANES, tm, LANES), _f32)],
        compiler_params=_params(),
        name="inproj",
    )(x, mod, norm_g3, w_all, ab, w_gate, wc, bc)


def _fft_kernel(lm_ref, p_ref, q_ref, o_ref, b_ref):
    ng = p_ref.shape[1]
    sub = p_ref.shape[2]
    grp = o_ref.shape[2] // ng
    half = sub // FFT_HALVES
    for g in range(ng):
        for h in range(FFT_HALVES):
            for n1 in range(FFT_RADIX):
                cols = slice(n1 * grp, (n1 + 1) * grp)
                b_ref[g, n1, h] = (_dot(lm_ref[n1, h, :, 0:sub], p_ref[0, g, :, cols])
                                   + _dot(lm_ref[n1, h, :, sub:2 * sub], q_ref[0, g, :, cols]))

    rt = np.float32(np.sqrt(0.5))

    def combine(g, h, c):
        r0 = c * FFT_ROWS
        for t in range(grp // LANES):
            lanes = slice(t * LANES, (t + 1) * LANES)
            out_lanes = slice(g * grp + t * LANES, g * grp + (t + 1) * LANES)
            a = [b_ref[g, n, h, r0:r0 + FFT_ROWS, lanes] for n in range(FFT_RADIX)]
            bi = [b_ref[g, n, h, half + r0:half + r0 + FFT_ROWS, lanes] for n in range(FFT_RADIX)]
            e0, e1 = a[0] + a[4], a[0] - a[4]
            p1, p2, p3 = a[1] + a[7], a[2] + a[6], a[3] + a[5]
            s13, rd, t0 = p1 + p3, rt * (p1 - p3), e0 + p2
            c0, c4, c2, c1, c3 = t0 + s13, t0 - s13, e0 - p2, e1 + rd, e1 - rd
            m1, m2, m3 = bi[1] - bi[7], bi[2] - bi[6], bi[3] - bi[5]
            rs = rt * (m1 + m3)
            s1, s3, s2 = m2 + rs, rs - m2, m1 - m3
            ys = (c0, c1 - s1, c2 - s2, c3 - s3, c4, c3 + s3, c2 + s2, c1 + s1)
            for k1, yk in enumerate(ys):
                row = k1 * sub + h * half + r0
                o_ref[0, row:row + FFT_ROWS, out_lanes] = yk.astype(_bf16)

    for g in range(ng):
        for h in range(FFT_HALVES):
            for c in range(half // FFT_ROWS):
                combine(g, h, c)


def _fft_matrices(seq):
    sub = seq // FFT_RADIX
    half = sub // FFT_HALVES
    n2 = np.arange(sub)[None, :]
    mats = []
    for n1 in range(FFT_RADIX):
        halves = []
        for h in range(FFT_HALVES):
            k2 = np.arange(h * half, (h + 1) * half)[:, None]
            ang = 2.0 * np.pi * ((k2 * (n1 + FFT_RADIX * n2)) % seq) / seq
            c, s = np.cos(ang), np.sin(ang)
            halves.append(np.block([[c, -s], [s, c]]))
        mats.append(np.stack(halves))
    return jnp.asarray(np.stack(mats), _f32).astype(_bf16)


def _seqfft(lm, p, q):
    b, ng, sub, width = p.shape
    grp = width // FFT_RADIX
    seq = sub * FFT_RADIX
    dec = pl.BlockSpec((1, ng, sub, FFT_RADIX * grp), lambda i: (i, 0, 0, 0))
    return pl.pallas_call(
        _fft_kernel,
        grid=(b,),
        in_specs=[
            pl.BlockSpec((FFT_RADIX, FFT_HALVES, 2 * sub // FFT_HALVES, 2 * sub),
                         lambda i: (0, 0, 0, 0), pipeline_mode=pl.Buffered(1)),
            dec, dec,
        ],
        out_specs=pl.BlockSpec((1, seq, ng * grp), lambda i: (i, 0, 0)),
        out_shape=jax.ShapeDtypeStruct((b, seq, ng * grp), _bf16),
        scratch_shapes=[pltpu.VMEM((ng, FFT_RADIX, FFT_HALVES, 2 * sub // FFT_HALVES, grp), _f32)],
        compiler_params=pltpu.CompilerParams(dimension_semantics=("arbitrary",),
                                             vmem_limit_bytes=VMEM_LIMIT_BYTES),
        name="seqfft",
    )(lm, p, q)


def _gla_kernel(tri_ref, q_ref, k_ref, v_ref, la_ref, g_ref, o_ref,
                oacc_ref, u_ref, qd_ref, dec_ref, *, seq):
    n_blocks = seq // GLA_BLOCK
    nb = CHUNKS_PER_BLOCK
    dk = GLA_HEAD_K
    scale = dk ** -0.5

    def chunked(a):
        return a.reshape(nb, CHUNK, a.shape[-1])

    def flat(a):
        return a.reshape(a.shape[0] * a.shape[1], a.shape[2])

    lane = lax.broadcasted_iota(jnp.int32, (CHUNK, 2 * CHUNK), 1)
    row_l = lax.broadcasted_iota(jnp.int32, (CHUNK, 2 * CHUNK), 0)
    keep = (((lane >= CHUNK) | (lane <= row_l), (lane < CHUNK) | (lane - CHUNK <= row_l)),
            ((lane >= CHUNK) | (lane > row_l), (lane < CHUNK) | (lane - CHUNK > row_l)))

    def prefix(blk):
        rows = pl.ds(blk * GLA_BLOCK, GLA_BLOCK)
        la = la_ref[0, rows, :]
        return blk, rows, la[:, dk:].astype(_f32), _dot(tri_ref[...], la)

    def operands(pre_stage):
        blk, rows, la_b, pre = pre_stage
        pre_b = pre[:, dk:]
        g_f = chunked(pre[:, :dk])
        g_b = chunked(pre_b[GLA_BLOCK - 1:GLA_BLOCK, :] - pre_b + la_b)
        zero = jnp.zeros((1, 1, dk), _f32)
        edge_f = g_f[:, CHUNK - 1:CHUNK, :]
        edge_b = g_b[:, 0:1, :]
        dirs = ((g_f, edge_f, jnp.concatenate([zero, edge_f[:nb - 1]], axis=0),
                 CHUNK // 2, list(range(nb))),
                (g_b, edge_b, jnp.concatenate([edge_b[1:], zero], axis=0),
                 CHUNK // 2 - 1, list(range(nb - 1, -1, -1))))

        qs = chunked(q_ref[0, rows, :].astype(_f32) * scale)
        kk = chunked(k_ref[0, rows, :].astype(_f32))
        zeros_chunk = jnp.zeros((CHUNK, dk), _f32)
        per_dir = []
        for d, (g, edge, base, ref_i, order) in enumerate(dirs):
            tot = edge[order[-1]:order[-1] + 1]
            cum = g - base
            ref = cum[:, ref_i:ref_i + 1, :]
            a = cum - ref
            qi = qs * jnp.exp2(a)
            ki = kk * jnp.exp2(-a)
            last_ref = (edge - base) - ref
            qd_ref[d, rows, :] = flat(qi * jnp.exp2(ref + base)).astype(_bf16)
            kd_t = flat(ki * jnp.exp2(last_ref + (tot - edge))).T.astype(_bf16)
            dec_ref[d, blk] = jnp.broadcast_to(jnp.exp2(tot).reshape(1, dk), (dk, dk)).T
            rhs_t = []
            for i in range(nb):
                pieces = []
                for c in range(nb):
                    if c == i:
                        pieces.append(ki[c])
                    elif order.index(c) < order.index(i):
                        pieces.append(ki[c] * jnp.exp2(last_ref[c] + ref[i] + base[i] - edge[c]))
                    else:
                        pieces.append(zeros_chunk)
                rhs_t.append(jnp.concatenate(pieces, axis=0).T.astype(_bf16))
            per_dir.append((flat(qi).astype(_bf16), rhs_t, kd_t))
        return blk, rows, per_dir

    def scores(ops):
        _, _, per_dir = ops
        return [[_dot(qi[i * CHUNK:(i + 1) * CHUNK], rhs_t[i]) for i in range(nb)]
                for qi, rhs_t, _ in per_dir]

    def outputs(ops, mats):
        blk, rows, per_dir = ops
        score_parts = []
        for d, s_rows in enumerate(mats):
            for i, s_i in enumerate(s_rows):
                tile, half = divmod(i, 2)
                lo, hi = tile * 2 * CHUNK, (tile + 1) * 2 * CHUNK
                masked = jnp.where(keep[d][half], s_i[:, lo:hi], 0.0)
                s_i = jnp.concatenate([masked, s_i[:, hi:]] if tile == 0 else [s_i[:, :lo], masked], axis=1)
                score_parts.append(s_i.astype(_bf16))
        lhs = jnp.concatenate(score_parts + [p[2] for p in per_dir], axis=0)
        res = _dot(lhs, v_ref[0, rows, :])
        oacc_ref[rows, :] = res[0:GLA_BLOCK] + res[GLA_BLOCK:2 * GLA_BLOCK]
        u_ref[0, blk] = res[2 * GLA_BLOCK:2 * GLA_BLOCK + dk]
        u_ref[1, blk] = res[2 * GLA_BLOCK + dk:2 * GLA_BLOCK + 2 * dk]

    n_pairs = n_blocks // 2
    pre_q, ops_q = {}, {}
    for t in range(n_pairs + 2):
        if t < n_pairs:
            pre_q[t] = [prefix(2 * t), prefix(2 * t + 1)]
        if 0 <= t - 1 < n_pairs:
            ops_q[t - 1] = [operands(p) for p in pre_q.pop(t - 1)]
        if 0 <= t - 2 < n_pairs:
            pair = ops_q.pop(t - 2)
            mats = [scores(o) for o in pair]
            for o, m in zip(pair, mats):
                outputs(o, m)

    def finish(blk):
        rows = pl.ds(blk * GLA_BLOCK, GLA_BLOCK)
        o = oacc_ref[rows, :]
        o = o * lax.rsqrt(jnp.mean(o * o, axis=-1, keepdims=True) + EPS) * g_ref[0]
        o_ref[0, rows, :] = o.astype(_bf16)

    states = [None, None]
    for i in range(n_blocks):
        for d in range(2):
            blk = i if d == 0 else n_blocks - 1 - i
            rows = pl.ds(blk * GLA_BLOCK, GLA_BLOCK)
            s = states[d]
            if s is None:
                states[d] = u_ref[d, blk]
                continue
            oacc_ref[rows, :] += _dot(qd_ref[d, rows, :], s.astype(_bf16))
            if i + 1 < n_blocks:
                dec = dec_ref[d, blk]
                states[d] = jnp.concatenate([dec, dec], axis=1) * s + u_ref[d, blk]
        if 2 * i + 1 >= n_blocks:
            for blk in sorted({i, n_blocks - 1 - i}):
                finish(blk)


def _gla(tri, qk, v, la, gla_norm_g3, layer):
    b, seq, _ = v.shape
    h = N_GLA_HEADS
    n_blocks = seq // GLA_BLOCK
    return pl.pallas_call(
        functools.partial(_gla_kernel, seq=seq),
        grid=(b, h),
        in_specs=[
            pl.BlockSpec((GLA_BLOCK, GLA_BLOCK), lambda i, j: (0, 0)),
            pl.BlockSpec((1, seq, GLA_HEAD_K), lambda i, j: (i, 0, j)),
            pl.BlockSpec((1, seq, GLA_HEAD_K), lambda i, j: (i, 0, h + j)),
            pl.BlockSpec((1, seq, GLA_HEAD_V), lambda i, j: (i, 0, j)),
            pl.BlockSpec((1, seq, 2 * GLA_HEAD_K), lambda i, j: (i, 0, j)),
            pl.BlockSpec((1, 1, GLA_HEAD_V), lambda i, j: (layer, 0, j)),
        ],
        out_specs=pl.BlockSpec((1, seq, GLA_HEAD_V), lambda i, j: (i, 0, j)),
        out_shape=jax.ShapeDtypeStruct((b, seq, D_GLA), _bf16),
        scratch_shapes=[
            pltpu.VMEM((seq, GLA_HEAD_V), _f32),
            pltpu.VMEM((2, n_blocks, GLA_HEAD_K, GLA_HEAD_V), _f32),
            pltpu.VMEM((2, seq, GLA_HEAD_K), _bf16),
            pltpu.VMEM((2, n_blocks, GLA_HEAD_K, GLA_HEAD_K), _f32),
        ],
        compiler_params=_params(),
        name="gla",
    )(tri, qk, qk, v, la, gla_norm_g3)


def _outproj_kernel(x_ref, yf_ref, z_ref, yg_ref, r_ref, w_ref, mod_ref, fg_ref, o_ref, *, final):
    y = None
    for y_ref, gate_ref, w0 in ((yf_ref, z_ref, 0), (yg_ref, r_ref, D_FOURIER)):
        for c0 in range(0, y_ref.shape[2], OUTPROJ_K_CHUNK):
            cols = slice(c0, c0 + OUTPROJ_K_CHUNK)
            a = (y_ref[0, :, cols].astype(_f32) * _silu(gate_ref[0, :, cols].astype(_f32))).astype(_bf16)
            part = _dot(a, w_ref[0, w0 + c0:w0 + c0 + OUTPROJ_K_CHUNK, :].astype(_bf16))
            y = part if y is None else y + part
    gate = mod_ref[0, 0][2:3, :]
    xn = x_ref[0] + gate * y
    if final:
        ms = jnp.mean(xn * xn, axis=-1, keepdims=True)
        xn = xn * lax.rsqrt(ms + EPS) * fg_ref[...]
    o_ref[0] = xn


def _outproj(x, yf, z, yg, r, w_out, mod, final_g, layer, tm, final):
    b, seq, d = x.shape
    row = lambda i, j: (i, j, 0)
    return pl.pallas_call(
        functools.partial(_outproj_kernel, final=final),
        grid=(b, seq // tm),
        in_specs=[
            pl.BlockSpec((1, tm, d), row),
            pl.BlockSpec((1, tm, D_FOURIER), row),
            pl.BlockSpec((1, tm, D_FOURIER), row),
            pl.BlockSpec((1, tm, D_GLA), row),
            pl.BlockSpec((1, tm, D_GLA), row),
            pl.BlockSpec((1, D_FOURIER + D_GLA, d), lambda i, j: (layer, 0, 0),
                         pipeline_mode=pl.Buffered(1)),
            pl.BlockSpec((1, 1, 3, d), lambda i, j: (layer, i, 0, 0)),
            pl.BlockSpec((1, d), lambda i, j: (0, 0)),
        ],
        out_specs=pl.BlockSpec((1, tm, d), row),
        out_shape=jax.ShapeDtypeStruct((b, seq, d), _f32),
        compiler_params=_params(),
        name="outproj",
    )(x, yf, z, yg, r, w_out, mod, final_g)


def _block_tri():
    idx = np.arange(GLA_BLOCK)
    return jnp.asarray((idx[None, :] <= idx[:, None]).astype(np.float32), _bf16)


def kernel(x, c, norm_g, w_ada, b_ada, w_in, w_fmap, w_af, b_af, w_ab, b_ab, gla_norm_g, w_out, final_g):
    depth = w_in.shape[0]
    b, seq, d = x.shape
    assert d == D_MODEL and seq % (2 * GLA_BLOCK) == 0 and seq % (FFT_RADIX * FFT_ROWS) == 0

    mod = _ada_mod(c, w_ada, b_ada).reshape(depth, b, 3, d)
    ab = _fold_fourier_weights(w_fmap, seq)
    w_in_t = jnp.swapaxes(w_in, 1, 2)
    w_all, w_gate = _prep_weights(w_in_t)

    zeros = jnp.zeros_like(w_af)
    wc = jnp.concatenate([jnp.concatenate([w_af, zeros], axis=1),
                          jnp.concatenate([zeros, w_ab], axis=1)], axis=-1)
    bc = jnp.concatenate([b_af, b_ab], axis=-1)

    def per_head(a):
        lead = a.shape[:-1]
        a = a.reshape(lead + (2, N_GLA_HEADS, GLA_HEAD_K))
        return jnp.swapaxes(a, -3, -2).reshape(lead + (2 * D_GLA_KEY,))

    wc = per_head(wc)
    bc = per_head(bc).reshape(depth, 1, 2 * D_GLA_KEY)

    lm = _fft_matrices(seq)
    tri = _block_tri()
    norm_g3 = norm_g.reshape(depth, 1, d)
    gla_norm_g3 = gla_norm_g.reshape(depth, 1, D_GLA)
    final_g2 = final_g.reshape(1, d)

    for l in range(depth):
        p, q, z, qk, v, r, la = _inproj(x, mod, norm_g3, w_all, ab, w_gate, wc, bc, layer=l, tm=1024)
        y_f = _seqfft(lm, p, q)
        y_g = _gla(tri, qk, v, la, gla_norm_g3, layer=l)
        x = _outproj(x, y_f, z, y_g, r, w_out, mod, final_g2, layer=l, tm=512,
                     final=(l == depth - 1))
    return x
```

```python
import functools

import numpy as np
import jax
import jax.numpy as jnp
from jax import lax
from jax.experimental import pallas as pl
from jax.experimental.pallas import tpu as pltpu

D_MODEL = 1024
D_FOURIER = 1024
N_FOURIER_GROUPS = 4
FOURIER_GROUP = 256
N_GLA_HEADS = 4
D_GLA_KEY = 512
D_GLA = 1024
GLA_HEAD_K = 128
GLA_HEAD_V = 256
GATE_RANK = 16
GATE_LOGIT_NORMALIZER = 16.0
CHUNK = 64
EPS = 1e-6
LOG2_E = 1.4426950408889634

_OFF_Z = D_FOURIER
_OFF_QK = 2 * D_FOURIER
_OFF_V = _OFF_QK + 2 * D_GLA_KEY
_OFF_R = _OFF_V + D_GLA
D_MAIN = _OFF_R + D_GLA

GLA_BLOCK = 256
CHUNKS_PER_BLOCK = GLA_BLOCK // CHUNK

FFT_RADIX = 8
FFT_ROWS = 16
FFT_HALVES = 2
LANES = 128
OUTPROJ_K_CHUNK = 256
GATE_COLS = 256

VMEM_LIMIT_BYTES = 56 * 1024 * 1024

_f32 = jnp.float32
_bf16 = jnp.bfloat16


def _dot(a, b):
    return jnp.dot(a, b, preferred_element_type=_f32)


def _split_bf16(a):
    hi = a.astype(_bf16)
    lo = (a - hi.astype(_f32)).astype(_bf16)
    return hi, lo


def _silu(v):
    return v * (1.0 / (1.0 + jnp.exp(-v)))


def _params():
    return pltpu.CompilerParams(dimension_semantics=("arbitrary", "arbitrary"),
                                vmem_limit_bytes=VMEM_LIMIT_BYTES)


def _ada_kernel(c_ref, w_ref, b_ref, o_ref):
    c_act = _silu(c_ref[...]).astype(_bf16)
    o_ref[0] = _dot(c_act, w_ref[0].astype(_bf16)) + b_ref[0]


def _ada_mod(c, w_ada, b_ada):
    depth = w_ada.shape[0]
    b = c.shape[0]
    n_tiles = 3
    return pl.pallas_call(
        _ada_kernel,
        grid=(depth, n_tiles),
        in_specs=[
            pl.BlockSpec((b, D_MODEL), lambda l, j: (0, 0)),
            pl.BlockSpec((1, D_MODEL, D_MODEL), lambda l, j: (l, 0, j)),
            pl.BlockSpec((1, 1, D_MODEL), lambda l, j: (l, 0, j)),
        ],
        out_specs=pl.BlockSpec((1, b, D_MODEL), lambda l, j: (l, 0, j)),
        out_shape=jax.ShapeDtypeStruct((depth, b, 3 * D_MODEL), _f32),
        compiler_params=_params(),
        name="ada_mod",
    )(c, w_ada, b_ada.reshape(depth, 1, 3 * D_MODEL))


def _fold_kernel(wf_ref, cc_ref, sc_ref, ab_ref, *, norm):
    hp = lax.Precision.HIGHEST
    for g in range(N_FOURIER_GROUPS):
        wf = wf_ref[0, g]
        a = jnp.dot(cc_ref[...], wf, precision=hp, preferred_element_type=_f32) * norm
        bm = jnp.dot(sc_ref[...], wf, precision=hp, preferred_element_type=_f32) * norm
        ab_ref[0, g] = jnp.concatenate([a, bm], axis=1).astype(_bf16)


def _fold_fourier_weights(w_fmap, seq_len):
    depth = w_fmap.shape[0]
    idx = np.arange(FOURIER_GROUP)
    ang = 2.0 * np.pi * ((idx[:, None] * idx[None, :]) % FOURIER_GROUP) / FOURIER_GROUP
    cc = jnp.asarray(np.cos(ang), _f32)
    sc = jnp.asarray(np.sin(ang), _f32)
    norm = float(1.0 / np.sqrt(seq_len * FOURIER_GROUP))
    ng = N_FOURIER_GROUPS
    gspec = pl.BlockSpec((FOURIER_GROUP, FOURIER_GROUP), lambda l: (0, 0))
    return pl.pallas_call(
        functools.partial(_fold_kernel, norm=norm),
        grid=(depth,),
        in_specs=[
            pl.BlockSpec((1, ng, FOURIER_GROUP, FOURIER_GROUP), lambda l: (l, 0, 0, 0)),
            gspec, gspec,
        ],
        out_specs=pl.BlockSpec((1, ng, FOURIER_GROUP, 2 * FOURIER_GROUP), lambda l: (l, 0, 0, 0)),
        out_shape=jax.ShapeDtypeStruct((depth, ng, FOURIER_GROUP, 2 * FOURIER_GROUP), _bf16),
        compiler_params=pltpu.CompilerParams(dimension_semantics=("arbitrary",),
                                             vmem_limit_bytes=VMEM_LIMIT_BYTES),
        name="fold_fourier",
    )(w_fmap, cc, sc)


PREP_ROWS = 1024


def _prep_main_kernel(wt_ref, o_ref):
    o_ref[0] = wt_ref[0].T.astype(_bf16)


def _prep_gate_kernel(wt_ref, o_ref):
    w = wt_ref[0]
    o_ref[0] = jnp.concatenate([w, w, w, jnp.zeros_like(w)], axis=0).T


def _prep_weights(w_in_t):
    depth, _, d = w_in_t.shape
    rank2 = 2 * GATE_RANK
    w_all = pl.pallas_call(
        _prep_main_kernel,
        grid=(depth, D_MAIN // PREP_ROWS),
        in_specs=[pl.BlockSpec((1, PREP_ROWS, d), lambda l, j: (l, j, 0))],
        out_specs=pl.BlockSpec((1, d, PREP_ROWS), lambda l, j: (l, 0, j)),
        out_shape=jax.ShapeDtypeStruct((depth, d, D_MAIN), _bf16),
        compiler_params=_params(),
        name="prep_main",
    )(w_in_t)
    w_gate = pl.pallas_call(
        _prep_gate_kernel,
        grid=(depth,),
        in_specs=[pl.BlockSpec((1, rank2, d), lambda l: (l, D_MAIN // rank2, 0))],
        out_specs=pl.BlockSpec((1, d, 4 * rank2), lambda l: (l, 0, 0)),
        out_shape=jax.ShapeDtypeStruct((depth, d, 4 * rank2), _f32),
        compiler_params=pltpu.CompilerParams(dimension_semantics=("arbitrary",),
                                             vmem_limit_bytes=VMEM_LIMIT_BYTES),
        name="prep_gate",
    )(w_in_t)
    return w_all, w_gate


def _inproj_kernel(x_ref, mod_ref, g_ref, w_ref, ab_ref, wg_ref, wc_ref, bc_ref,
                   p_ref, q_ref, z_ref, qk_ref, v_ref, r_ref, la_ref, pq_sc):
    x = x_ref[0]
    ms = jnp.mean(x * x, axis=-1, keepdims=True)
    y = x * lax.rsqrt(ms + EPS) * g_ref[0]
    mod = mod_ref[0, 0]
    shift = mod[0:1, :]
    scale = mod[1:2, :]
    h = (y * (1.0 + scale) + shift).astype(_bf16)

    rank2 = 2 * GATE_RANK
    g3 = _dot(h, wg_ref[0].astype(_bf16))
    u = _dot(h, w_ref[0, :, 0:_OFF_Z]).astype(_bf16)
    z_ref[0] = _dot(h, w_ref[0, :, _OFF_Z:_OFF_QK]).astype(_bf16)

    g_hi, g_lo = _split_bf16(g3)
    lane = lax.broadcasted_iota(jnp.int32, g3.shape, 1)
    lhs = jnp.where((lane >= rank2) & (lane < 2 * rank2), g_lo, g_hi)
    w_hi, w_lo = _split_bf16(wc_ref[0])
    rhs = jnp.concatenate([w_hi, w_hi, w_lo, jnp.zeros_like(w_hi)], axis=0)

    def log2_decay(k):
        cols = slice(k * GATE_COLS, (k + 1) * GATE_COLS)
        x_g = _dot(lhs, rhs[:, cols]) + bc_ref[0][:, cols]
        soft = jnp.log2(1.0 + jnp.exp2(jnp.abs(x_g) * -LOG2_E))
        la_ref[0, :, cols] = (jnp.minimum(x_g, 0.0) * (LOG2_E / GATE_LOGIT_NORMALIZER)
                              - soft * (1.0 / GATE_LOGIT_NORMALIZER)).astype(_bf16)

    log2_decay(0)
    qk_ref[0] = _dot(h, w_ref[0, :, _OFF_QK:_OFF_V]).astype(_bf16)
    log2_decay(1)
    n_slab = 2 * FOURIER_GROUP // LANES
    sub_rows = pq_sc.shape[1] // FFT_RADIX
    for g in range(N_FOURIER_GROUPS):
        pq = _dot(u[:, g * FOURIER_GROUP:(g + 1) * FOURIER_GROUP], ab_ref[0, g])
        for s in range(n_slab):
            pq_sc[s] = pq[:, s * LANES:(s + 1) * LANES]
        for s in range(n_slab):
            dst = p_ref if s < n_slab // 2 else q_ref
            c0 = (s % (n_slab // 2)) * LANES
            for n1 in range(FFT_RADIX):
                rows = pq_sc[s, pl.ds(n1, sub_rows, stride=FFT_RADIX), :]
                lo = n1 * FOURIER_GROUP + c0
                dst[0, g, :, lo:lo + LANES] = rows.astype(_bf16)
    log2_decay(2)
    v_ref[0] = _dot(h, w_ref[0, :, _OFF_V:_OFF_R]).astype(_bf16)
    log2_decay(3)
    r_ref[0] = _dot(h, w_ref[0, :, _OFF_R:D_MAIN]).astype(_bf16)


def _inproj(x, mod, norm_g3, w_all, ab, w_gate, wc, bc, layer, tm):
    b, seq, d = x.shape
    row = lambda i, j: (i, j, 0)
    lay3 = lambda i, j: (layer, 0, 0)
    lay4 = lambda i, j: (layer, 0, 0, 0)
    ng = N_FOURIER_GROUPS

    def out(width, dtype=_bf16):
        return (pl.BlockSpec((1, tm, width), row),
                jax.ShapeDtypeStruct((b, seq, width), dtype))

    def grouped():
        width = FFT_RADIX * FOURIER_GROUP
        return (pl.BlockSpec((1, ng, tm // FFT_RADIX, width), lambda i, j: (i, 0, j, 0)),
                jax.ShapeDtypeStruct((b, ng, seq // FFT_RADIX, width), _bf16))

    outs = [grouped(), grouped(), out(1024), out(1024), out(1024), out(1024), out(1024)]
    return pl.pallas_call(
        _inproj_kernel,
        grid=(b, seq // tm),
        in_specs=[
            pl.BlockSpec((1, tm, d), row),
            pl.BlockSpec((1, 1, 3, d), lambda i, j: (layer, i, 0, 0)),
            pl.BlockSpec((1, 1, d), lay3),
            pl.BlockSpec((1, d, D_MAIN), lay3, pipeline_mode=pl.Buffered(1)),
            pl.BlockSpec((1, ng, FOURIER_GROUP, 2 * FOURIER_GROUP), lay4),
            pl.BlockSpec((1, d, 8 * GATE_RANK), lay3),
            pl.BlockSpec((1, 2 * GATE_RANK, 2 * D_GLA_KEY), lay3),
            pl.BlockSpec((1, 1, 2 * D_GLA_KEY), lay3),
        ],
        out_specs=[o[0] for o in outs],
        out_shape=[o[1] for o in outs],
        scratch_shapes=[pltpu.VMEM((2 * FOURIER_GROUP // LANES, tm, LANES), _f32)],
        compiler_params=_params(),
        name="inproj",
    )(x, mod, norm_g3, w_all, ab, w_gate, wc, bc)


def _fft_kernel(lm_ref, p_ref, q_ref, o_ref, b_ref):
    ng = p_ref.shape[1]
    sub = p_ref.shape[2]
    grp = o_ref.shape[2] // ng
    half = sub // FFT_HALVES
    for g in range(ng):
        for h in range(FFT_HALVES):
            for n1 in range(FFT_RADIX):
                cols = slice(n1 * grp, (n1 + 1) * grp)
                b_ref[g, n1, h] = (_dot(lm_ref[n1, h, :, 0:sub], p_ref[0, g, :, cols])
                                   + _dot(lm_ref[n1, h, :, sub:2 * sub], q_ref[0, g, :, cols]))

    rt = np.float32(np.sqrt(0.5))

    def combine(g, h, c):
        r0 = c * FFT_ROWS
        for t in range(grp // LANES):
            lanes = slice(t * LANES, (t + 1) * LANES)
            out_lanes = slice(g * grp + t * LANES, g * grp + (t + 1) * LANES)
            a = [b_ref[g, n, h, r0:r0 + FFT_ROWS, lanes] for n in range(FFT_RADIX)]
            bi = [b_ref[g, n, h, half + r0:half + r0 + FFT_ROWS, lanes] for n in range(FFT_RADIX)]
            e0, e1 = a[0] + a[4], a[0] - a[4]
            p1, p2, p3 = a[1] + a[7], a[2] + a[6], a[3] + a[5]
            s13, rd, t0 = p1 + p3, rt * (p1 - p3), e0 + p2
            c0, c4, c2, c1, c3 = t0 + s13, t0 - s13, e0 - p2, e1 + rd, e1 - rd
            m1, m2, m3 = bi[1] - bi[7], bi[2] - bi[6], bi[3] - bi[5]
            rs = rt * (m1 + m3)
            s1, s3, s2 = m2 + rs, rs - m2, m1 - m3
            ys = (c0, c1 - s1, c2 - s2, c3 - s3, c4, c3 + s3, c2 + s2, c1 + s1)
            for k1, yk in enumerate(ys):
                row = k1 * sub + h * half + r0
                o_ref[0, row:row + FFT_ROWS, out_lanes] = yk.astype(_bf16)

    for g in range(ng):
        for h in range(FFT_HALVES):
            for c in range(half // FFT_ROWS):
                combine(g, h, c)


def _fft_matrices(seq):
    sub = seq // FFT_RADIX
    half = sub // FFT_HALVES
    n2 = np.arange(sub)[None, :]
    mats = []
    for n1 in range(FFT_RADIX):
        halves = []
        for h in range(FFT_HALVES):
            k2 = np.arange(h * half, (h + 1) * half)[:, None]
            ang = 2.0 * np.pi * ((k2 * (n1 + FFT_RADIX * n2)) % seq) / seq
            c, s = np.cos(ang), np.sin(ang)
            halves.append(np.block([[c, -s], [s, c]]))
        mats.append(np.stack(halves))
    return jnp.asarray(np.stack(mats), _f32).astype(_bf16)


def _seqfft(lm, p, q):
    b, ng, sub, width = p.shape
    grp = width // FFT_RADIX
    seq = sub * FFT_RADIX
    dec = pl.BlockSpec((1, ng, sub, FFT_RADIX * grp), lambda i: (i, 0, 0, 0))
    return pl.pallas_call(
        _fft_kernel,
        grid=(b,),
        in_specs=[
            pl.BlockSpec((FFT_RADIX, FFT_HALVES, 2 * sub // FFT_HALVES, 2 * sub),
                         lambda i: (0, 0, 0, 0), pipeline_mode=pl.Buffered(1)),
            dec, dec,
        ],
        out_specs=pl.BlockSpec((1, seq, ng * grp), lambda i: (i, 0, 0)),
        out_shape=jax.ShapeDtypeStruct((b, seq, ng * grp), _bf16),
        scratch_shapes=[pltpu.VMEM((ng, FFT_RADIX, FFT_HALVES, 2 * sub // FFT_HALVES, grp), _f32)],
        compiler_params=pltpu.CompilerParams(dimension_semantics=("arbitrary",),
                                             vmem_limit_bytes=VMEM_LIMIT_BYTES),
        name="seqfft",
    )(lm, p, q)


def _gla_kernel(tri_ref, q_ref, k_ref, v_ref, la_ref, g_ref, o_ref,
                oacc_ref, u_ref, qd_ref, dec_ref, *, seq):
    n_blocks = seq // GLA_BLOCK
    nb = CHUNKS_PER_BLOCK
    dk = GLA_HEAD_K
    scale = dk ** -0.5

    def chunked(a):
        return a.reshape(nb, CHUNK, a.shape[-1])

    def flat(a):
        return a.reshape(a.shape[0] * a.shape[1], a.shape[2])

    lane = lax.broadcasted_iota(jnp.int32, (CHUNK, 2 * CHUNK), 1)
    row_l = lax.broadcasted_iota(jnp.int32, (CHUNK, 2 * CHUNK), 0)
    keep = (((lane >= CHUNK) | (lane <= row_l), (lane < CHUNK) | (lane - CHUNK <= row_l)),
            ((lane >= CHUNK) | (lane > row_l), (lane < CHUNK) | (lane - CHUNK > row_l)))

    def prefix(blk):
        rows = pl.ds(blk * GLA_BLOCK, GLA_BLOCK)
        la = la_ref[0, rows, :]
        return blk, rows, la[:, dk:].astype(_f32), _dot(tri_ref[...], la)

    def operands(pre_stage):
        blk, rows, la_b, pre = pre_stage
        pre_b = pre[:, dk:]
        g_f = chunked(pre[:, :dk])
        g_b = chunked(pre_b[GLA_BLOCK - 1:GLA_BLOCK, :] - pre_b + la_b)
        zero = jnp.zeros((1, 1, dk), _f32)
        edge_f = g_f[:, CHUNK - 1:CHUNK, :]
        edge_b = g_b[:, 0:1, :]
        dirs = ((g_f, edge_f, jnp.concatenate([zero, edge_f[:nb - 1]], axis=0),
                 CHUNK // 2, list(range(nb))),
                (g_b, edge_b, jnp.concatenate([edge_b[1:], zero], axis=0),
                 CHUNK // 2 - 1, list(range(nb - 1, -1, -1))))

        qs = chunked(q_ref[0, rows, :].astype(_f32) * scale)
        kk = chunked(k_ref[0, rows, :].astype(_f32))
        zeros_chunk = jnp.zeros((CHUNK, dk), _f32)
        per_dir = []
        for d, (g, edge, base, ref_i, order) in enumerate(dirs):
            tot = edge[order[-1]:order[-1] + 1]
            cum = g - base
            ref = cum[:, ref_i:ref_i + 1, :]
            a = cum - ref
            qi = qs * jnp.exp2(a)
            ki = kk * jnp.exp2(-a)
            last_ref = (edge - base) - ref
            qd_ref[d, rows, :] = flat(qi * jnp.exp2(ref + base)).astype(_bf16)
            kd_t = flat(ki * jnp.exp2(last_ref + (tot - edge))).T.astype(_bf16)
            dec_ref[d, blk] = jnp.broadcast_to(jnp.exp2(tot).reshape(1, dk), (dk, dk)).T
            rhs_t = []
            for i in range(nb):
                pieces = []
                for c in range(nb):
                    if c == i:
                        pieces.append(ki[c])
                    elif order.index(c) < order.index(i):
                        pieces.append(ki[c] * jnp.exp2(last_ref[c] + ref[i] + base[i] - edge[c]))
                    else:
                        pieces.append(zeros_chunk)
                rhs_t.append(jnp.concatenate(pieces, axis=0).T.astype(_bf16))
            per_dir.append((flat(qi).astype(_bf16), rhs_t, kd_t))
        return blk, rows, per_dir

    def scores(ops):
        _, _, per_dir = ops
        return [[_dot(qi[i * CHUNK:(i + 1) * CHUNK], rhs_t[i]) for i in range(nb)]
                for qi, rhs_t, _ in per_dir]

    def outputs(ops, mats):
        blk, rows, per_dir = ops
        score_parts = []
        for d, s_rows in enumerate(mats):
            for i, s_i in enumerate(s_rows):
                tile, half = divmod(i, 2)
                lo, hi = tile * 2 * CHUNK, (tile + 1) * 2 * CHUNK
                masked = jnp.where(keep[d][half], s_i[:, lo:hi], 0.0)
                s_i = jnp.concatenate([masked, s_i[:, hi:]] if tile == 0 else [s_i[:, :lo], masked], axis=1)
                score_parts.append(s_i.astype(_bf16))
        lhs = jnp.concatenate(score_parts + [p[2] for p in per_dir], axis=0)
        res = _dot(lhs, v_ref[0, rows, :])
        oacc_ref[rows, :] = res[0:GLA_BLOCK] + res[GLA_BLOCK:2 * GLA_BLOCK]
        u_ref[0, blk] = res[2 * GLA_BLOCK:2 * GLA_BLOCK + dk]
        u_ref[1, blk] = res[2 * GLA_BLOCK + dk:2 * GLA_BLOCK + 2 * dk]

    def recur(d, blk, state, last):
        if state is None:
            return u_ref[d, blk]
        rows = pl.ds(blk * GLA_BLOCK, GLA_BLOCK)
        oacc_ref[rows, :] += _dot(qd_ref[d, rows, :], state.astype(_bf16))
        if last:
            return state
        dec = dec_ref[d, blk]
        return jnp.concatenate([dec, dec], axis=1) * state + u_ref[d, blk]

    def finish(blk):
        rows = pl.ds(blk * GLA_BLOCK, GLA_BLOCK)
        o = oacc_ref[rows, :]
        o = o * lax.rsqrt(jnp.mean(o * o, axis=-1, keepdims=True) + EPS) * g_ref[0]
        o_ref[0, rows, :] = o.astype(_bf16)

    n_pairs = n_blocks // 2
    pre_q, ops_q = {}, {}
    state = [None, None]
    for t in range(n_pairs + 2):
        if t < n_pairs:
            pre_q[t] = [prefix(t), prefix(n_blocks - 1 - t)]
        if 0 <= t - 1 < n_pairs:
            ops_q[t - 1] = [operands(p) for p in pre_q.pop(t - 1)]
        if 0 <= t - 2 < n_pairs:
            pair = ops_q.pop(t - 2)
            mats = [scores(o) for o in pair]
            for d, (o, m) in enumerate(zip(pair, mats)):
                outputs(o, m)
                state[d] = recur(d, o[0], state[d], last=False)
    for j in range(n_pairs, n_blocks):
        for d, blk in enumerate((j, n_blocks - 1 - j)):
            state[d] = recur(d, blk, state[d], last=(j == n_blocks - 1))
            finish(blk)


def _gla(tri, qk, v, la, gla_norm_g3, layer):
    b, seq, _ = v.shape
    h = N_GLA_HEADS
    n_blocks = seq // GLA_BLOCK
    return pl.pallas_call(
        functools.partial(_gla_kernel, seq=seq),
        grid=(b, h),
        in_specs=[
            pl.BlockSpec((GLA_BLOCK, GLA_BLOCK), lambda i, j: (0, 0)),
            pl.BlockSpec((1, seq, GLA_HEAD_K), lambda i, j: (i, 0, j)),
            pl.BlockSpec((1, seq, GLA_HEAD_K), lambda i, j: (i, 0, h + j)),
            pl.BlockSpec((1, seq, GLA_HEAD_V), lambda i, j: (i, 0, j)),
            pl.BlockSpec((1, seq, 2 * GLA_HEAD_K), lambda i, j: (i, 0, j)),
            pl.BlockSpec((1, 1, GLA_HEAD_V), lambda i, j: (layer, 0, j)),
        ],
        out_specs=pl.BlockSpec((1, seq, GLA_HEAD_V), lambda i, j: (i, 0, j)),
        out_shape=jax.ShapeDtypeStruct((b, seq, D_GLA), _bf16),
        scratch_shapes=[
            pltpu.VMEM((seq, GLA_HEAD_V), _f32),
            pltpu.VMEM((2, n_blocks, GLA_HEAD_K, GLA_HEAD_V), _f32),
            pltpu.VMEM((2, seq, GLA_HEAD_K), _bf16),
            pltpu.VMEM((2, n_blocks, GLA_HEAD_K, GLA_HEAD_K), _f32),
        ],
        compiler_params=_params(),
        name="gla",
    )(tri, qk, qk, v, la, gla_norm_g3)


def _outproj_kernel(x_ref, yf_ref, z_ref, yg_ref, r_ref, w_ref, mod_ref, fg_ref, o_ref, *, final):
    y = None
    for y_ref, gate_ref, w0 in ((yf_ref, z_ref, 0), (yg_ref, r_ref, D_FOURIER)):
        for c0 in range(0, y_ref.shape[2], OUTPROJ_K_CHUNK):
            cols = slice(c0, c0 + OUTPROJ_K_CHUNK)
            a = (y_ref[0, :, cols].astype(_f32) * _silu(gate_ref[0, :, cols].astype(_f32))).astype(_bf16)
            part = _dot(a, w_ref[0, w0 + c0:w0 + c0 + OUTPROJ_K_CHUNK, :].astype(_bf16))
            y = part if y is None else y + part
    gate = mod_ref[0, 0][2:3, :]
    xn = x_ref[0] + gate * y
    if final:
        ms = jnp.mean(xn * xn, axis=-1, keepdims=True)
        xn = xn * lax.rsqrt(ms + EPS) * fg_ref[...]
    o_ref[0] = xn


def _outproj(x, yf, z, yg, r, w_out, mod, final_g, layer, tm, final):
    b, seq, d = x.shape
    row = lambda i, j: (i, j, 0)
    return pl.pallas_call(
        functools.partial(_outproj_kernel, final=final),
        grid=(b, seq // tm),
        in_specs=[
            pl.BlockSpec((1, tm, d), row),
            pl.BlockSpec((1, tm, D_FOURIER), row),
            pl.BlockSpec((1, tm, D_FOURIER), row),
            pl.BlockSpec((1, tm, D_GLA), row),
            pl.BlockSpec((1, tm, D_GLA), row),
            pl.BlockSpec((1, D_FOURIER + D_GLA, d), lambda i, j: (layer, 0, 0),
                         pipeline_mode=pl.Buffered(1)),
            pl.BlockSpec((1, 1, 3, d), lambda i, j: (layer, i, 0, 0)),
            pl.BlockSpec((1, d), lambda i, j: (0, 0)),
        ],
        out_specs=pl.BlockSpec((1, tm, d), row),
        out_shape=jax.ShapeDtypeStruct((b, seq, d), _f32),
        compiler_params=_params(),
        name="outproj",
    )(x, yf, z, yg, r, w_out, mod, final_g)


def _block_tri():
    idx = np.arange(GLA_BLOCK)
    return jnp.asarray((idx[None, :] <= idx[:, None]).astype(np.float32), _bf16)


def kernel(x, c, norm_g, w_ada, b_ada, w_in, w_fmap, w_af, b_af, w_ab, b_ab, gla_norm_g, w_out, final_g):
    depth = w_in.shape[0]
    b, seq, d = x.shape
    assert d == D_MODEL and seq % (2 * GLA_BLOCK) == 0 and seq % (FFT_RADIX * FFT_ROWS) == 0

    mod = _ada_mod(c, w_ada, b_ada).reshape(depth, b, 3, d)
    ab = _fold_fourier_weights(w_fmap, seq)
    w_in_t = jnp.swapaxes(w_in, 1, 2)
    w_all, w_gate = _prep_weights(w_in_t)

    zeros = jnp.zeros_like(w_af)
    wc = jnp.concatenate([jnp.concatenate([w_af, zeros], axis=1),
                          jnp.concatenate([zeros, w_ab], axis=1)], axis=-1)
    bc = jnp.concatenate([b_af, b_ab], axis=-1)

    def per_head(a):
        lead = a.shape[:-1]
        a = a.reshape(lead + (2, N_GLA_HEADS, GLA_HEAD_K))
        return jnp.swapaxes(a, -3, -2).reshape(lead + (2 * D_GLA_KEY,))

    wc = per_head(wc)
    bc = per_head(bc).reshape(depth, 1, 2 * D_GLA_KEY)

    lm = _fft_matrices(seq)
    tri = _block_tri()
    norm_g3 = norm_g.reshape(depth, 1, d)
    gla_norm_g3 = gla_norm_g.reshape(depth, 1, D_GLA)
    final_g2 = final_g.reshape(1, d)

    for l in range(depth):
        p, q, z, qk, v, r, la = _inproj(x, mod, norm_g3, w_all, ab, w_gate, wc, bc, layer=l, tm=1024)
        y_f = _seqfft(lm, p, q)
        y_g = _gla(tri, qk, v, la, gla_norm_g3, layer=l)
        x = _outproj(x, y_f, z, y_g, r, w_out, mod, final_g2, layer=l, tm=1024,
                     final=(l == depth - 1))
    return x
```

```python
import functools

import numpy as np
import jax
import jax.numpy as jnp
from jax import lax
from jax.experimental import pallas as pl
from jax.experimental.pallas import tpu as pltpu

D_MODEL = 1024
D_FOURIER = 1024
N_FOURIER_GROUPS = 4
FOURIER_GROUP = 256
N_GLA_HEADS = 4
D_GLA_KEY = 512
D_GLA = 1024
GLA_HEAD_K = 128
GLA_HEAD_V = 256
GATE_RANK = 16
GATE_LOGIT_NORMALIZER = 16.0
CHUNK = 64
EPS = 1e-6
LOG2_E = 1.4426950408889634

_OFF_Z = D_FOURIER
_OFF_QK = 2 * D_FOURIER
_OFF_V = _OFF_QK + 2 * D_GLA_KEY
_OFF_R = _OFF_V + D_GLA
D_MAIN = _OFF_R + D_GLA

GLA_BLOCK = 256
CHUNKS_PER_BLOCK = GLA_BLOCK // CHUNK
GLA_HEADS_PER_STEP = 2

FFT_RADIX = 8
FFT_ROWS = 16
FFT_HALVES = 2
LANES = 128
OUTPROJ_K_CHUNK = 256
GATE_COLS = 256

VMEM_LIMIT_BYTES = 56 * 1024 * 1024

_f32 = jnp.float32
_bf16 = jnp.bfloat16


def _dot(a, b):
    return jnp.dot(a, b, preferred_element_type=_f32)


def _split_bf16(a):
    hi = a.astype(_bf16)
    lo = (a - hi.astype(_f32)).astype(_bf16)
    return hi, lo


def _silu(v):
    return v * (1.0 / (1.0 + jnp.exp(-v)))


def _params():
    return pltpu.CompilerParams(dimension_semantics=("arbitrary", "arbitrary"),
                                vmem_limit_bytes=VMEM_LIMIT_BYTES)


def _ada_kernel(c_ref, w_ref, b_ref, o_ref):
    c_act = _silu(c_ref[...]).astype(_bf16)
    o_ref[0] = _dot(c_act, w_ref[0].astype(_bf16)) + b_ref[0]


def _ada_mod(c, w_ada, b_ada):
    depth = w_ada.shape[0]
    b = c.shape[0]
    n_tiles = 3
    return pl.pallas_call(
        _ada_kernel,
        grid=(depth, n_tiles),
        in_specs=[
            pl.BlockSpec((b, D_MODEL), lambda l, j: (0, 0)),
            pl.BlockSpec((1, D_MODEL, D_MODEL), lambda l, j: (l, 0, j)),
            pl.BlockSpec((1, 1, D_MODEL), lambda l, j: (l, 0, j)),
        ],
        out_specs=pl.BlockSpec((1, b, D_MODEL), lambda l, j: (l, 0, j)),
        out_shape=jax.ShapeDtypeStruct((depth, b, 3 * D_MODEL), _f32),
        compiler_params=_params(),
        name="ada_mod",
    )(c, w_ada, b_ada.reshape(depth, 1, 3 * D_MODEL))


def _fold_kernel(wf_ref, cc_ref, sc_ref, ab_ref, *, norm):
    hp = lax.Precision.HIGHEST
    for g in range(N_FOURIER_GROUPS):
        wf = wf_ref[0, g]
        a = jnp.dot(cc_ref[...], wf, precision=hp, preferred_element_type=_f32) * norm
        bm = jnp.dot(sc_ref[...], wf, precision=hp, preferred_element_type=_f32) * norm
        ab_ref[0, g] = jnp.concatenate([a, bm], axis=1).astype(_bf16)


def _fold_fourier_weights(w_fmap, seq_len):
    depth = w_fmap.shape[0]
    idx = np.arange(FOURIER_GROUP)
    ang = 2.0 * np.pi * ((idx[:, None] * idx[None, :]) % FOURIER_GROUP) / FOURIER_GROUP
    cc = jnp.asarray(np.cos(ang), _f32)
    sc = jnp.asarray(np.sin(ang), _f32)
    norm = float(1.0 / np.sqrt(seq_len * FOURIER_GROUP))
    ng = N_FOURIER_GROUPS
    gspec = pl.BlockSpec((FOURIER_GROUP, FOURIER_GROUP), lambda l: (0, 0))
    return pl.pallas_call(
        functools.partial(_fold_kernel, norm=norm),
        grid=(depth,),
        in_specs=[
            pl.BlockSpec((1, ng, FOURIER_GROUP, FOURIER_GROUP), lambda l: (l, 0, 0, 0)),
            gspec, gspec,
        ],
        out_specs=pl.BlockSpec((1, ng, FOURIER_GROUP, 2 * FOURIER_GROUP), lambda l: (l, 0, 0, 0)),
        out_shape=jax.ShapeDtypeStruct((depth, ng, FOURIER_GROUP, 2 * FOURIER_GROUP), _bf16),
        compiler_params=pltpu.CompilerParams(dimension_semantics=("arbitrary",),
                                             vmem_limit_bytes=VMEM_LIMIT_BYTES),
        name="fold_fourier",
    )(w_fmap, cc, sc)


PREP_ROWS = 1024


def _prep_main_kernel(wt_ref, o_ref):
    o_ref[0] = wt_ref[0].T.astype(_bf16)


def _prep_gate_kernel(wt_ref, o_ref):
    w = wt_ref[0]
    o_ref[0] = jnp.concatenate([w, w, w, jnp.zeros_like(w)], axis=0).T


def _prep_weights(w_in_t):
    depth, _, d = w_in_t.shape
    rank2 = 2 * GATE_RANK
    w_all = pl.pallas_call(
        _prep_main_kernel,
        grid=(depth, D_MAIN // PREP_ROWS),
        in_specs=[pl.BlockSpec((1, PREP_ROWS, d), lambda l, j: (l, j, 0))],
        out_specs=pl.BlockSpec((1, d, PREP_ROWS), lambda l, j: (l, 0, j)),
        out_shape=jax.ShapeDtypeStruct((depth, d, D_MAIN), _bf16),
        compiler_params=_params(),
        name="prep_main",
    )(w_in_t)
    w_gate = pl.pallas_call(
        _prep_gate_kernel,
        grid=(depth,),
        in_specs=[pl.BlockSpec((1, rank2, d), lambda l: (l, D_MAIN // rank2, 0))],
        out_specs=pl.BlockSpec((1, d, 4 * rank2), lambda l: (l, 0, 0)),
        out_shape=jax.ShapeDtypeStruct((depth, d, 4 * rank2), _f32),
        compiler_params=pltpu.CompilerParams(dimension_semantics=("arbitrary",),
                                             vmem_limit_bytes=VMEM_LIMIT_BYTES),
        name="prep_gate",
    )(w_in_t)
    return w_all, w_gate


def _inproj_kernel(x_ref, mod_ref, g_ref, w_ref, ab_ref, wg_ref, wc_ref, bc_ref,
                   p_ref, q_ref, z_ref, qk_ref, v_ref, r_ref, la_ref, pq_sc):
    x = x_ref[0]
    ms = jnp.mean(x * x, axis=-1, keepdims=True)
    y = x * lax.rsqrt(ms + EPS) * g_ref[0]
    mod = mod_ref[0, 0]
    shift = mod[0:1, :]
    scale = mod[1:2, :]
    h = (y * (1.0 + scale) + shift).astype(_bf16)

    rank2 = 2 * GATE_RANK
    g3 = _dot(h, wg_ref[0].astype(_bf16))
    u = _dot(h, w_ref[0, :, 0:_OFF_Z]).astype(_bf16)
    z_ref[0] = _dot(h, w_ref[0, :, _OFF_Z:_OFF_QK]).astype(_bf16)

    g_hi, g_lo = _split_bf16(g3)
    lane = lax.broadcasted_iota(jnp.int32, g3.shape, 1)
    lhs = jnp.where((lane >= rank2) & (lane < 2 * rank2), g_lo, g_hi)
    w_hi, w_lo = _split_bf16(wc_ref[0])
    rhs = jnp.concatenate([w_hi, w_hi, w_lo, jnp.zeros_like(w_hi)], axis=0)

    def log2_decay(k):
        cols = slice(k * GATE_COLS, (k + 1) * GATE_COLS)
        x_g = _dot(lhs, rhs[:, cols]) + bc_ref[0][:, cols]
        soft = jnp.log2(1.0 + jnp.exp2(jnp.abs(x_g) * -LOG2_E))
        la_ref[0, :, cols] = (jnp.minimum(x_g, 0.0) * (LOG2_E / GATE_LOGIT_NORMALIZER)
                              - soft * (1.0 / GATE_LOGIT_NORMALIZER)).astype(_bf16)

    log2_decay(0)
    qk_ref[0] = _dot(h, w_ref[0, :, _OFF_QK:_OFF_V]).astype(_bf16)
    log2_decay(1)
    n_slab = 2 * FOURIER_GROUP // LANES
    sub_rows = pq_sc.shape[1] // FFT_RADIX
    for g in range(N_FOURIER_GROUPS):
        pq = _dot(u[:, g * FOURIER_GROUP:(g + 1) * FOURIER_GROUP], ab_ref[0, g])
        for s in range(n_slab):
            pq_sc[s] = pq[:, s * LANES:(s + 1) * LANES]
        for s in range(n_slab):
            dst = p_ref if s < n_slab // 2 else q_ref
            c0 = (s % (n_slab // 2)) * LANES
            for n1 in range(FFT_RADIX):
                rows = pq_sc[s, pl.ds(n1, sub_rows, stride=FFT_RADIX), :]
                lo = n1 * FOURIER_GROUP + c0
                dst[0, g, :, lo:lo + LANES] = rows.astype(_bf16)
    log2_decay(2)
    v_ref[0] = _dot(h, w_ref[0, :, _OFF_V:_OFF_R]).astype(_bf16)
    log2_decay(3)
    r_ref[0] = _dot(h, w_ref[0, :, _OFF_R:D_MAIN]).astype(_bf16)


def _inproj(x, mod, norm_g3, w_all, ab, w_gate, wc, bc, layer, tm):
    b, seq, d = x.shape
    row = lambda i, j: (i, j, 0)
    lay3 = lambda i, j: (layer, 0, 0)
    lay4 = lambda i, j: (layer, 0, 0, 0)
    ng = N_FOURIER_GROUPS

    def out(width, dtype=_bf16):
        return (pl.BlockSpec((1, tm, width), row),
                jax.ShapeDtypeStruct((b, seq, width), dtype))

    def grouped():
        width = FFT_RADIX * FOURIER_GROUP
        return (pl.BlockSpec((1, ng, tm // FFT_RADIX, width), lambda i, j: (i, 0, j, 0)),
                jax.ShapeDtypeStruct((b, ng, seq // FFT_RADIX, width), _bf16))

    outs = [grouped(), grouped(), out(1024), out(1024), out(1024), out(1024), out(1024)]
    return pl.pallas_call(
        _inproj_kernel,
        grid=(b, seq // tm),
        in_specs=[
            pl.BlockSpec((1, tm, d), row),
            pl.BlockSpec((1, 1, 3, d), lambda i, j: (layer, i, 0, 0)),
            pl.BlockSpec((1, 1, d), lay3),
            pl.BlockSpec((1, d, D_MAIN), lay3, pipeline_mode=pl.Buffered(1)),
            pl.BlockSpec((1, ng, FOURIER_GROUP, 2 * FOURIER_GROUP), lay4),
            pl.BlockSpec((1, d, 8 * GATE_RANK), lay3),
            pl.BlockSpec((1, 2 * GATE_RANK, 2 * D_GLA_KEY), lay3),
            pl.BlockSpec((1, 1, 2 * D_GLA_KEY), lay3),
        ],
        out_specs=[o[0] for o in outs],
        out_shape=[o[1] for o in outs],
        scratch_shapes=[pltpu.VMEM((2 * FOURIER_GROUP // LANES, tm, LANES), _f32)],
        compiler_params=_params(),
        name="inproj",
    )(x, mod, norm_g3, w_all, ab, w_gate, wc, bc)


def _fft_kernel(lm_ref, p_ref, q_ref, o_ref, b_ref):
    ng = p_ref.shape[1]
    sub = p_ref.shape[2]
    grp = o_ref.shape[2] // ng
    half = sub // FFT_HALVES
    for g in range(ng):
        for h in range(FFT_HALVES):
            for n1 in range(FFT_RADIX):
                cols = slice(n1 * grp, (n1 + 1) * grp)
                b_ref[g, n1, h] = (_dot(lm_ref[n1, h, :, 0:sub], p_ref[0, g, :, cols])
                                   + _dot(lm_ref[n1, h, :, sub:2 * sub], q_ref[0, g, :, cols]))

    rt = np.float32(np.sqrt(0.5))

    def combine(g, h, c):
        r0 = c * FFT_ROWS
        for t in range(grp // LANES):
            lanes = slice(t * LANES, (t + 1) * LANES)
            out_lanes = slice(g * grp + t * LANES, g * grp + (t + 1) * LANES)
            a = [b_ref[g, n, h, r0:r0 + FFT_ROWS, lanes] for n in range(FFT_RADIX)]
            bi = [b_ref[g, n, h, half + r0:half + r0 + FFT_ROWS, lanes] for n in range(FFT_RADIX)]
            e0, e1 = a[0] + a[4], a[0] - a[4]
            p1, p2, p3 = a[1] + a[7], a[2] + a[6], a[3] + a[5]
            s13, rd, t0 = p1 + p3, rt * (p1 - p3), e0 + p2
            c0, c4, c2, c1, c3 = t0 + s13, t0 - s13, e0 - p2, e1 + rd, e1 - rd
            m1, m2, m3 = bi[1] - bi[7], bi[2] - bi[6], bi[3] - bi[5]
            rs = rt * (m1 + m3)
            s1, s3, s2 = m2 + rs, rs - m2, m1 - m3
            ys = (c0, c1 - s1, c2 - s2, c3 - s3, c4, c3 + s3, c2 + s2, c1 + s1)
            for k1, yk in enumerate(ys):
                row = k1 * sub + h * half + r0
                o_ref[0, row:row + FFT_ROWS, out_lanes] = yk.astype(_bf16)

    for g in range(ng):
        for h in range(FFT_HALVES):
            for c in range(half // FFT_ROWS):
                combine(g, h, c)


def _fft_matrices(seq):
    sub = seq // FFT_RADIX
    half = sub // FFT_HALVES
    n2 = np.arange(sub)[None, :]
    mats = []
    for n1 in range(FFT_RADIX):
        halves = []
        for h in range(FFT_HALVES):
            k2 = np.arange(h * half, (h + 1) * half)[:, None]
            ang = 2.0 * np.pi * ((k2 * (n1 + FFT_RADIX * n2)) % seq) / seq
            c, s = np.cos(ang), np.sin(ang)
            halves.append(np.block([[c, -s], [s, c]]))
        mats.append(np.stack(halves))
    return jnp.asarray(np.stack(mats), _f32).astype(_bf16)


def _seqfft(lm, p, q):
    b, ng, sub, width = p.shape
    grp = width // FFT_RADIX
    seq = sub * FFT_RADIX
    dec = pl.BlockSpec((1, ng, sub, FFT_RADIX * grp), lambda i: (i, 0, 0, 0))
    return pl.pallas_call(
        _fft_kernel,
        grid=(b,),
        in_specs=[
            pl.BlockSpec((FFT_RADIX, FFT_HALVES, 2 * sub // FFT_HALVES, 2 * sub),
                         lambda i: (0, 0, 0, 0), pipeline_mode=pl.Buffered(1)),
            dec, dec,
        ],
        out_specs=pl.BlockSpec((1, seq, ng * grp), lambda i: (i, 0, 0)),
        out_shape=jax.ShapeDtypeStruct((b, seq, ng * grp), _bf16),
        scratch_shapes=[pltpu.VMEM((ng, FFT_RADIX, FFT_HALVES, 2 * sub // FFT_HALVES, grp), _f32)],
        compiler_params=pltpu.CompilerParams(dimension_semantics=("arbitrary",),
                                             vmem_limit_bytes=VMEM_LIMIT_BYTES),
        name="seqfft",
    )(lm, p, q)


def _gla_kernel(tri_ref, q_ref, k_ref, v_ref, la_ref, g_ref, o_ref,
                oacc_ref, u_ref, qd_ref, dec_ref, *, seq, heads):
    n_blocks = seq // GLA_BLOCK
    nb = CHUNKS_PER_BLOCK
    dk = GLA_HEAD_K
    scale = dk ** -0.5

    def chunked(a):
        return a.reshape(nb, CHUNK, a.shape[-1])

    def flat(a):
        return a.reshape(a.shape[0] * a.shape[1], a.shape[2])

    lane = lax.broadcasted_iota(jnp.int32, (CHUNK, 2 * CHUNK), 1)
    row_l = lax.broadcasted_iota(jnp.int32, (CHUNK, 2 * CHUNK), 0)
    keep = (((lane >= CHUNK) | (lane <= row_l), (lane < CHUNK) | (lane - CHUNK <= row_l)),
            ((lane >= CHUNK) | (lane > row_l), (lane < CHUNK) | (lane - CHUNK > row_l)))

    dv = GLA_HEAD_V

    def prefix(hh, blk):
        rows = pl.ds(blk * GLA_BLOCK, GLA_BLOCK)
        la = la_ref[0, rows, hh * 2 * dk:(hh + 1) * 2 * dk]
        return hh, blk, rows, la[:, dk:].astype(_f32), _dot(tri_ref[...], la)

    def operands(pre_stage):
        hh, blk, rows, la_b, pre = pre_stage
        pre_b = pre[:, dk:]
        g_f = chunked(pre[:, :dk])
        g_b = chunked(pre_b[GLA_BLOCK - 1:GLA_BLOCK, :] - pre_b + la_b)
        zero = jnp.zeros((1, 1, dk), _f32)
        edge_f = g_f[:, CHUNK - 1:CHUNK, :]
        edge_b = g_b[:, 0:1, :]
        dirs = ((g_f, edge_f, jnp.concatenate([zero, edge_f[:nb - 1]], axis=0),
                 CHUNK // 2, list(range(nb))),
                (g_b, edge_b, jnp.concatenate([edge_b[1:], zero], axis=0),
                 CHUNK // 2 - 1, list(range(nb - 1, -1, -1))))

        qs = chunked(q_ref[0, rows, hh * dk:(hh + 1) * dk].astype(_f32) * scale)
        kk = chunked(k_ref[0, rows, hh * dk:(hh + 1) * dk].astype(_f32))
        zeros_chunk = jnp.zeros((CHUNK, dk), _f32)
        per_dir = []
        for d, (g, edge, base, ref_i, order) in enumerate(dirs):
            tot = edge[order[-1]:order[-1] + 1]
            cum = g - base
            ref = cum[:, ref_i:ref_i + 1, :]
            a = cum - ref
            qi = qs * jnp.exp2(a)
            ki = kk * jnp.exp2(-a)
            last_ref = (edge - base) - ref
            qd_ref[hh, d, rows, :] = flat(qi * jnp.exp2(ref + base)).astype(_bf16)
            kd_t = flat(ki * jnp.exp2(last_ref + (tot - edge))).T.astype(_bf16)
            dec_ref[hh, d, blk] = jnp.broadcast_to(jnp.exp2(tot).reshape(1, dk), (dk, dk)).T
            rhs_t = []
            for i in range(nb):
                pieces = []
                for c in range(nb):
                    if c == i:
                        pieces.append(ki[c])
                    elif order.index(c) < order.index(i):
                        pieces.append(ki[c] * jnp.exp2(last_ref[c] + ref[i] + base[i] - edge[c]))
                    else:
                        pieces.append(zeros_chunk)
                rhs_t.append(jnp.concatenate(pieces, axis=0).T.astype(_bf16))
            per_dir.append((flat(qi).astype(_bf16), rhs_t, kd_t))
        return hh, blk, rows, per_dir

    def scores(ops):
        per_dir = ops[3]
        return [[_dot(qi[i * CHUNK:(i + 1) * CHUNK], rhs_t[i]) for i in range(nb)]
                for qi, rhs_t, _ in per_dir]

    def outputs(ops, mats):
        hh, blk, rows, per_dir = ops
        score_parts = []
        for d, s_rows in enumerate(mats):
            for i, s_i in enumerate(s_rows):
                tile, half = divmod(i, 2)
                lo, hi = tile * 2 * CHUNK, (tile + 1) * 2 * CHUNK
                masked = jnp.where(keep[d][half], s_i[:, lo:hi], 0.0)
                s_i = jnp.concatenate([masked, s_i[:, hi:]] if tile == 0 else [s_i[:, :lo], masked], axis=1)
                score_parts.append(s_i.astype(_bf16))
        lhs = jnp.concatenate(score_parts + [p[2] for p in per_dir], axis=0)
        res = _dot(lhs, v_ref[0, rows, hh * dv:(hh + 1) * dv])
        oacc_ref[hh, rows, :] = res[0:GLA_BLOCK] + res[GLA_BLOCK:2 * GLA_BLOCK]
        u_ref[hh, 0, blk] = res[2 * GLA_BLOCK:2 * GLA_BLOCK + dk]
        u_ref[hh, 1, blk] = res[2 * GLA_BLOCK + dk:2 * GLA_BLOCK + 2 * dk]

    def recur(hh, d, blk, state, last):
        if state is None:
            return u_ref[hh, d, blk]
        rows = pl.ds(blk * GLA_BLOCK, GLA_BLOCK)
        oacc_ref[hh, rows, :] += _dot(qd_ref[hh, d, rows, :], state.astype(_bf16))
        if last:
            return state
        dec = dec_ref[hh, d, blk]
        return jnp.concatenate([dec, dec], axis=1) * state + u_ref[hh, d, blk]

    def finish(hh, blk):
        rows = pl.ds(blk * GLA_BLOCK, GLA_BLOCK)
        o = oacc_ref[hh, rows, :]
        o = o * lax.rsqrt(jnp.mean(o * o, axis=-1, keepdims=True) + EPS) * g_ref[0][:, hh * dv:(hh + 1) * dv]
        o_ref[0, rows, hh * dv:(hh + 1) * dv] = o.astype(_bf16)

    n_pairs = n_blocks // 2
    items = [(hh, t) for hh in range(heads) for t in range(n_pairs)]
    pre_q, ops_q = {}, {}
    state = {}
    for s in range(len(items) + 2):
        if s < len(items):
            hh, t = items[s]
            pre_q[s] = [prefix(hh, t), prefix(hh, n_blocks - 1 - t)]
        if 0 <= s - 1 < len(items):
            ops_q[s - 1] = [operands(p) for p in pre_q.pop(s - 1)]
        if 0 <= s - 2 < len(items):
            hh, t = items[s - 2]
            pair = ops_q.pop(s - 2)
            mats = [scores(o) for o in pair]
            for d, (o, m) in enumerate(zip(pair, mats)):
                outputs(o, m)
                state[hh, d] = recur(hh, d, o[1], state.get((hh, d)), last=False)
            if t == n_pairs - 1:
                for j in range(n_pairs, n_blocks):
                    for d, blk in enumerate((j, n_blocks - 1 - j)):
                        state[hh, d] = recur(hh, d, blk, state[hh, d], last=(j == n_blocks - 1))
                        finish(hh, blk)


def _gla(tri, qk, v, la, gla_norm_g3, layer):
    b, seq, _ = v.shape
    hp = GLA_HEADS_PER_STEP
    steps = N_GLA_HEADS // hp
    n_blocks = seq // GLA_BLOCK
    return pl.pallas_call(
        functools.partial(_gla_kernel, seq=seq, heads=hp),
        grid=(b, steps),
        in_specs=[
            pl.BlockSpec((GLA_BLOCK, GLA_BLOCK), lambda i, j: (0, 0)),
            pl.BlockSpec((1, seq, hp * GLA_HEAD_K), lambda i, j: (i, 0, j)),
            pl.BlockSpec((1, seq, hp * GLA_HEAD_K), lambda i, j: (i, 0, steps + j)),
            pl.BlockSpec((1, seq, hp * GLA_HEAD_V), lambda i, j: (i, 0, j)),
            pl.BlockSpec((1, seq, hp * 2 * GLA_HEAD_K), lambda i, j: (i, 0, j)),
            pl.BlockSpec((1, 1, hp * GLA_HEAD_V), lambda i, j: (layer, 0, j)),
        ],
        out_specs=pl.BlockSpec((1, seq, hp * GLA_HEAD_V), lambda i, j: (i, 0, j)),
        out_shape=jax.ShapeDtypeStruct((b, seq, D_GLA), _bf16),
        scratch_shapes=[
            pltpu.VMEM((hp, seq, GLA_HEAD_V), _f32),
            pltpu.VMEM((hp, 2, n_blocks, GLA_HEAD_K, GLA_HEAD_V), _f32),
            pltpu.VMEM((hp, 2, seq, GLA_HEAD_K), _bf16),
            pltpu.VMEM((hp, 2, n_blocks, GLA_HEAD_K, GLA_HEAD_K), _f32),
        ],
        compiler_params=_params(),
        name="gla",
    )(tri, qk, qk, v, la, gla_norm_g3)


def _outproj_kernel(x_ref, yf_ref, z_ref, yg_ref, r_ref, w_ref, mod_ref, fg_ref, o_ref, *, final):
    y = None
    for y_ref, gate_ref, w0 in ((yf_ref, z_ref, 0), (yg_ref, r_ref, D_FOURIER)):
        for c0 in range(0, y_ref.shape[2], OUTPROJ_K_CHUNK):
            cols = slice(c0, c0 + OUTPROJ_K_CHUNK)
            a = (y_ref[0, :, cols].astype(_f32) * _silu(gate_ref[0, :, cols].astype(_f32))).astype(_bf16)
            part = _dot(a, w_ref[0, w0 + c0:w0 + c0 + OUTPROJ_K_CHUNK, :].astype(_bf16))
            y = part if y is None else y + part
    gate = mod_ref[0, 0][2:3, :]
    xn = x_ref[0] + gate * y
    if final:
        ms = jnp.mean(xn * xn, axis=-1, keepdims=True)
        xn = xn * lax.rsqrt(ms + EPS) * fg_ref[...]
    o_ref[0] = xn


def _outproj(x, yf, z, yg, r, w_out, mod, final_g, layer, tm, final):
    b, seq, d = x.shape
    row = lambda i, j: (i, j, 0)
    return pl.pallas_call(
        functools.partial(_outproj_kernel, final=final),
        grid=(b, seq // tm),
        in_specs=[
            pl.BlockSpec((1, tm, d), row),
            pl.BlockSpec((1, tm, D_FOURIER), row),
            pl.BlockSpec((1, tm, D_FOURIER), row),
            pl.BlockSpec((1, tm, D_GLA), row),
            pl.BlockSpec((1, tm, D_GLA), row),
            pl.BlockSpec((1, D_FOURIER + D_GLA, d), lambda i, j: (layer, 0, 0),
                         pipeline_mode=pl.Buffered(1)),
            pl.BlockSpec((1, 1, 3, d), lambda i, j: (layer, i, 0, 0)),
            pl.BlockSpec((1, d), lambda i, j: (0, 0)),
        ],
        out_specs=pl.BlockSpec((1, tm, d), row),
        out_shape=jax.ShapeDtypeStruct((b, seq, d), _f32),
        compiler_params=_params(),
        name="outproj",
    )(x, yf, z, yg, r, w_out, mod, final_g)


def _block_tri():
    idx = np.arange(GLA_BLOCK)
    return jnp.asarray((idx[None, :] <= idx[:, None]).astype(np.float32), _bf16)


def kernel(x, c, norm_g, w_ada, b_ada, w_in, w_fmap, w_af, b_af, w_ab, b_ab, gla_norm_g, w_out, final_g):
    depth = w_in.shape[0]
    b, seq, d = x.shape
    assert d == D_MODEL and seq % (2 * GLA_BLOCK) == 0 and seq % (FFT_RADIX * FFT_ROWS) == 0

    mod = _ada_mod(c, w_ada, b_ada).reshape(depth, b, 3, d)
    ab = _fold_fourier_weights(w_fmap, seq)
    w_in_t = jnp.swapaxes(w_in, 1, 2)
    w_all, w_gate = _prep_weights(w_in_t)

    zeros = jnp.zeros_like(w_af)
    wc = jnp.concatenate([jnp.concatenate([w_af, zeros], axis=1),
                          jnp.concatenate([zeros, w_ab], axis=1)], axis=-1)
    bc = jnp.concatenate([b_af, b_ab], axis=-1)

    def per_head(a):
        lead = a.shape[:-1]
        a = a.reshape(lead + (2, N_GLA_HEADS, GLA_HEAD_K))
        return jnp.swapaxes(a, -3, -2).reshape(lead + (2 * D_GLA_KEY,))

    wc = per_head(wc)
    bc = per_head(bc).reshape(depth, 1, 2 * D_GLA_KEY)

    lm = _fft_matrices(seq)
    tri = _block_tri()
    norm_g3 = norm_g.reshape(depth, 1, d)
    gla_norm_g3 = gla_norm_g.reshape(depth, 1, D_GLA)
    final_g2 = final_g.reshape(1, d)

    for l in range(depth):
        p, q, z, qk, v, r, la = _inproj(x, mod, norm_g3, w_all, ab, w_gate, wc, bc, layer=l, tm=1024)
        y_f = _seqfft(lm, p, q)
        y_g = _gla(tri, qk, v, la, gla_norm_g3, layer=l)
        x = _outproj(x, y_f, z, y_g, r, w_out, mod, final_g2, layer=l, tm=1024,
                     final=(l == depth - 1))
    return x
```

```python
import functools

import numpy as np
import jax
import jax.numpy as jnp
from jax import lax
from jax.experimental import pallas as pl
from jax.experimental.pallas import tpu as pltpu

D_MODEL = 1024
D_FOURIER = 1024
N_FOURIER_GROUPS = 4
FOURIER_GROUP = 256
N_GLA_HEADS = 4
D_GLA_KEY = 512
D_GLA = 1024
GLA_HEAD_K = 128
GLA_HEAD_V = 256
GATE_RANK = 16
GATE_LOGIT_NORMALIZER = 16.0
CHUNK = 64
EPS = 1e-6
LOG2_E = 1.4426950408889634

_OFF_Z = D_FOURIER
_OFF_QK = 2 * D_FOURIER
_OFF_V = _OFF_QK + 2 * D_GLA_KEY
_OFF_R = _OFF_V + D_GLA
D_MAIN = _OFF_R + D_GLA

GLA_BLOCK = 256
CHUNKS_PER_BLOCK = GLA_BLOCK // CHUNK
GLA_HEADS_PER_STEP = 2

FFT_RADIX = 8
FFT_ROWS = 16
FFT_HALVES = 2
LANES = 128
OUTPROJ_K_CHUNK = 256
GATE_COLS = 256

VMEM_LIMIT_BYTES = 56 * 1024 * 1024

_f32 = jnp.float32
_bf16 = jnp.bfloat16


def _dot(a, b):
    return jnp.dot(a, b, preferred_element_type=_f32)


def _split_bf16(a):
    hi = a.astype(_bf16)
    lo = (a - hi.astype(_f32)).astype(_bf16)
    return hi, lo


def _silu(v):
    return v * (1.0 / (1.0 + jnp.exp(-v)))


def _params():
    return pltpu.CompilerParams(dimension_semantics=("arbitrary", "arbitrary"),
                                vmem_limit_bytes=VMEM_LIMIT_BYTES)


def _ada_kernel(c_ref, w_ref, b_ref, o_ref):
    c_act = _silu(c_ref[...]).astype(_bf16)
    o_ref[0] = _dot(c_act, w_ref[0].astype(_bf16)) + b_ref[0]


def _ada_mod(c, w_ada, b_ada):
    depth = w_ada.shape[0]
    b = c.shape[0]
    n_tiles = 3
    return pl.pallas_call(
        _ada_kernel,
        grid=(depth, n_tiles),
        in_specs=[
            pl.BlockSpec((b, D_MODEL), lambda l, j: (0, 0)),
            pl.BlockSpec((1, D_MODEL, D_MODEL), lambda l, j: (l, 0, j)),
            pl.BlockSpec((1, 1, D_MODEL), lambda l, j: (l, 0, j)),
        ],
        out_specs=pl.BlockSpec((1, b, D_MODEL), lambda l, j: (l, 0, j)),
        out_shape=jax.ShapeDtypeStruct((depth, b, 3 * D_MODEL), _f32),
        compiler_params=_params(),
        name="ada_mod",
    )(c, w_ada, b_ada.reshape(depth, 1, 3 * D_MODEL))


def _fold_kernel(wf_ref, cc_ref, sc_ref, ab_ref, *, norm):
    hp = lax.Precision.HIGHEST
    for g in range(N_FOURIER_GROUPS):
        wf = wf_ref[0, g]
        a = jnp.dot(cc_ref[...], wf, precision=hp, preferred_element_type=_f32) * norm
        bm = jnp.dot(sc_ref[...], wf, precision=hp, preferred_element_type=_f32) * norm
        ab_ref[0, g] = jnp.concatenate([a, bm], axis=1).astype(_bf16)


def _fold_fourier_weights(w_fmap, seq_len):
    depth = w_fmap.shape[0]
    idx = np.arange(FOURIER_GROUP)
    ang = 2.0 * np.pi * ((idx[:, None] * idx[None, :]) % FOURIER_GROUP) / FOURIER_GROUP
    cc = jnp.asarray(np.cos(ang), _f32)
    sc = jnp.asarray(np.sin(ang), _f32)
    norm = float(1.0 / np.sqrt(seq_len * FOURIER_GROUP))
    ng = N_FOURIER_GROUPS
    gspec = pl.BlockSpec((FOURIER_GROUP, FOURIER_GROUP), lambda l: (0, 0))
    return pl.pallas_call(
        functools.partial(_fold_kernel, norm=norm),
        grid=(depth,),
        in_specs=[
            pl.BlockSpec((1, ng, FOURIER_GROUP, FOURIER_GROUP), lambda l: (l, 0, 0, 0)),
            gspec, gspec,
        ],
        out_specs=pl.BlockSpec((1, ng, FOURIER_GROUP, 2 * FOURIER_GROUP), lambda l: (l, 0, 0, 0)),
        out_shape=jax.ShapeDtypeStruct((depth, ng, FOURIER_GROUP, 2 * FOURIER_GROUP), _bf16),
        compiler_params=pltpu.CompilerParams(dimension_semantics=("arbitrary",),
                                             vmem_limit_bytes=VMEM_LIMIT_BYTES),
        name="fold_fourier",
    )(w_fmap, cc, sc)


PREP_ROWS = 1024


def _prep_main_kernel(wt_ref, o_ref):
    o_ref[0] = wt_ref[0].T.astype(_bf16)


def _prep_gate_kernel(wt_ref, o_ref):
    w = wt_ref[0]
    o_ref[0] = jnp.concatenate([w, w, w, jnp.zeros_like(w)], axis=0).T


def _prep_weights(w_in_t):
    depth, _, d = w_in_t.shape
    rank2 = 2 * GATE_RANK
    w_all = pl.pallas_call(
        _prep_main_kernel,
        grid=(depth, D_MAIN // PREP_ROWS),
        in_specs=[pl.BlockSpec((1, PREP_ROWS, d), lambda l, j: (l, j, 0))],
        out_specs=pl.BlockSpec((1, d, PREP_ROWS), lambda l, j: (l, 0, j)),
        out_shape=jax.ShapeDtypeStruct((depth, d, D_MAIN), _bf16),
        compiler_params=_params(),
        name="prep_main",
    )(w_in_t)
    w_gate = pl.pallas_call(
        _prep_gate_kernel,
        grid=(depth,),
        in_specs=[pl.BlockSpec((1, rank2, d), lambda l: (l, D_MAIN // rank2, 0))],
        out_specs=pl.BlockSpec((1, d, 4 * rank2), lambda l: (l, 0, 0)),
        out_shape=jax.ShapeDtypeStruct((depth, d, 4 * rank2), _f32),
        compiler_params=pltpu.CompilerParams(dimension_semantics=("arbitrary",),
                                             vmem_limit_bytes=VMEM_LIMIT_BYTES),
        name="prep_gate",
    )(w_in_t)
    return w_all, w_gate


def _inproj_kernel(x_ref, mod_ref, g_ref, w_ref, ab_ref, wg_ref, wc_ref, bc_ref,
                   p_ref, q_ref, z_ref, qk_ref, v_ref, r_ref, la_ref, u_sc):
    mod = mod_ref[0, 0]
    shift = mod[0:1, :]
    gain = g_ref[0] * (1.0 + mod[1:2, :])

    def normed(rows):
        x = x_ref[0, rows, :]
        ms = jnp.mean(x * x, axis=-1, keepdims=True)
        return (x * lax.rsqrt(ms + EPS) * gain + shift).astype(_bf16)

    rank2 = 2 * GATE_RANK
    half = x_ref.shape[1] // 2
    w_gate = wg_ref[0].astype(_bf16)
    halves, g3 = [], []
    for r0 in (0, half):
        h_half = normed(slice(r0, r0 + half))
        halves.append(h_half)
        g3.append(_dot(h_half, w_gate))
        u_f32 = _dot(h_half, w_ref[0, :, 0:_OFF_Z])
        for s in range(u_sc.shape[0]):
            u_sc[s, r0:r0 + half, :] = u_f32[:, s * LANES:(s + 1) * LANES]
    h = jnp.concatenate(halves, axis=0)
    g3 = jnp.concatenate(g3, axis=0)
    z_ref[0] = _dot(h, w_ref[0, :, _OFF_Z:_OFF_QK]).astype(_bf16)

    g_hi, g_lo = _split_bf16(g3)
    lane = lax.broadcasted_iota(jnp.int32, g3.shape, 1)
    lhs = jnp.where((lane >= rank2) & (lane < 2 * rank2), g_lo, g_hi)
    w_hi, w_lo = _split_bf16(wc_ref[0])
    rhs = jnp.concatenate([w_hi, w_hi, w_lo, jnp.zeros_like(w_hi)], axis=0)

    def log2_decay(k):
        cols = slice(k * GATE_COLS, (k + 1) * GATE_COLS)
        x_g = _dot(lhs, rhs[:, cols]) + bc_ref[0][:, cols]
        soft = jnp.log2(1.0 + jnp.exp2(jnp.abs(x_g) * -LOG2_E))
        la_ref[0, :, cols] = (jnp.minimum(x_g, 0.0) * (LOG2_E / GATE_LOGIT_NORMALIZER)
                              - soft * (1.0 / GATE_LOGIT_NORMALIZER)).astype(_bf16)

    log2_decay(0)
    qk_ref[0] = _dot(h, w_ref[0, :, _OFF_QK:_OFF_V]).astype(_bf16)
    log2_decay(1)
    sub_rows = u_sc.shape[1] // FFT_RADIX
    slabs_per_group = FOURIER_GROUP // LANES
    for g in range(N_FOURIER_GROUPS):
        u_g = jnp.concatenate(
            [jnp.concatenate([u_sc[g * slabs_per_group + s, pl.ds(n1, sub_rows, stride=FFT_RADIX), :]
                              for s in range(slabs_per_group)], axis=1)
             for n1 in range(FFT_RADIX)], axis=0).astype(_bf16)
        pq = _dot(u_g, ab_ref[0, g])
        for n1 in range(FFT_RADIX):
            rows = slice(n1 * sub_rows, (n1 + 1) * sub_rows)
            cols = slice(n1 * FOURIER_GROUP, (n1 + 1) * FOURIER_GROUP)
            p_ref[0, g, :, cols] = pq[rows, :FOURIER_GROUP].astype(_bf16)
            q_ref[0, g, :, cols] = pq[rows, FOURIER_GROUP:].astype(_bf16)
    log2_decay(2)
    v_ref[0] = _dot(h, w_ref[0, :, _OFF_V:_OFF_R]).astype(_bf16)
    log2_decay(3)
    r_ref[0] = _dot(h, w_ref[0, :, _OFF_R:D_MAIN]).astype(_bf16)


def _inproj(x, mod, norm_g3, w_all, ab, w_gate, wc, bc, layer, tm):
    b, seq, d = x.shape
    row = lambda i, j: (i, j, 0)
    lay3 = lambda i, j: (layer, 0, 0)
    lay4 = lambda i, j: (layer, 0, 0, 0)
    ng = N_FOURIER_GROUPS

    def out(width, dtype=_bf16):
        return (pl.BlockSpec((1, tm, width), row),
                jax.ShapeDtypeStruct((b, seq, width), dtype))

    def grouped():
        width = FFT_RADIX * FOURIER_GROUP
        return (pl.BlockSpec((1, ng, tm // FFT_RADIX, width), lambda i, j: (i, 0, j, 0)),
                jax.ShapeDtypeStruct((b, ng, seq // FFT_RADIX, width), _bf16))

    outs = [grouped(), grouped(), out(1024), out(1024), out(1024), out(1024), out(1024)]
    return pl.pallas_call(
        _inproj_kernel,
        grid=(b, seq // tm),
        in_specs=[
            pl.BlockSpec((1, tm, d), row),
            pl.BlockSpec((1, 1, 3, d), lambda i, j: (layer, i, 0, 0)),
            pl.BlockSpec((1, 1, d), lay3),
            pl.BlockSpec((1, d, D_MAIN), lay3, pipeline_mode=pl.Buffered(1)),
            pl.BlockSpec((1, ng, FOURIER_GROUP, 2 * FOURIER_GROUP), lay4),
            pl.BlockSpec((1, d, 8 * GATE_RANK), lay3),
            pl.BlockSpec((1, 2 * GATE_RANK, 2 * D_GLA_KEY), lay3),
            pl.BlockSpec((1, 1, 2 * D_GLA_KEY), lay3),
        ],
        out_specs=[o[0] for o in outs],
        out_shape=[o[1] for o in outs],
        scratch_shapes=[pltpu.VMEM((D_FOURIER // LANES, tm, LANES), _f32)],
        compiler_params=_params(),
        name="inproj",
    )(x, mod, norm_g3, w_all, ab, w_gate, wc, bc)


def _fft_kernel(lm_ref, p_ref, q_ref, o_ref, b_ref):
    ng = p_ref.shape[1]
    sub = p_ref.shape[2]
    grp = o_ref.shape[2] // ng
    half = sub // FFT_HALVES
    for g in range(ng):
        for h in range(FFT_HALVES):
            for n1 in range(FFT_RADIX):
                cols = slice(n1 * grp, (n1 + 1) * grp)
                b_ref[g, n1, h] = (_dot(lm_ref[n1, h, :, 0:sub], p_ref[0, g, :, cols])
                                   + _dot(lm_ref[n1, h, :, sub:2 * sub], q_ref[0, g, :, cols]))

    rt = np.float32(np.sqrt(0.5))

    def combine(g, h, c):
        r0 = c * FFT_ROWS
        for t in range(grp // LANES):
            lanes = slice(t * LANES, (t + 1) * LANES)
            out_lanes = slice(g * grp + t * LANES, g * grp + (t + 1) * LANES)
            a = [b_ref[g, n, h, r0:r0 + FFT_ROWS, lanes] for n in range(FFT_RADIX)]
            bi = [b_ref[g, n, h, half + r0:half + r0 + FFT_ROWS, lanes] for n in range(FFT_RADIX)]
            e0, e1 = a[0] + a[4], a[0] - a[4]
            p1, p2, p3 = a[1] + a[7], a[2] + a[6], a[3] + a[5]
            s13, rd, t0 = p1 + p3, rt * (p1 - p3), e0 + p2
            c0, c4, c2, c1, c3 = t0 + s13, t0 - s13, e0 - p2, e1 + rd, e1 - rd
            m1, m2, m3 = bi[1] - bi[7], bi[2] - bi[6], bi[3] - bi[5]
            rs = rt * (m1 + m3)
            s1, s3, s2 = m2 + rs, rs - m2, m1 - m3
            ys = (c0, c1 - s1, c2 - s2, c3 - s3, c4, c3 + s3, c2 + s2, c1 + s1)
            for k1, yk in enumerate(ys):
                row = k1 * sub + h * half + r0
                o_ref[0, row:row + FFT_ROWS, out_lanes] = yk.astype(_bf16)

    for g in range(ng):
        for h in range(FFT_HALVES):
            for c in range(half // FFT_ROWS):
                combine(g, h, c)


def _fft_matrices(seq):
    sub = seq // FFT_RADIX
    half = sub // FFT_HALVES
    n2 = np.arange(sub)[None, :]
    mats = []
    for n1 in range(FFT_RADIX):
        halves = []
        for h in range(FFT_HALVES):
            k2 = np.arange(h * half, (h + 1) * half)[:, None]
            ang = 2.0 * np.pi * ((k2 * (n1 + FFT_RADIX * n2)) % seq) / seq
            c, s = np.cos(ang), np.sin(ang)
            halves.append(np.block([[c, -s], [s, c]]))
        mats.append(np.stack(halves))
    return jnp.asarray(np.stack(mats), _f32).astype(_bf16)


def _seqfft(lm, p, q):
    b, ng, sub, width = p.shape
    grp = width // FFT_RADIX
    seq = sub * FFT_RADIX
    dec = pl.BlockSpec((1, ng, sub, FFT_RADIX * grp), lambda i: (i, 0, 0, 0))
    return pl.pallas_call(
        _fft_kernel,
        grid=(b,),
        in_specs=[
            pl.BlockSpec((FFT_RADIX, FFT_HALVES, 2 * sub // FFT_HALVES, 2 * sub),
                         lambda i: (0, 0, 0, 0), pipeline_mode=pl.Buffered(1)),
            dec, dec,
        ],
        out_specs=pl.BlockSpec((1, seq, ng * grp), lambda i: (i, 0, 0)),
        out_shape=jax.ShapeDtypeStruct((b, seq, ng * grp), _bf16),
        scratch_shapes=[pltpu.VMEM((ng, FFT_RADIX, FFT_HALVES, 2 * sub // FFT_HALVES, grp), _f32)],
        compiler_params=pltpu.CompilerParams(dimension_semantics=("arbitrary",),
                                             vmem_limit_bytes=VMEM_LIMIT_BYTES),
        name="seqfft",
    )(lm, p, q)


def _gla_kernel(tri_ref, q_ref, k_ref, v_ref, la_ref, g_ref, o_ref,
                oacc_ref, u_ref, qd_ref, dec_ref, *, seq, heads):
    n_blocks = seq // GLA_BLOCK
    nb = CHUNKS_PER_BLOCK
    dk = GLA_HEAD_K
    scale = dk ** -0.5

    def chunked(a):
        return a.reshape(nb, CHUNK, a.shape[-1])

    def flat(a):
        return a.reshape(a.shape[0] * a.shape[1], a.shape[2])

    lane = lax.broadcasted_iota(jnp.int32, (CHUNK, 2 * CHUNK), 1)
    row_l = lax.broadcasted_iota(jnp.int32, (CHUNK, 2 * CHUNK), 0)
    keep = (((lane >= CHUNK) | (lane <= row_l), (lane < CHUNK) | (lane - CHUNK <= row_l)),
            ((lane >= CHUNK) | (lane > row_l), (lane < CHUNK) | (lane - CHUNK > row_l)))

    dv = GLA_HEAD_V

    def prefix(hh, blk):
        rows = pl.ds(blk * GLA_BLOCK, GLA_BLOCK)
        la = la_ref[0, rows, hh * 2 * dk:(hh + 1) * 2 * dk]
        return hh, blk, rows, la[:, dk:].astype(_f32), _dot(tri_ref[...], la)

    def operands(pre_stage):
        hh, blk, rows, la_b, pre = pre_stage
        pre_b = pre[:, dk:]
        g_f = chunked(pre[:, :dk])
        g_b = chunked(pre_b[GLA_BLOCK - 1:GLA_BLOCK, :] - pre_b + la_b)
        zero = jnp.zeros((1, 1, dk), _f32)
        edge_f = g_f[:, CHUNK - 1:CHUNK, :]
        edge_b = g_b[:, 0:1, :]
        dirs = ((g_f, edge_f, jnp.concatenate([zero, edge_f[:nb - 1]], axis=0),
                 CHUNK // 2, list(range(nb))),
                (g_b, edge_b, jnp.concatenate([edge_b[1:], zero], axis=0),
                 CHUNK // 2 - 1, list(range(nb - 1, -1, -1))))

        qs = chunked(q_ref[0, rows, hh * dk:(hh + 1) * dk].astype(_f32) * scale)
        kk = chunked(k_ref[0, rows, hh * dk:(hh + 1) * dk].astype(_f32))
        zeros_chunk = jnp.zeros((CHUNK, dk), _f32)
        per_dir = []
        for d, (g, edge, base, ref_i, order) in enumerate(dirs):
            tot = edge[order[-1]:order[-1] + 1]
            cum = g - base
            ref = cum[:, ref_i:ref_i + 1, :]
            a = cum - ref
            qi = qs * jnp.exp2(a)
            ki = kk * jnp.exp2(-a)
            last_ref = (edge - base) - ref
            qd_ref[hh, d, rows, :] = flat(qi * jnp.exp2(ref + base)).astype(_bf16)
            kd_t = flat(ki * jnp.exp2(last_ref + (tot - edge))).T.astype(_bf16)
            dec_ref[hh, d, blk] = jnp.broadcast_to(jnp.exp2(tot).reshape(1, dk), (dk, dk)).T
            rhs_t = []
            for i in range(nb):
                pieces = []
                for c in range(nb):
                    if c == i:
                        pieces.append(ki[c])
                    elif order.index(c) < order.index(i):
                        pieces.append(ki[c] * jnp.exp2(last_ref[c] + ref[i] + base[i] - edge[c]))
                    else:
                        pieces.append(zeros_chunk)
                rhs_t.append(jnp.concatenate(pieces, axis=0).T.astype(_bf16))
            per_dir.append((flat(qi).astype(_bf16), rhs_t, kd_t))
        return hh, blk, rows, per_dir

    def scores(ops):
        per_dir = ops[3]
        return [[_dot(qi[i * CHUNK:(i + 1) * CHUNK], rhs_t[i]) for i in range(nb)]
                for qi, rhs_t, _ in per_dir]

    def outputs(ops, mats):
        hh, blk, rows, per_dir = ops
        score_parts = []
        for d, s_rows in enumerate(mats):
            for i, s_i in enumerate(s_rows):
                tile, half = divmod(i, 2)
                lo, hi = tile * 2 * CHUNK, (tile + 1) * 2 * CHUNK
                masked = jnp.where(keep[d][half], s_i[:, lo:hi], 0.0)
                s_i = jnp.concatenate([masked, s_i[:, hi:]] if tile == 0 else [s_i[:, :lo], masked], axis=1)
                score_parts.append(s_i.astype(_bf16))
        lhs = jnp.concatenate(score_parts + [p[2] for p in per_dir], axis=0)
        res = _dot(lhs, v_ref[0, rows, hh * dv:(hh + 1) * dv])
        oacc_ref[hh, rows, :] = res[0:GLA_BLOCK] + res[GLA_BLOCK:2 * GLA_BLOCK]
        u_ref[hh, 0, blk] = res[2 * GLA_BLOCK:2 * GLA_BLOCK + dk]
        u_ref[hh, 1, blk] = res[2 * GLA_BLOCK + dk:2 * GLA_BLOCK + 2 * dk]

    def recur(hh, d, blk, state, last):
        if state is None:
            return u_ref[hh, d, blk]
        rows = pl.ds(blk * GLA_BLOCK, GLA_BLOCK)
        oacc_ref[hh, rows, :] += _dot(qd_ref[hh, d, rows, :], state.astype(_bf16))
        if last:
            return state
        dec = dec_ref[hh, d, blk]
        return jnp.concatenate([dec, dec], axis=1) * state + u_ref[hh, d, blk]

    def finish(hh, blk):
        rows = pl.ds(blk * GLA_BLOCK, GLA_BLOCK)
        o = oacc_ref[hh, rows, :]
        o = o * lax.rsqrt(jnp.mean(o * o, axis=-1, keepdims=True) + EPS) * g_ref[0][:, hh * dv:(hh + 1) * dv]
        o_ref[0, rows, hh * dv:(hh + 1) * dv] = o.astype(_bf16)

    n_pairs = n_blocks // 2
    items = [(hh, t) for hh in range(heads) for t in range(n_pairs)]
    pre_q, ops_q = {}, {}
    state = {}
    for s in range(len(items) + 2):
        if s < len(items):
            hh, t = items[s]
            pre_q[s] = [prefix(hh, t), prefix(hh, n_blocks - 1 - t)]
        if 0 <= s - 1 < len(items):
            ops_q[s - 1] = [operands(p) for p in pre_q.pop(s - 1)]
        if 0 <= s - 2 < len(items):
            hh, t = items[s - 2]
            pair = ops_q.pop(s - 2)
            mats = [scores(o) for o in pair]
            for d, (o, m) in enumerate(zip(pair, mats)):
                outputs(o, m)
                state[hh, d] = recur(hh, d, o[1], state.get((hh, d)), last=False)
            if t == n_pairs - 1:
                for j in range(n_pairs, n_blocks):
                    for d, blk in enumerate((j, n_blocks - 1 - j)):
                        state[hh, d] = recur(hh, d, blk, state[hh, d], last=(j == n_blocks - 1))
                        finish(hh, blk)


def _gla(tri, qk, v, la, gla_norm_g3, layer):
    b, seq, _ = v.shape
    hp = GLA_HEADS_PER_STEP
    steps = N_GLA_HEADS // hp
    n_blocks = seq // GLA_BLOCK
    return pl.pallas_call(
        functools.partial(_gla_kernel, seq=seq, heads=hp),
        grid=(b, steps),
        in_specs=[
            pl.BlockSpec((GLA_BLOCK, GLA_BLOCK), lambda i, j: (0, 0)),
            pl.BlockSpec((1, seq, hp * GLA_HEAD_K), lambda i, j: (i, 0, j)),
            pl.BlockSpec((1, seq, hp * GLA_HEAD_K), lambda i, j: (i, 0, steps + j)),
            pl.BlockSpec((1, seq, hp * GLA_HEAD_V), lambda i, j: (i, 0, j)),
            pl.BlockSpec((1, seq, hp * 2 * GLA_HEAD_K), lambda i, j: (i, 0, j)),
            pl.BlockSpec((1, 1, hp * GLA_HEAD_V), lambda i, j: (layer, 0, j)),
        ],
        out_specs=pl.BlockSpec((1, seq, hp * GLA_HEAD_V), lambda i, j: (i, 0, j)),
        out_shape=jax.ShapeDtypeStruct((b, seq, D_GLA), _bf16),
        scratch_shapes=[
            pltpu.VMEM((hp, seq, GLA_HEAD_V), _f32),
            pltpu.VMEM((hp, 2, n_blocks, GLA_HEAD_K, GLA_HEAD_V), _f32),
            pltpu.VMEM((hp, 2, seq, GLA_HEAD_K), _bf16),
            pltpu.VMEM((hp, 2, n_blocks, GLA_HEAD_K, GLA_HEAD_K), _f32),
        ],
        compiler_params=_params(),
        name="gla",
    )(tri, qk, qk, v, la, gla_norm_g3)


def _outproj_kernel(x_ref, yf_ref, z_ref, yg_ref, r_ref, w_ref, mod_ref, fg_ref, o_ref, *, final):
    y = None
    for y_ref, gate_ref, w0 in ((yf_ref, z_ref, 0), (yg_ref, r_ref, D_FOURIER)):
        for c0 in range(0, y_ref.shape[2], OUTPROJ_K_CHUNK):
            cols = slice(c0, c0 + OUTPROJ_K_CHUNK)
            a = (y_ref[0, :, cols].astype(_f32) * _silu(gate_ref[0, :, cols].astype(_f32))).astype(_bf16)
            part = _dot(a, w_ref[0, w0 + c0:w0 + c0 + OUTPROJ_K_CHUNK, :].astype(_bf16))
            y = part if y is None else y + part
    gate = mod_ref[0, 0][2:3, :]
    xn = x_ref[0] + gate * y
    if final:
        ms = jnp.mean(xn * xn, axis=-1, keepdims=True)
        xn = xn * lax.rsqrt(ms + EPS) * fg_ref[...]
    o_ref[0] = xn


def _outproj(x, yf, z, yg, r, w_out, mod, final_g, layer, tm, final):
    b, seq, d = x.shape
    row = lambda i, j: (i, j, 0)
    return pl.pallas_call(
        functools.partial(_outproj_kernel, final=final),
        grid=(b, seq // tm),
        in_specs=[
            pl.BlockSpec((1, tm, d), row),
            pl.BlockSpec((1, tm, D_FOURIER), row),
            pl.BlockSpec((1, tm, D_FOURIER), row),
            pl.BlockSpec((1, tm, D_GLA), row),
            pl.BlockSpec((1, tm, D_GLA), row),
            pl.BlockSpec((1, D_FOURIER + D_GLA, d), lambda i, j: (layer, 0, 0),
                         pipeline_mode=pl.Buffered(1)),
            pl.BlockSpec((1, 1, 3, d), lambda i, j: (layer, i, 0, 0)),
            pl.BlockSpec((1, d), lambda i, j: (0, 0)),
        ],
        out_specs=pl.BlockSpec((1, tm, d), row),
        out_shape=jax.ShapeDtypeStruct((b, seq, d), _f32),
        compiler_params=_params(),
        name="outproj",
    )(x, yf, z, yg, r, w_out, mod, final_g)


def _block_tri():
    idx = np.arange(GLA_BLOCK)
    return jnp.asarray((idx[None, :] <= idx[:, None]).astype(np.float32), _bf16)


def kernel(x, c, norm_g, w_ada, b_ada, w_in, w_fmap, w_af, b_af, w_ab, b_ab, gla_norm_g, w_out, final_g):
    depth = w_in.shape[0]
    b, seq, d = x.shape
    assert d == D_MODEL and seq % (2 * GLA_BLOCK) == 0 and seq % (FFT_RADIX * FFT_ROWS) == 0

    mod = _ada_mod(c, w_ada, b_ada).reshape(depth, b, 3, d)
    ab = _fold_fourier_weights(w_fmap, seq)
    w_in_t = jnp.swapaxes(w_in, 1, 2)
    w_all, w_gate = _prep_weights(w_in_t)

    zeros = jnp.zeros_like(w_af)
    wc = jnp.concatenate([jnp.concatenate([w_af, zeros], axis=1),
                          jnp.concatenate([zeros, w_ab], axis=1)], axis=-1)
    bc = jnp.concatenate([b_af, b_ab], axis=-1)

    def per_head(a):
        lead = a.shape[:-1]
        a = a.reshape(lead + (2, N_GLA_HEADS, GLA_HEAD_K))
        return jnp.swapaxes(a, -3, -2).reshape(lead + (2 * D_GLA_KEY,))

    wc = per_head(wc)
    bc = per_head(bc).reshape(depth, 1, 2 * D_GLA_KEY)

    lm = _fft_matrices(seq)
    tri = _block_tri()
    norm_g3 = norm_g.reshape(depth, 1, d)
    gla_norm_g3 = gla_norm_g.reshape(depth, 1, D_GLA)
    final_g2 = final_g.reshape(1, d)

    for l in range(depth):
        p, q, z, qk, v, r, la = _inproj(x, mod, norm_g3, w_all, ab, w_gate, wc, bc, layer=l, tm=1024)
        y_f = _seqfft(lm, p, q)
        y_g = _gla(tri, qk, v, la, gla_norm_g3, layer=l)
        x = _outproj(x, y_f, z, y_g, r, w_out, mod, final_g2, layer=l, tm=1024,
                     final=(l == depth - 1))
    return x
```

```python
import functools

import numpy as np
import jax
import jax.numpy as jnp
from jax import lax
from jax.experimental import pallas as pl
from jax.experimental.pallas import tpu as pltpu

D_MODEL = 1024
D_FOURIER = 1024
N_FOURIER_GROUPS = 4
FOURIER_GROUP = 256
N_GLA_HEADS = 4
D_GLA_KEY = 512
D_GLA = 1024
GLA_HEAD_K = 128
GLA_HEAD_V = 256
GATE_RANK = 16
GATE_LOGIT_NORMALIZER = 16.0
CHUNK = 64
EPS = 1e-6
LOG2_E = 1.4426950408889634

_OFF_Z = D_FOURIER
_OFF_QK = 2 * D_FOURIER
_OFF_V = _OFF_QK + 2 * D_GLA_KEY
_OFF_R = _OFF_V + D_GLA
D_MAIN = _OFF_R + D_GLA

GLA_BLOCK = 256
CHUNKS_PER_BLOCK = GLA_BLOCK // CHUNK
GLA_HEADS_PER_STEP = 2

FFT_RADIX = 8
FFT_ROWS = 16
FFT_HALVES = 2
LANES = 128
OUTPROJ_K_CHUNK = 256
GATE_COLS = 256

VMEM_LIMIT_BYTES = 56 * 1024 * 1024

_f32 = jnp.float32
_bf16 = jnp.bfloat16


def _dot(a, b):
    return jnp.dot(a, b, preferred_element_type=_f32)


def _split_bf16(a):
    hi = a.astype(_bf16)
    lo = (a - hi.astype(_f32)).astype(_bf16)
    return hi, lo


def _silu(v):
    return v * (1.0 / (1.0 + jnp.exp(-v)))


def _params():
    return pltpu.CompilerParams(dimension_semantics=("arbitrary", "arbitrary"),
                                vmem_limit_bytes=VMEM_LIMIT_BYTES)


def _ada_kernel(c_ref, w_ref, b_ref, o_ref):
    c_act = _silu(c_ref[...]).astype(_bf16)
    o_ref[0] = _dot(c_act, w_ref[0].astype(_bf16)) + b_ref[0]


def _ada_mod(c, w_ada, b_ada):
    depth = w_ada.shape[0]
    b = c.shape[0]
    n_tiles = 3
    return pl.pallas_call(
        _ada_kernel,
        grid=(depth, n_tiles),
        in_specs=[
            pl.BlockSpec((b, D_MODEL), lambda l, j: (0, 0)),
            pl.BlockSpec((1, D_MODEL, D_MODEL), lambda l, j: (l, 0, j)),
            pl.BlockSpec((1, 1, D_MODEL), lambda l, j: (l, 0, j)),
        ],
        out_specs=pl.BlockSpec((1, b, D_MODEL), lambda l, j: (l, 0, j)),
        out_shape=jax.ShapeDtypeStruct((depth, b, 3 * D_MODEL), _f32),
        compiler_params=_params(),
        name="ada_mod",
    )(c, w_ada, b_ada.reshape(depth, 1, 3 * D_MODEL))


def _fold_kernel(wf_ref, cc_ref, sc_ref, ab_ref, *, norm):
    hp = lax.Precision.HIGHEST
    for g in range(N_FOURIER_GROUPS):
        wf = wf_ref[0, g]
        a = jnp.dot(cc_ref[...], wf, precision=hp, preferred_element_type=_f32) * norm
        bm = jnp.dot(sc_ref[...], wf, precision=hp, preferred_element_type=_f32) * norm
        ab_ref[0, g] = jnp.concatenate([a, bm], axis=1).astype(_bf16)


def _fold_fourier_weights(w_fmap, seq_len):
    depth = w_fmap.shape[0]
    idx = np.arange(FOURIER_GROUP)
    ang = 2.0 * np.pi * ((idx[:, None] * idx[None, :]) % FOURIER_GROUP) / FOURIER_GROUP
    cc = jnp.asarray(np.cos(ang), _f32)
    sc = jnp.asarray(np.sin(ang), _f32)
    norm = float(1.0 / np.sqrt(seq_len * FOURIER_GROUP))
    ng = N_FOURIER_GROUPS
    gspec = pl.BlockSpec((FOURIER_GROUP, FOURIER_GROUP), lambda l: (0, 0))
    return pl.pallas_call(
        functools.partial(_fold_kernel, norm=norm),
        grid=(depth,),
        in_specs=[
            pl.BlockSpec((1, ng, FOURIER_GROUP, FOURIER_GROUP), lambda l: (l, 0, 0, 0)),
            gspec, gspec,
        ],
        out_specs=pl.BlockSpec((1, ng, FOURIER_GROUP, 2 * FOURIER_GROUP), lambda l: (l, 0, 0, 0)),
        out_shape=jax.ShapeDtypeStruct((depth, ng, FOURIER_GROUP, 2 * FOURIER_GROUP), _bf16),
        compiler_params=pltpu.CompilerParams(dimension_semantics=("arbitrary",),
                                             vmem_limit_bytes=VMEM_LIMIT_BYTES),
        name="fold_fourier",
    )(w_fmap, cc, sc)


PREP_ROWS = 1024


def _prep_main_kernel(wt_ref, o_ref):
    o_ref[0] = wt_ref[0].T.astype(_bf16)


def _prep_gate_kernel(wt_ref, o_ref):
    w = wt_ref[0]
    o_ref[0] = jnp.concatenate([w, w, w, jnp.zeros_like(w)], axis=0).T


def _prep_weights(w_in_t):
    depth, _, d = w_in_t.shape
    rank2 = 2 * GATE_RANK
    w_all = pl.pallas_call(
        _prep_main_kernel,
        grid=(depth, D_MAIN // PREP_ROWS),
        in_specs=[pl.BlockSpec((1, PREP_ROWS, d), lambda l, j: (l, j, 0))],
        out_specs=pl.BlockSpec((1, d, PREP_ROWS), lambda l, j: (l, 0, j)),
        out_shape=jax.ShapeDtypeStruct((depth, d, D_MAIN), _bf16),
        compiler_params=_params(),
        name="prep_main",
    )(w_in_t)
    w_gate = pl.pallas_call(
        _prep_gate_kernel,
        grid=(depth,),
        in_specs=[pl.BlockSpec((1, rank2, d), lambda l: (l, D_MAIN // rank2, 0))],
        out_specs=pl.BlockSpec((1, d, 4 * rank2), lambda l: (l, 0, 0)),
        out_shape=jax.ShapeDtypeStruct((depth, d, 4 * rank2), _f32),
        compiler_params=pltpu.CompilerParams(dimension_semantics=("arbitrary",),
                                             vmem_limit_bytes=VMEM_LIMIT_BYTES),
        name="prep_gate",
    )(w_in_t)
    return w_all, w_gate


def _inproj_kernel(x_ref, mod_ref, g_ref, w_ref, ab_ref, wg_ref, wc_ref, bc_ref,
                   p_ref, q_ref, z_ref, qk_ref, v_ref, r_ref, la_ref, u_sc):
    mod = mod_ref[0, 0]
    shift = mod[0:1, :]
    gain = g_ref[0] * (1.0 + mod[1:2, :])

    def normed(rows):
        x = x_ref[0, rows, :]
        ms = jnp.mean(x * x, axis=-1, keepdims=True)
        return (x * lax.rsqrt(ms + EPS) * gain + shift).astype(_bf16)

    rank2 = 2 * GATE_RANK
    half = x_ref.shape[1] // 2
    w_gate = wg_ref[0].astype(_bf16)
    halves, g3 = [], []
    for r0 in (0, half):
        h_half = normed(slice(r0, r0 + half))
        halves.append(h_half)
        g3.append(_dot(h_half, w_gate))
        u_f32 = _dot(h_half, w_ref[0, :, 0:_OFF_Z])
        for s in range(u_sc.shape[0]):
            u_sc[s, r0:r0 + half, :] = u_f32[:, s * LANES:(s + 1) * LANES]
    h = jnp.concatenate(halves, axis=0)
    g3 = jnp.concatenate(g3, axis=0)
    z_ref[0] = _dot(h, w_ref[0, :, _OFF_Z:_OFF_QK]).astype(_bf16)

    g_hi, g_lo = _split_bf16(g3)
    lane = lax.broadcasted_iota(jnp.int32, g3.shape, 1)
    lhs = jnp.where((lane >= rank2) & (lane < 2 * rank2), g_lo, g_hi)
    w_hi, w_lo = _split_bf16(wc_ref[0])
    rhs = jnp.concatenate([w_hi, w_hi, w_lo, jnp.zeros_like(w_hi)], axis=0)

    def log2_decay(k):
        cols = slice(k * GATE_COLS, (k + 1) * GATE_COLS)
        x_g = _dot(lhs, rhs[:, cols]) + bc_ref[0][:, cols]
        soft = jnp.log2(1.0 + jnp.exp2(jnp.abs(x_g) * -LOG2_E))
        la_ref[0, :, cols] = (jnp.minimum(x_g, 0.0) * (LOG2_E / GATE_LOGIT_NORMALIZER)
                              - soft * (1.0 / GATE_LOGIT_NORMALIZER)).astype(_bf16)

    log2_decay(0)
    qk_ref[0] = _dot(h, w_ref[0, :, _OFF_QK:_OFF_V]).astype(_bf16)
    log2_decay(1)
    sub_rows = u_sc.shape[1] // FFT_RADIX
    slabs_per_group = FOURIER_GROUP // LANES
    for g in range(N_FOURIER_GROUPS):
        u_g = jnp.concatenate(
            [jnp.concatenate([u_sc[g * slabs_per_group + s, pl.ds(n1, sub_rows, stride=FFT_RADIX), :]
                              for s in range(slabs_per_group)], axis=1)
             for n1 in range(FFT_RADIX)], axis=0).astype(_bf16)
        pq = _dot(u_g, ab_ref[0, g])
        for n1 in range(FFT_RADIX):
            rows = slice(n1 * sub_rows, (n1 + 1) * sub_rows)
            cols = slice(n1 * FOURIER_GROUP, (n1 + 1) * FOURIER_GROUP)
            p_ref[0, g, :, cols] = pq[rows, :FOURIER_GROUP].astype(_bf16)
            q_ref[0, g, :, cols] = pq[rows, FOURIER_GROUP:].astype(_bf16)
    log2_decay(2)
    v_ref[0] = _dot(h, w_ref[0, :, _OFF_V:_OFF_R]).astype(_bf16)
    log2_decay(3)
    r_ref[0] = _dot(h, w_ref[0, :, _OFF_R:D_MAIN]).astype(_bf16)


def _inproj(x, mod, norm_g3, w_all, ab, w_gate, wc, bc, layer, tm):
    b, seq, d = x.shape
    row = lambda i, j: (i, j, 0)
    lay3 = lambda i, j: (layer, 0, 0)
    lay4 = lambda i, j: (layer, 0, 0, 0)
    ng = N_FOURIER_GROUPS

    def out(width, dtype=_bf16):
        return (pl.BlockSpec((1, tm, width), row),
                jax.ShapeDtypeStruct((b, seq, width), dtype))

    def grouped():
        width = FFT_RADIX * FOURIER_GROUP
        return (pl.BlockSpec((1, ng, tm // FFT_RADIX, width), lambda i, j: (i, 0, j, 0)),
                jax.ShapeDtypeStruct((b, ng, seq // FFT_RADIX, width), _bf16))

    outs = [grouped(), grouped(), out(1024), out(1024), out(1024), out(1024), out(1024)]
    return pl.pallas_call(
        _inproj_kernel,
        grid=(b, seq // tm),
        in_specs=[
            pl.BlockSpec((1, tm, d), row),
            pl.BlockSpec((1, 1, 3, d), lambda i, j: (layer, i, 0, 0)),
            pl.BlockSpec((1, 1, d), lay3),
            pl.BlockSpec((1, d, D_MAIN), lay3, pipeline_mode=pl.Buffered(1)),
            pl.BlockSpec((1, ng, FOURIER_GROUP, 2 * FOURIER_GROUP), lay4),
            pl.BlockSpec((1, d, 8 * GATE_RANK), lay3),
            pl.BlockSpec((1, 2 * GATE_RANK, 2 * D_GLA_KEY), lay3),
            pl.BlockSpec((1, 1, 2 * D_GLA_KEY), lay3),
        ],
        out_specs=[o[0] for o in outs],
        out_shape=[o[1] for o in outs],
        scratch_shapes=[pltpu.VMEM((D_FOURIER // LANES, tm, LANES), _f32)],
        compiler_params=_params(),
        name="inproj",
    )(x, mod, norm_g3, w_all, ab, w_gate, wc, bc)


def _fft_kernel(lm_ref, p_ref, q_ref, o_ref, b_ref):
    ng = p_ref.shape[1]
    sub = p_ref.shape[2]
    grp = o_ref.shape[2] // ng
    half = sub // FFT_HALVES
    for g in range(ng):
        for h in range(FFT_HALVES):
            for n1 in range(FFT_RADIX):
                cols = slice(n1 * grp, (n1 + 1) * grp)
                b_ref[g, n1, h] = (_dot(lm_ref[n1, h, :, 0:sub], p_ref[0, g, :, cols])
                                   + _dot(lm_ref[n1, h, :, sub:2 * sub], q_ref[0, g, :, cols]))

    rt = np.float32(np.sqrt(0.5))

    def combine(g, h, c):
        r0 = c * FFT_ROWS
        for t in range(grp // LANES):
            lanes = slice(t * LANES, (t + 1) * LANES)
            out_lanes = slice(g * grp + t * LANES, g * grp + (t + 1) * LANES)
            a = [b_ref[g, n, h, r0:r0 + FFT_ROWS, lanes] for n in range(FFT_RADIX)]
            bi = [b_ref[g, n, h, half + r0:half + r0 + FFT_ROWS, lanes] for n in range(FFT_RADIX)]
            e0, e1 = a[0] + a[4], a[0] - a[4]
            p1, p2, p3 = a[1] + a[7], a[2] + a[6], a[3] + a[5]
            s13, rd, t0 = p1 + p3, rt * (p1 - p3), e0 + p2
            c0, c4, c2, c1, c3 = t0 + s13, t0 - s13, e0 - p2, e1 + rd, e1 - rd
            m1, m2, m3 = bi[1] - bi[7], bi[2] - bi[6], bi[3] - bi[5]
            rs = rt * (m1 + m3)
            s1, s3, s2 = m2 + rs, rs - m2, m1 - m3
            ys = (c0, c1 - s1, c2 - s2, c3 - s3, c4, c3 + s3, c2 + s2, c1 + s1)
            for k1, yk in enumerate(ys):
                row = k1 * sub + h * half + r0
                o_ref[0, row:row + FFT_ROWS, out_lanes] = yk.astype(_bf16)

    for g in range(ng):
        for h in range(FFT_HALVES):
            for c in range(half // FFT_ROWS):
                combine(g, h, c)


def _fft_matrices(seq):
    sub = seq // FFT_RADIX
    half = sub // FFT_HALVES
    n2 = np.arange(sub)[None, :]
    mats = []
    for n1 in range(FFT_RADIX):
        halves = []
        for h in range(FFT_HALVES):
            k2 = np.arange(h * half, (h + 1) * half)[:, None]
            ang = 2.0 * np.pi * ((k2 * (n1 + FFT_RADIX * n2)) % seq) / seq
            c, s = np.cos(ang), np.sin(ang)
            halves.append(np.block([[c, -s], [s, c]]))
        mats.append(np.stack(halves))
    return jnp.asarray(np.stack(mats), _f32).astype(_bf16)


def _seqfft(lm, p, q):
    b, ng, sub, width = p.shape
    grp = width // FFT_RADIX
    seq = sub * FFT_RADIX
    dec = pl.BlockSpec((1, ng, sub, FFT_RADIX * grp), lambda i: (i, 0, 0, 0))
    return pl.pallas_call(
        _fft_kernel,
        grid=(b,),
        in_specs=[
            pl.BlockSpec((FFT_RADIX, FFT_HALVES, 2 * sub // FFT_HALVES, 2 * sub),
                         lambda i: (0, 0, 0, 0), pipeline_mode=pl.Buffered(1)),
            dec, dec,
        ],
        out_specs=pl.BlockSpec((1, seq, ng * grp), lambda i: (i, 0, 0)),
        out_shape=jax.ShapeDtypeStruct((b, seq, ng * grp), _bf16),
        scratch_shapes=[pltpu.VMEM((ng, FFT_RADIX, FFT_HALVES, 2 * sub // FFT_HALVES, grp), _f32)],
        compiler_params=pltpu.CompilerParams(dimension_semantics=("arbitrary",),
                                             vmem_limit_bytes=VMEM_LIMIT_BYTES),
        name="seqfft",
    )(lm, p, q)


def _gla_kernel(tri_ref, q_ref, k_ref, v_ref, la_ref, o_ref,
                oacc_ref, u_ref, qd_ref, dec_ref, *, seq, heads):
    n_blocks = seq // GLA_BLOCK
    nb = CHUNKS_PER_BLOCK
    dk = GLA_HEAD_K
    scale = dk ** -0.5

    def chunked(a):
        return a.reshape(nb, CHUNK, a.shape[-1])

    def flat(a):
        return a.reshape(a.shape[0] * a.shape[1], a.shape[2])

    lane = lax.broadcasted_iota(jnp.int32, (CHUNK, 2 * CHUNK), 1)
    row_l = lax.broadcasted_iota(jnp.int32, (CHUNK, 2 * CHUNK), 0)
    keep = (((lane >= CHUNK) | (lane <= row_l), (lane < CHUNK) | (lane - CHUNK <= row_l)),
            ((lane >= CHUNK) | (lane > row_l), (lane < CHUNK) | (lane - CHUNK > row_l)))

    dv = GLA_HEAD_V

    def prefix(hh, blk):
        rows = pl.ds(blk * GLA_BLOCK, GLA_BLOCK)
        la = la_ref[0, rows, hh * 2 * dk:(hh + 1) * 2 * dk]
        return hh, blk, rows, la[:, dk:].astype(_f32), _dot(tri_ref[...], la)

    def operands(pre_stage):
        hh, blk, rows, la_b, pre = pre_stage
        pre_b = pre[:, dk:]
        g_f = chunked(pre[:, :dk])
        g_b = chunked(pre_b[GLA_BLOCK - 1:GLA_BLOCK, :] - pre_b + la_b)
        zero = jnp.zeros((1, 1, dk), _f32)
        edge_f = g_f[:, CHUNK - 1:CHUNK, :]
        edge_b = g_b[:, 0:1, :]
        dirs = ((g_f, edge_f, jnp.concatenate([zero, edge_f[:nb - 1]], axis=0),
                 CHUNK // 2, list(range(nb))),
                (g_b, edge_b, jnp.concatenate([edge_b[1:], zero], axis=0),
                 CHUNK // 2 - 1, list(range(nb - 1, -1, -1))))

        qs = chunked(q_ref[0, rows, hh * dk:(hh + 1) * dk].astype(_f32) * scale)
        kk = chunked(k_ref[0, rows, hh * dk:(hh + 1) * dk].astype(_f32))
        zeros_chunk = jnp.zeros((CHUNK, dk), _f32)
        per_dir = []
        for d, (g, edge, base, ref_i, order) in enumerate(dirs):
            tot = edge[order[-1]:order[-1] + 1]
            cum = g - base
            ref = cum[:, ref_i:ref_i + 1, :]
            a = cum - ref
            qi = qs * jnp.exp2(a)
            ki = kk * jnp.exp2(-a)
            last_ref = (edge - base) - ref
            qd_ref[hh, d, rows, :] = flat(qi * jnp.exp2(ref + base)).astype(_bf16)
            kd_t = flat(ki * jnp.exp2(last_ref + (tot - edge))).T.astype(_bf16)
            dec_ref[hh, d, blk] = jnp.broadcast_to(jnp.exp2(tot).reshape(1, dk), (dk, dk)).T
            rhs_t = []
            for i in range(nb):
                pieces = []
                for c in range(nb):
                    if c == i:
                        pieces.append(ki[c])
                    elif order.index(c) < order.index(i):
                        pieces.append(ki[c] * jnp.exp2(last_ref[c] + ref[i] + base[i] - edge[c]))
                    else:
                        pieces.append(zeros_chunk)
                rhs_t.append(jnp.concatenate(pieces, axis=0).T.astype(_bf16))
            per_dir.append((flat(qi).astype(_bf16), rhs_t, kd_t))
        return hh, blk, rows, per_dir

    def scores(ops):
        per_dir = ops[3]
        return [[_dot(qi[i * CHUNK:(i + 1) * CHUNK], rhs_t[i]) for i in range(nb)]
                for qi, rhs_t, _ in per_dir]

    def outputs(ops, mats):
        hh, blk, rows, per_dir = ops
        score_parts = []
        for d, s_rows in enumerate(mats):
            for i, s_i in enumerate(s_rows):
                tile, half = divmod(i, 2)
                lo, hi = tile * 2 * CHUNK, (tile + 1) * 2 * CHUNK
                masked = jnp.where(keep[d][half], s_i[:, lo:hi], 0.0)
                s_i = jnp.concatenate([masked, s_i[:, hi:]] if tile == 0 else [s_i[:, :lo], masked], axis=1)
                score_parts.append(s_i.astype(_bf16))
        lhs = jnp.concatenate(score_parts + [p[2] for p in per_dir], axis=0)
        res = _dot(lhs, v_ref[0, rows, hh * dv:(hh + 1) * dv])
        oacc_ref[hh, rows, :] = res[0:GLA_BLOCK] + res[GLA_BLOCK:2 * GLA_BLOCK]
        u_ref[hh, 0, blk] = res[2 * GLA_BLOCK:2 * GLA_BLOCK + dk]
        u_ref[hh, 1, blk] = res[2 * GLA_BLOCK + dk:2 * GLA_BLOCK + 2 * dk]

    def recur(hh, d, blk, state, last):
        if state is None:
            return u_ref[hh, d, blk]
        rows = pl.ds(blk * GLA_BLOCK, GLA_BLOCK)
        oacc_ref[hh, rows, :] += _dot(qd_ref[hh, d, rows, :], state.astype(_bf16))
        if last:
            return state
        dec = dec_ref[hh, d, blk]
        return jnp.concatenate([dec, dec], axis=1) * state + u_ref[hh, d, blk]

    def finish(hh, blk):
        rows = pl.ds(blk * GLA_BLOCK, GLA_BLOCK)
        o_ref[0, rows, hh * dv:(hh + 1) * dv] = oacc_ref[hh, rows, :].astype(_bf16)

    n_pairs = n_blocks // 2
    items = [(hh, t) for hh in range(heads) for t in range(n_pairs)]
    pre_q, ops_q = {}, {}
    state = {}
    for s in range(len(items) + 2):
        if s < len(items):
            hh, t = items[s]
            pre_q[s] = [prefix(hh, t), prefix(hh, n_blocks - 1 - t)]
        if 0 <= s - 1 < len(items):
            ops_q[s - 1] = [operands(p) for p in pre_q.pop(s - 1)]
        if 0 <= s - 2 < len(items):
            hh, t = items[s - 2]
            pair = ops_q.pop(s - 2)
            mats = [scores(o) for o in pair]
            for d, (o, m) in enumerate(zip(pair, mats)):
                outputs(o, m)
                state[hh, d] = recur(hh, d, o[1], state.get((hh, d)), last=False)
            if t == n_pairs - 1:
                for j in range(n_pairs, n_blocks):
                    for d, blk in enumerate((j, n_blocks - 1 - j)):
                        state[hh, d] = recur(hh, d, blk, state[hh, d], last=(j == n_blocks - 1))
                        finish(hh, blk)


def _gla(tri, qk, v, la):
    b, seq, _ = v.shape
    hp = GLA_HEADS_PER_STEP
    steps = N_GLA_HEADS // hp
    n_blocks = seq // GLA_BLOCK
    return pl.pallas_call(
        functools.partial(_gla_kernel, seq=seq, heads=hp),
        grid=(b, steps),
        in_specs=[
            pl.BlockSpec((GLA_BLOCK, GLA_BLOCK), lambda i, j: (0, 0)),
            pl.BlockSpec((1, seq, hp * GLA_HEAD_K), lambda i, j: (i, 0, j)),
            pl.BlockSpec((1, seq, hp * GLA_HEAD_K), lambda i, j: (i, 0, steps + j)),
            pl.BlockSpec((1, seq, hp * GLA_HEAD_V), lambda i, j: (i, 0, j)),
            pl.BlockSpec((1, seq, hp * 2 * GLA_HEAD_K), lambda i, j: (i, 0, j)),
        ],
        out_specs=pl.BlockSpec((1, seq, hp * GLA_HEAD_V), lambda i, j: (i, 0, j)),
        out_shape=jax.ShapeDtypeStruct((b, seq, D_GLA), _bf16),
        scratch_shapes=[
            pltpu.VMEM((hp, seq, GLA_HEAD_V), _f32),
            pltpu.VMEM((hp, 2, n_blocks, GLA_HEAD_K, GLA_HEAD_V), _f32),
            pltpu.VMEM((hp, 2, seq, GLA_HEAD_K), _bf16),
            pltpu.VMEM((hp, 2, n_blocks, GLA_HEAD_K, GLA_HEAD_K), _f32),
        ],
        compiler_params=_params(),
        name="gla",
    )(tri, qk, qk, v, la)


def _outproj_kernel(x_ref, yf_ref, z_ref, yg_ref, r_ref, gn_ref, w_ref, mod_ref, fg_ref, o_ref, *, final):
    assert OUTPROJ_K_CHUNK == GLA_HEAD_V
    y = None
    for is_gla, y_ref, gate_ref, w0 in ((False, yf_ref, z_ref, 0), (True, yg_ref, r_ref, D_FOURIER)):
        for c0 in range(0, y_ref.shape[2], OUTPROJ_K_CHUNK):
            cols = slice(c0, c0 + OUTPROJ_K_CHUNK)
            branch = y_ref[0, :, cols].astype(_f32)
            if is_gla:
                ms = jnp.mean(branch * branch, axis=-1, keepdims=True)
                branch = branch * lax.rsqrt(ms + EPS) * gn_ref[0][:, cols]
            a = (branch * _silu(gate_ref[0, :, cols].astype(_f32))).astype(_bf16)
            part = _dot(a, w_ref[0, w0 + c0:w0 + c0 + OUTPROJ_K_CHUNK, :].astype(_bf16))
            y = part if y is None else y + part
    gate = mod_ref[0, 0][2:3, :]
    xn = x_ref[0] + gate * y
    if final:
        ms = jnp.mean(xn * xn, axis=-1, keepdims=True)
        xn = xn * lax.rsqrt(ms + EPS) * fg_ref[...]
    o_ref[0] = xn


def _outproj(x, yf, z, yg, r, gla_norm_g3, w_out, mod, final_g, layer, tm, final):
    b, seq, d = x.shape
    row = lambda i, j: (i, j, 0)
    return pl.pallas_call(
        functools.partial(_outproj_kernel, final=final),
        grid=(b, seq // tm),
        in_specs=[
            pl.BlockSpec((1, tm, d), row),
            pl.BlockSpec((1, tm, D_FOURIER), row),
            pl.BlockSpec((1, tm, D_FOURIER), row),
            pl.BlockSpec((1, tm, D_GLA), row),
            pl.BlockSpec((1, tm, D_GLA), row),
            pl.BlockSpec((1, 1, D_GLA), lambda i, j: (layer, 0, 0)),
            pl.BlockSpec((1, D_FOURIER + D_GLA, d), lambda i, j: (layer, 0, 0),
                         pipeline_mode=pl.Buffered(1)),
            pl.BlockSpec((1, 1, 3, d), lambda i, j: (layer, i, 0, 0)),
            pl.BlockSpec((1, d), lambda i, j: (0, 0)),
        ],
        out_specs=pl.BlockSpec((1, tm, d), row),
        out_shape=jax.ShapeDtypeStruct((b, seq, d), _f32),
        compiler_params=_params(),
        name="outproj",
    )(x, yf, z, yg, r, gla_norm_g3, w_out, mod, final_g)


def _block_tri():
    idx = np.arange(GLA_BLOCK)
    return jnp.asarray((idx[None, :] <= idx[:, None]).astype(np.float32), _bf16)


def kernel(x, c, norm_g, w_ada, b_ada, w_in, w_fmap, w_af, b_af, w_ab, b_ab, gla_norm_g, w_out, final_g):
    depth = w_in.shape[0]
    b, seq, d = x.shape
    assert d == D_MODEL and seq % (2 * GLA_BLOCK) == 0 and seq % (FFT_RADIX * FFT_ROWS) == 0

    mod = _ada_mod(c, w_ada, b_ada).reshape(depth, b, 3, d)
    ab = _fold_fourier_weights(w_fmap, seq)
    w_in_t = jnp.swapaxes(w_in, 1, 2)
    w_all, w_gate = _prep_weights(w_in_t)

    zeros = jnp.zeros_like(w_af)
    wc = jnp.concatenate([jnp.concatenate([w_af, zeros], axis=1),
                          jnp.concatenate([zeros, w_ab], axis=1)], axis=-1)
    bc = jnp.concatenate([b_af, b_ab], axis=-1)

    def per_head(a):
        lead = a.shape[:-1]
        a = a.reshape(lead + (2, N_GLA_HEADS, GLA_HEAD_K))
        return jnp.swapaxes(a, -3, -2).reshape(lead + (2 * D_GLA_KEY,))

    wc = per_head(wc)
    bc = per_head(bc).reshape(depth, 1, 2 * D_GLA_KEY)

    lm = _fft_matrices(seq)
    tri = _block_tri()
    norm_g3 = norm_g.reshape(depth, 1, d)
    gla_norm_g3 = gla_norm_g.reshape(depth, 1, D_GLA)
    final_g2 = final_g.reshape(1, d)

    for l in range(depth):
        p, q, z, qk, v, r, la = _inproj(x, mod, norm_g3, w_all, ab, w_gate, wc, bc, layer=l, tm=1024)
        y_f = _seqfft(lm, p, q)
        y_g = _gla(tri, qk, v, la)
        x = _outproj(x, y_f, z, y_g, r, gla_norm_g3, w_out, mod, final_g2, layer=l, tm=512,
                     final=(l == depth - 1))
    return x
```

```python
import functools

import numpy as np
import jax
import jax.numpy as jnp
from jax import lax
from jax.experimental import pallas as pl
from jax.experimental.pallas import tpu as pltpu

D_MODEL = 1024
D_FOURIER = 1024
N_FOURIER_GROUPS = 4
FOURIER_GROUP = 256
N_GLA_HEADS = 4
D_GLA_KEY = 512
D_GLA = 1024
GLA_HEAD_K = 128
GLA_HEAD_V = 256
GATE_RANK = 16
GATE_LOGIT_NORMALIZER = 16.0
CHUNK = 64
EPS = 1e-6
LOG2_E = 1.4426950408889634

_OFF_Z = D_FOURIER
_OFF_QK = 2 * D_FOURIER
_OFF_V = _OFF_QK + 2 * D_GLA_KEY
_OFF_R = _OFF_V + D_GLA
D_MAIN = _OFF_R + D_GLA

GLA_BLOCK = 256
CHUNKS_PER_BLOCK = GLA_BLOCK // CHUNK
GLA_HEADS_PER_STEP = 2

FFT_RADIX = 8
FFT_ROWS = 16
FFT_HALVES = 2
LANES = 128
OUTPROJ_K_CHUNK = 256
GATE_COLS = 256

VMEM_LIMIT_BYTES = 56 * 1024 * 1024

_f32 = jnp.float32
_bf16 = jnp.bfloat16


def _dot(a, b):
    return jnp.dot(a, b, preferred_element_type=_f32)


def _split_bf16(a):
    hi = a.astype(_bf16)
    lo = (a - hi.astype(_f32)).astype(_bf16)
    return hi, lo


def _silu(v):
    return v * (1.0 / (1.0 + jnp.exp(-v)))


def _params():
    return pltpu.CompilerParams(dimension_semantics=("arbitrary", "arbitrary"),
                                vmem_limit_bytes=VMEM_LIMIT_BYTES)


def _ada_kernel(c_ref, w_ref, b_ref, o_ref):
    c_act = _silu(c_ref[...]).astype(_bf16)
    o_ref[0] = _dot(c_act, w_ref[0].astype(_bf16)) + b_ref[0]


def _ada_mod(c, w_ada, b_ada):
    depth = w_ada.shape[0]
    b = c.shape[0]
    n_tiles = 3
    return pl.pallas_call(
        _ada_kernel,
        grid=(depth, n_tiles),
        in_specs=[
            pl.BlockSpec((b, D_MODEL), lambda l, j: (0, 0)),
            pl.BlockSpec((1, D_MODEL, D_MODEL), lambda l, j: (l, 0, j)),
            pl.BlockSpec((1, 1, D_MODEL), lambda l, j: (l, 0, j)),
        ],
        out_specs=pl.BlockSpec((1, b, D_MODEL), lambda l, j: (l, 0, j)),
        out_shape=jax.ShapeDtypeStruct((depth, b, 3 * D_MODEL), _f32),
        compiler_params=_params(),
        name="ada_mod",
    )(c, w_ada, b_ada.reshape(depth, 1, 3 * D_MODEL))


def _fold_kernel(wf_ref, cc_ref, sc_ref, ab_ref, *, norm):
    hp = lax.Precision.HIGHEST
    for g in range(N_FOURIER_GROUPS):
        wf = wf_ref[0, g]
        a = jnp.dot(cc_ref[...], wf, precision=hp, preferred_element_type=_f32) * norm
        bm = jnp.dot(sc_ref[...], wf, precision=hp, preferred_element_type=_f32) * norm
        ab_ref[0, g] = jnp.concatenate([a, bm], axis=1).astype(_bf16)


def _fold_fourier_weights(w_fmap, seq_len):
    depth = w_fmap.shape[0]
    idx = np.arange(FOURIER_GROUP)
    ang = 2.0 * np.pi * ((idx[:, None] * idx[None, :]) % FOURIER_GROUP) / FOURIER_GROUP
    cc = jnp.asarray(np.cos(ang), _f32)
    sc = jnp.asarray(np.sin(ang), _f32)
    norm = float(1.0 / np.sqrt(seq_len * FOURIER_GROUP))
    ng = N_FOURIER_GROUPS
    gspec = pl.BlockSpec((FOURIER_GROUP, FOURIER_GROUP), lambda l: (0, 0))
    return pl.pallas_call(
        functools.partial(_fold_kernel, norm=norm),
        grid=(depth,),
        in_specs=[
            pl.BlockSpec((1, ng, FOURIER_GROUP, FOURIER_GROUP), lambda l: (l, 0, 0, 0)),
            gspec, gspec,
        ],
        out_specs=pl.BlockSpec((1, ng, FOURIER_GROUP, 2 * FOURIER_GROUP), lambda l: (l, 0, 0, 0)),
        out_shape=jax.ShapeDtypeStruct((depth, ng, FOURIER_GROUP, 2 * FOURIER_GROUP), _bf16),
        compiler_params=pltpu.CompilerParams(dimension_semantics=("arbitrary",),
                                             vmem_limit_bytes=VMEM_LIMIT_BYTES),
        name="fold_fourier",
    )(w_fmap, cc, sc)


PREP_ROWS = 2560


def _prep_main_kernel(wt_ref, o_ref):
    o_ref[0] = wt_ref[0].T.astype(_bf16)


def _prep_gate_kernel(wt_ref, o_ref):
    w = wt_ref[0]
    o_ref[0] = jnp.concatenate([w, w, w, jnp.zeros_like(w)], axis=0).T


def _prep_weights(w_in_t):
    depth, _, d = w_in_t.shape
    rank2 = 2 * GATE_RANK
    w_all = pl.pallas_call(
        _prep_main_kernel,
        grid=(depth, D_MAIN // PREP_ROWS),
        in_specs=[pl.BlockSpec((1, PREP_ROWS, d), lambda l, j: (l, j, 0))],
        out_specs=pl.BlockSpec((1, d, PREP_ROWS), lambda l, j: (l, 0, j)),
        out_shape=jax.ShapeDtypeStruct((depth, d, D_MAIN), _bf16),
        compiler_params=_params(),
        name="prep_main",
    )(w_in_t)
    w_gate = pl.pallas_call(
        _prep_gate_kernel,
        grid=(depth,),
        in_specs=[pl.BlockSpec((1, rank2, d), lambda l: (l, D_MAIN // rank2, 0))],
        out_specs=pl.BlockSpec((1, d, 4 * rank2), lambda l: (l, 0, 0)),
        out_shape=jax.ShapeDtypeStruct((depth, d, 4 * rank2), _f32),
        compiler_params=pltpu.CompilerParams(dimension_semantics=("arbitrary",),
                                             vmem_limit_bytes=VMEM_LIMIT_BYTES),
        name="prep_gate",
    )(w_in_t)
    return w_all, w_gate


def _inproj_kernel(x_ref, mod_ref, g_ref, w_ref, ab_ref, wg_ref, wc_ref, bc_ref,
                   p_ref, q_ref, z_ref, qk_ref, v_ref, r_ref, la_ref, u_sc):
    mod = mod_ref[0, 0]
    shift = mod[0:1, :]
    gain = g_ref[0] * (1.0 + mod[1:2, :])

    def normed(rows):
        x = x_ref[0, rows, :]
        ms = jnp.mean(x * x, axis=-1, keepdims=True)
        return (x * lax.rsqrt(ms + EPS) * gain + shift).astype(_bf16)

    rank2 = 2 * GATE_RANK
    half = x_ref.shape[1] // 2
    w_gate = wg_ref[0].astype(_bf16)
    halves, g3 = [], []
    for r0 in (0, half):
        h_half = normed(slice(r0, r0 + half))
        halves.append(h_half)
        g3.append(_dot(h_half, w_gate))
        u_f32 = _dot(h_half, w_ref[0, :, 0:_OFF_Z])
        for s in range(u_sc.shape[0]):
            u_sc[s, r0:r0 + half, :] = u_f32[:, s * LANES:(s + 1) * LANES]
    h = jnp.concatenate(halves, axis=0)
    g3 = jnp.concatenate(g3, axis=0)
    z_ref[0] = _dot(h, w_ref[0, :, _OFF_Z:_OFF_QK]).astype(_bf16)

    g_hi, g_lo = _split_bf16(g3)
    lane = lax.broadcasted_iota(jnp.int32, g3.shape, 1)
    lhs = jnp.where((lane >= rank2) & (lane < 2 * rank2), g_lo, g_hi)
    w_hi, w_lo = _split_bf16(wc_ref[0])
    rhs = jnp.concatenate([w_hi, w_hi, w_lo, jnp.zeros_like(w_hi)], axis=0)

    def log2_decay(k):
        cols = slice(k * GATE_COLS, (k + 1) * GATE_COLS)
        x_g = _dot(lhs, rhs[:, cols]) + bc_ref[0][:, cols]
        soft = jnp.log2(1.0 + jnp.exp2(jnp.abs(x_g) * -LOG2_E))
        la_ref[0, :, cols] = (jnp.minimum(x_g, 0.0) * (LOG2_E / GATE_LOGIT_NORMALIZER)
                              - soft * (1.0 / GATE_LOGIT_NORMALIZER)).astype(_bf16)

    log2_decay(0)
    qk_ref[0] = _dot(h, w_ref[0, :, _OFF_QK:_OFF_V]).astype(_bf16)
    log2_decay(1)
    sub_rows = u_sc.shape[1] // FFT_RADIX
    slabs_per_group = FOURIER_GROUP // LANES
    for g in range(N_FOURIER_GROUPS):
        u_g = jnp.concatenate(
            [jnp.concatenate([u_sc[g * slabs_per_group + s, pl.ds(n1, sub_rows, stride=FFT_RADIX), :]
                              for s in range(slabs_per_group)], axis=1)
             for n1 in range(FFT_RADIX)], axis=0).astype(_bf16)
        pq = _dot(u_g, ab_ref[0, g])
        for n1 in range(FFT_RADIX):
            rows = slice(n1 * sub_rows, (n1 + 1) * sub_rows)
            cols = slice(n1 * FOURIER_GROUP, (n1 + 1) * FOURIER_GROUP)
            p_ref[0, g, :, cols] = pq[rows, :FOURIER_GROUP].astype(_bf16)
            q_ref[0, g, :, cols] = pq[rows, FOURIER_GROUP:].astype(_bf16)
    log2_decay(2)
    v_ref[0] = _dot(h, w_ref[0, :, _OFF_V:_OFF_R]).astype(_bf16)
    log2_decay(3)
    r_ref[0] = _dot(h, w_ref[0, :, _OFF_R:D_MAIN]).astype(_bf16)


def _inproj(x, mod, norm_g3, w_all, ab, w_gate, wc, bc, layer, tm):
    b, seq, d = x.shape
    row = lambda i, j: (i, j, 0)
    lay3 = lambda i, j: (layer, 0, 0)
    lay4 = lambda i, j: (layer, 0, 0, 0)
    ng = N_FOURIER_GROUPS

    def out(width, dtype=_bf16):
        return (pl.BlockSpec((1, tm, width), row),
                jax.ShapeDtypeStruct((b, seq, width), dtype))

    def grouped():
        width = FFT_RADIX * FOURIER_GROUP
        return (pl.BlockSpec((1, ng, tm // FFT_RADIX, width), lambda i, j: (i, 0, j, 0)),
                jax.ShapeDtypeStruct((b, ng, seq // FFT_RADIX, width), _bf16))

    outs = [grouped(), grouped(), out(1024), out(1024), out(1024), out(1024), out(1024)]
    return pl.pallas_call(
        _inproj_kernel,
        grid=(b, seq // tm),
        in_specs=[
            pl.BlockSpec((1, tm, d), row),
            pl.BlockSpec((1, 1, 3, d), lambda i, j: (layer, i, 0, 0)),
            pl.BlockSpec((1, 1, d), lay3),
            pl.BlockSpec((1, d, D_MAIN), lay3, pipeline_mode=pl.Buffered(1)),
            pl.BlockSpec((1, ng, FOURIER_GROUP, 2 * FOURIER_GROUP), lay4),
            pl.BlockSpec((1, d, 8 * GATE_RANK), lay3),
            pl.BlockSpec((1, 2 * GATE_RANK, 2 * D_GLA_KEY), lay3),
            pl.BlockSpec((1, 1, 2 * D_GLA_KEY), lay3),
        ],
        out_specs=[o[0] for o in outs],
        out_shape=[o[1] for o in outs],
        scratch_shapes=[pltpu.VMEM((D_FOURIER // LANES, tm, LANES), _f32)],
        compiler_params=_params(),
        name="inproj",
    )(x, mod, norm_g3, w_all, ab, w_gate, wc, bc)


def _fft_kernel(lm_ref, p_ref, q_ref, o_ref, b_ref):
    ng = p_ref.shape[1]
    sub = p_ref.shape[2]
    grp = o_ref.shape[2] // ng
    half = sub // FFT_HALVES
    for g in range(ng):
        for h in range(FFT_HALVES):
            for n1 in range(FFT_RADIX):
                cols = slice(n1 * grp, (n1 + 1) * grp)
                b_ref[g, n1, h] = (_dot(lm_ref[n1, h, :, 0:sub], p_ref[0, g, :, cols])
                                   + _dot(lm_ref[n1, h, :, sub:2 * sub], q_ref[0, g, :, cols]))

    rt = np.float32(np.sqrt(0.5))

    def combine(g, h, c):
        r0 = c * FFT_ROWS
        for t in range(grp // LANES):
            lanes = slice(t * LANES, (t + 1) * LANES)
            out_lanes = slice(g * grp + t * LANES, g * grp + (t + 1) * LANES)
            a = [b_ref[g, n, h, r0:r0 + FFT_ROWS, lanes] for n in range(FFT_RADIX)]
            bi = [b_ref[g, n, h, half + r0:half + r0 + FFT_ROWS, lanes] for n in range(FFT_RADIX)]
            e0, e1 = a[0] + a[4], a[0] - a[4]
            p1, p2, p3 = a[1] + a[7], a[2] + a[6], a[3] + a[5]
            s13, rd, t0 = p1 + p3, rt * (p1 - p3), e0 + p2
            c0, c4, c2, c1, c3 = t0 + s13, t0 - s13, e0 - p2, e1 + rd, e1 - rd
            m1, m2, m3 = bi[1] - bi[7], bi[2] - bi[6], bi[3] - bi[5]
            rs = rt * (m1 + m3)
            s1, s3, s2 = m2 + rs, rs - m2, m1 - m3
            ys = (c0, c1 - s1, c2 - s2, c3 - s3, c4, c3 + s3, c2 + s2, c1 + s1)
            for k1, yk in enumerate(ys):
                row = k1 * sub + h * half + r0
                o_ref[0, row:row + FFT_ROWS, out_lanes] = yk.astype(_bf16)

    for g in range(ng):
        for h in range(FFT_HALVES):
            for c in range(half // FFT_ROWS):
                combine(g, h, c)


def _fft_matrices(seq):
    sub = seq // FFT_RADIX
    half = sub // FFT_HALVES
    n2 = np.arange(sub)[None, :]
    mats = []
    for n1 in range(FFT_RADIX):
        halves = []
        for h in range(FFT_HALVES):
            k2 = np.arange(h * half, (h + 1) * half)[:, None]
            ang = 2.0 * np.pi * ((k2 * (n1 + FFT_RADIX * n2)) % seq) / seq
            c, s = np.cos(ang), np.sin(ang)
            halves.append(np.block([[c, -s], [s, c]]))
        mats.append(np.stack(halves))
    return jnp.asarray(np.stack(mats), _f32).astype(_bf16)


def _seqfft(lm, p, q):
    b, ng, sub, width = p.shape
    grp = width // FFT_RADIX
    seq = sub * FFT_RADIX
    dec = pl.BlockSpec((1, ng, sub, FFT_RADIX * grp), lambda i: (i, 0, 0, 0))
    return pl.pallas_call(
        _fft_kernel,
        grid=(b,),
        in_specs=[
            pl.BlockSpec((FFT_RADIX, FFT_HALVES, 2 * sub // FFT_HALVES, 2 * sub),
                         lambda i: (0, 0, 0, 0), pipeline_mode=pl.Buffered(1)),
            dec, dec,
        ],
        out_specs=pl.BlockSpec((1, seq, ng * grp), lambda i: (i, 0, 0)),
        out_shape=jax.ShapeDtypeStruct((b, seq, ng * grp), _bf16),
        scratch_shapes=[pltpu.VMEM((ng, FFT_RADIX, FFT_HALVES, 2 * sub // FFT_HALVES, grp), _f32)],
        compiler_params=pltpu.CompilerParams(dimension_semantics=("arbitrary",),
                                             vmem_limit_bytes=VMEM_LIMIT_BYTES),
        name="seqfft",
    )(lm, p, q)


def _gla_kernel(tri_ref, q_ref, k_ref, v_ref, la_ref, o_ref,
                oacc_ref, u_ref, qd_ref, dec_ref, *, seq, heads):
    n_blocks = seq // GLA_BLOCK
    nb = CHUNKS_PER_BLOCK
    dk = GLA_HEAD_K
    scale = dk ** -0.5

    def chunked(a):
        return a.reshape(nb, CHUNK, a.shape[-1])

    def flat(a):
        return a.reshape(a.shape[0] * a.shape[1], a.shape[2])

    lane = lax.broadcasted_iota(jnp.int32, (CHUNK, 2 * CHUNK), 1)
    row_l = lax.broadcasted_iota(jnp.int32, (CHUNK, 2 * CHUNK), 0)
    keep = (((lane >= CHUNK) | (lane <= row_l), (lane < CHUNK) | (lane - CHUNK <= row_l)),
            ((lane >= CHUNK) | (lane > row_l), (lane < CHUNK) | (lane - CHUNK > row_l)))

    dv = GLA_HEAD_V

    def prefix(hh, blk):
        rows = pl.ds(blk * GLA_BLOCK, GLA_BLOCK)
        la = la_ref[0, rows, hh * 2 * dk:(hh + 1) * 2 * dk]
        return hh, blk, rows, la[:, dk:].astype(_f32), _dot(tri_ref[...], la)

    def operands(pre_stage):
        hh, blk, rows, la_b, pre = pre_stage
        pre_b = pre[:, dk:]
        g_f = chunked(pre[:, :dk])
        g_b = chunked(pre_b[GLA_BLOCK - 1:GLA_BLOCK, :] - pre_b + la_b)
        zero = jnp.zeros((1, 1, dk), _f32)
        edge_f = g_f[:, CHUNK - 1:CHUNK, :]
        edge_b = g_b[:, 0:1, :]
        dirs = ((g_f, edge_f, jnp.concatenate([zero, edge_f[:nb - 1]], axis=0),
                 CHUNK // 2, list(range(nb))),
                (g_b, edge_b, jnp.concatenate([edge_b[1:], zero], axis=0),
                 CHUNK // 2 - 1, list(range(nb - 1, -1, -1))))

        qs = chunked(q_ref[0, rows, hh * dk:(hh + 1) * dk].astype(_f32) * scale)
        kk = chunked(k_ref[0, rows, hh * dk:(hh + 1) * dk].astype(_f32))
        zeros_chunk = jnp.zeros((CHUNK, dk), _f32)
        per_dir = []
        for d, (g, edge, base, ref_i, order) in enumerate(dirs):
            tot = edge[order[-1]:order[-1] + 1]
            cum = g - base
            ref = cum[:, ref_i:ref_i + 1, :]
            a = cum - ref
            qi = qs * jnp.exp2(a)
            ki = kk * jnp.exp2(-a)
            last_ref = (edge - base) - ref
            qd_ref[hh, d, rows, :] = flat(qi * jnp.exp2(ref + base)).astype(_bf16)
            kd_t = flat(ki * jnp.exp2(last_ref + (tot - edge))).T.astype(_bf16)
            dec_ref[hh, d, blk] = jnp.broadcast_to(jnp.exp2(tot).reshape(1, dk), (dk, dk)).T
            rhs_t = []
            for i in range(nb):
                live = [c for c in range(nb) if order.index(c) <= order.index(i)]
                tiles = sorted({c // 2 for c in live})
                pieces = []
                for c in range(2 * tiles[0], 2 * tiles[-1] + 2):
                    if c == i:
                        pieces.append(ki[c])
                    elif c in live:
                        pieces.append(ki[c] * jnp.exp2(last_ref[c] + ref[i] + base[i] - edge[c]))
                    else:
                        pieces.append(zeros_chunk)
                rhs_t.append((tiles, jnp.concatenate(pieces, axis=0).T.astype(_bf16)))
            per_dir.append((flat(qi).astype(_bf16), rhs_t, kd_t))
        return hh, blk, rows, per_dir

    def scores(ops):
        per_dir = ops[3]
        return [[_dot(qi[i * CHUNK:(i + 1) * CHUNK], rhs_t[i][1]) for i in range(nb)]
                for qi, rhs_t, _ in per_dir]

    def outputs(ops, mats):
        hh, blk, rows, per_dir = ops
        tile_w = 2 * CHUNK
        zero_tile = jnp.zeros((CHUNK, tile_w), _f32)
        score_parts = []
        for d, s_rows in enumerate(mats):
            for i, s_i in enumerate(s_rows):
                tiles = per_dir[d][1][i][0]
                own, half = divmod(i, 2)
                cols = []
                for tt in range(nb // 2):
                    if tt not in tiles:
                        cols.append(zero_tile)
                        continue
                    lo = tiles.index(tt) * tile_w
                    piece = s_i[:, lo:lo + tile_w]
                    cols.append(jnp.where(keep[d][half], piece, 0.0) if tt == own else piece)
                score_parts.append(jnp.concatenate(cols, axis=1).astype(_bf16))
        lhs = jnp.concatenate(score_parts + [p[2] for p in per_dir], axis=0)
        res = _dot(lhs, v_ref[0, rows, hh * dv:(hh + 1) * dv])
        oacc_ref[hh, rows, :] = res[0:GLA_BLOCK] + res[GLA_BLOCK:2 * GLA_BLOCK]
        u_ref[hh, 0, blk] = res[2 * GLA_BLOCK:2 * GLA_BLOCK + dk]
        u_ref[hh, 1, blk] = res[2 * GLA_BLOCK + dk:2 * GLA_BLOCK + 2 * dk]

    def recur(hh, d, blk, state, last):
        if state is None:
            return u_ref[hh, d, blk]
        rows = pl.ds(blk * GLA_BLOCK, GLA_BLOCK)
        oacc_ref[hh, rows, :] += _dot(qd_ref[hh, d, rows, :], state.astype(_bf16))
        if last:
            return state
        dec = dec_ref[hh, d, blk]
        return jnp.concatenate([dec, dec], axis=1) * state + u_ref[hh, d, blk]

    def finish(hh, blk):
        rows = pl.ds(blk * GLA_BLOCK, GLA_BLOCK)
        o_ref[0, rows, hh * dv:(hh + 1) * dv] = oacc_ref[hh, rows, :].astype(_bf16)

    n_pairs = n_blocks // 2
    items = [(hh, t) for hh in range(heads) for t in range(n_pairs)]
    pre_q, ops_q = {}, {}
    state = {}
    for s in range(len(items) + 2):
        if s < len(items):
            hh, t = items[s]
            pre_q[s] = [prefix(hh, t), prefix(hh, n_blocks - 1 - t)]
        if 0 <= s - 1 < len(items):
            ops_q[s - 1] = [operands(p) for p in pre_q.pop(s - 1)]
        if 0 <= s - 2 < len(items):
            hh, t = items[s - 2]
            pair = ops_q.pop(s - 2)
            mats = [scores(o) for o in pair]
            for d, (o, m) in enumerate(zip(pair, mats)):
                outputs(o, m)
                state[hh, d] = recur(hh, d, o[1], state.get((hh, d)), last=False)
            if t == n_pairs - 1:
                for j in range(n_pairs, n_blocks):
                    for d, blk in enumerate((j, n_blocks - 1 - j)):
                        state[hh, d] = recur(hh, d, blk, state[hh, d], last=(j == n_blocks - 1))
                        finish(hh, blk)


def _gla(tri, qk, v, la):
    b, seq, _ = v.shape
    hp = GLA_HEADS_PER_STEP
    steps = N_GLA_HEADS // hp
    n_blocks = seq // GLA_BLOCK
    return pl.pallas_call(
        functools.partial(_gla_kernel, seq=seq, heads=hp),
        grid=(b, steps),
        in_specs=[
            pl.BlockSpec((GLA_BLOCK, GLA_BLOCK), lambda i, j: (0, 0)),
            pl.BlockSpec((1, seq, hp * GLA_HEAD_K), lambda i, j: (i, 0, j)),
            pl.BlockSpec((1, seq, hp * GLA_HEAD_K), lambda i, j: (i, 0, steps + j)),
            pl.BlockSpec((1, seq, hp * GLA_HEAD_V), lambda i, j: (i, 0, j)),
            pl.BlockSpec((1, seq, hp * 2 * GLA_HEAD_K), lambda i, j: (i, 0, j)),
        ],
        out_specs=pl.BlockSpec((1, seq, hp * GLA_HEAD_V), lambda i, j: (i, 0, j)),
        out_shape=jax.ShapeDtypeStruct((b, seq, D_GLA), _bf16),
        scratch_shapes=[
            pltpu.VMEM((hp, seq, GLA_HEAD_V), _f32),
            pltpu.VMEM((hp, 2, n_blocks, GLA_HEAD_K, GLA_HEAD_V), _f32),
            pltpu.VMEM((hp, 2, seq, GLA_HEAD_K), _bf16),
            pltpu.VMEM((hp, 2, n_blocks, GLA_HEAD_K, GLA_HEAD_K), _f32),
        ],
        compiler_params=_params(),
        name="gla",
    )(tri, qk, qk, v, la)


def _outproj_kernel(x_ref, yf_ref, z_ref, yg_ref, r_ref, gn_ref, w_ref, mod_ref, fg_ref, o_ref, *, final):
    assert OUTPROJ_K_CHUNK == GLA_HEAD_V
    y = None
    for is_gla, y_ref, gate_ref, w0 in ((False, yf_ref, z_ref, 0), (True, yg_ref, r_ref, D_FOURIER)):
        for c0 in range(0, y_ref.shape[2], OUTPROJ_K_CHUNK):
            cols = slice(c0, c0 + OUTPROJ_K_CHUNK)
            branch = y_ref[0, :, cols].astype(_f32)
            if is_gla:
                ms = jnp.mean(branch * branch, axis=-1, keepdims=True)
                branch = branch * lax.rsqrt(ms + EPS) * gn_ref[0][:, cols]
            a = (branch * _silu(gate_ref[0, :, cols].astype(_f32))).astype(_bf16)
            part = _dot(a, w_ref[0, w0 + c0:w0 + c0 + OUTPROJ_K_CHUNK, :].astype(_bf16))
            y = part if y is None else y + part
    gate = mod_ref[0, 0][2:3, :]
    xn = x_ref[0] + gate * y
    if final:
        ms = jnp.mean(xn * xn, axis=-1, keepdims=True)
        xn = xn * lax.rsqrt(ms + EPS) * fg_ref[...]
    o_ref[0] = xn


def _outproj(x, yf, z, yg, r, gla_norm_g3, w_out, mod, final_g, layer, tm, final):
    b, seq, d = x.shape
    row = lambda i, j: (i, j, 0)
    return pl.pallas_call(
        functools.partial(_outproj_kernel, final=final),
        grid=(b, seq // tm),
        in_specs=[
            pl.BlockSpec((1, tm, d), row),
            pl.BlockSpec((1, tm, D_FOURIER), row),
            pl.BlockSpec((1, tm, D_FOURIER), row),
            pl.BlockSpec((1, tm, D_GLA), row),
            pl.BlockSpec((1, tm, D_GLA), row),
            pl.BlockSpec((1, 1, D_GLA), lambda i, j: (layer, 0, 0)),
            pl.BlockSpec((1, D_FOURIER + D_GLA, d), lambda i, j: (layer, 0, 0),
                         pipeline_mode=pl.Buffered(1)),
            pl.BlockSpec((1, 1, 3, d), lambda i, j: (layer, i, 0, 0)),
            pl.BlockSpec((1, d), lambda i, j: (0, 0)),
        ],
        out_specs=pl.BlockSpec((1, tm, d), row),
        out_shape=jax.ShapeDtypeStruct((b, seq, d), _f32),
        compiler_params=_params(),
        name="outproj",
    )(x, yf, z, yg, r, gla_norm_g3, w_out, mod, final_g)


def _block_tri():
    idx = np.arange(GLA_BLOCK)
    return jnp.asarray((idx[None, :] <= idx[:, None]).astype(np.float32), _bf16)


def kernel(x, c, norm_g, w_ada, b_ada, w_in, w_fmap, w_af, b_af, w_ab, b_ab, gla_norm_g, w_out, final_g):
    depth = w_in.shape[0]
    b, seq, d = x.shape
    assert d == D_MODEL and seq % (2 * GLA_BLOCK) == 0 and seq % (FFT_RADIX * FFT_ROWS) == 0

    mod = _ada_mod(c, w_ada, b_ada).reshape(depth, b, 3, d)
    ab = _fold_fourier_weights(w_fmap, seq)
    w_in_t = jnp.swapaxes(w_in, 1, 2)
    w_all, w_gate = _prep_weights(w_in_t)

    zeros = jnp.zeros_like(w_af)
    wc = jnp.concatenate([jnp.concatenate([w_af, zeros], axis=1),
                          jnp.concatenate([zeros, w_ab], axis=1)], axis=-1)
    bc = jnp.concatenate([b_af, b_ab], axis=-1)

    def per_head(a):
        lead = a.shape[:-1]
        a = a.reshape(lead + (2, N_GLA_HEADS, GLA_HEAD_K))
        return jnp.swapaxes(a, -3, -2).reshape(lead + (2 * D_GLA_KEY,))

    wc = per_head(wc)
    bc = per_head(bc).reshape(depth, 1, 2 * D_GLA_KEY)

    lm = _fft_matrices(seq)
    tri = _block_tri()
    norm_g3 = norm_g.reshape(depth, 1, d)
    gla_norm_g3 = gla_norm_g.reshape(depth, 1, D_GLA)
    final_g2 = final_g.reshape(1, d)

    for l in range(depth):
        p, q, z, qk, v, r, la = _inproj(x, mod, norm_g3, w_all, ab, w_gate, wc, bc, layer=l, tm=1024)
        y_f = _seqfft(lm, p, q)
        y_g = _gla(tri, qk, v, la)
        x = _outproj(x, y_f, z, y_g, r, gla_norm_g3, w_out, mod, final_g2, layer=l, tm=512,
                     final=(l == depth - 1))
    return x
```

```python
import functools

import numpy as np
import jax
import jax.numpy as jnp
from jax import lax
from jax.experimental import pallas as pl
from jax.experimental.pallas import tpu as pltpu

D_MODEL = 1024
D_FOURIER = 1024
N_FOURIER_GROUPS = 4
FOURIER_GROUP = 256
N_GLA_HEADS = 4
D_GLA_KEY = 512
D_GLA = 1024
GLA_HEAD_K = 128
GLA_HEAD_V = 256
GATE_RANK = 16
GATE_LOGIT_NORMALIZER = 16.0
CHUNK = 64
EPS = 1e-6
LOG2_E = 1.4426950408889634

_OFF_Z = D_FOURIER
_OFF_QK = 2 * D_FOURIER
_OFF_V = _OFF_QK + 2 * D_GLA_KEY
_OFF_R = _OFF_V + D_GLA
D_MAIN = _OFF_R + D_GLA

GLA_BLOCK = 256
CHUNKS_PER_BLOCK = GLA_BLOCK // CHUNK
GLA_HEADS_PER_STEP = 2

FFT_RADIX = 8
FFT_ROWS = 16
FFT_HALVES = 2
LANES = 128
OUTPROJ_K_CHUNK = 256
GATE_COLS = 256

VMEM_LIMIT_BYTES = 56 * 1024 * 1024

_f32 = jnp.float32
_bf16 = jnp.bfloat16


def _dot(a, b):
    return jnp.dot(a, b, preferred_element_type=_f32)


def _split_bf16(a):
    hi = a.astype(_bf16)
    lo = (a - hi.astype(_f32)).astype(_bf16)
    return hi, lo


def _silu(v):
    return v * (1.0 / (1.0 + jnp.exp(-v)))


def _params():
    return pltpu.CompilerParams(dimension_semantics=("arbitrary", "arbitrary"),
                                vmem_limit_bytes=VMEM_LIMIT_BYTES)


def _ada_kernel(c_ref, w_ref, b_ref, o_ref):
    c_act = _silu(c_ref[...]).astype(_bf16)
    o_ref[0] = _dot(c_act, w_ref[0].astype(_bf16)) + b_ref[0]


def _ada_mod(c, w_ada, b_ada):
    depth = w_ada.shape[0]
    b = c.shape[0]
    n_tiles = 3
    return pl.pallas_call(
        _ada_kernel,
        grid=(depth, n_tiles),
        in_specs=[
            pl.BlockSpec((b, D_MODEL), lambda l, j: (0, 0)),
            pl.BlockSpec((1, D_MODEL, D_MODEL), lambda l, j: (l, 0, j)),
            pl.BlockSpec((1, 1, D_MODEL), lambda l, j: (l, 0, j)),
        ],
        out_specs=pl.BlockSpec((1, b, D_MODEL), lambda l, j: (l, 0, j)),
        out_shape=jax.ShapeDtypeStruct((depth, b, 3 * D_MODEL), _f32),
        compiler_params=_params(),
        name="ada_mod",
    )(c, w_ada, b_ada.reshape(depth, 1, 3 * D_MODEL))


def _fold_kernel(wf_ref, cc_ref, sc_ref, ab_ref, *, norm):
    hp = lax.Precision.HIGHEST
    for g in range(N_FOURIER_GROUPS):
        wf = wf_ref[0, g]
        a = jnp.dot(cc_ref[...], wf, precision=hp, preferred_element_type=_f32) * norm
        bm = jnp.dot(sc_ref[...], wf, precision=hp, preferred_element_type=_f32) * norm
        ab_ref[0, g] = jnp.concatenate([a, bm], axis=1).astype(_bf16)


def _fold_fourier_weights(w_fmap, seq_len):
    depth = w_fmap.shape[0]
    idx = np.arange(FOURIER_GROUP)
    ang = 2.0 * np.pi * ((idx[:, None] * idx[None, :]) % FOURIER_GROUP) / FOURIER_GROUP
    cc = jnp.asarray(np.cos(ang), _f32)
    sc = jnp.asarray(np.sin(ang), _f32)
    norm = float(1.0 / np.sqrt(seq_len * FOURIER_GROUP))
    ng = N_FOURIER_GROUPS
    gspec = pl.BlockSpec((FOURIER_GROUP, FOURIER_GROUP), lambda l: (0, 0))
    return pl.pallas_call(
        functools.partial(_fold_kernel, norm=norm),
        grid=(depth,),
        in_specs=[
            pl.BlockSpec((1, ng, FOURIER_GROUP, FOURIER_GROUP), lambda l: (l, 0, 0, 0)),
            gspec, gspec,
        ],
        out_specs=pl.BlockSpec((1, ng, FOURIER_GROUP, 2 * FOURIER_GROUP), lambda l: (l, 0, 0, 0)),
        out_shape=jax.ShapeDtypeStruct((depth, ng, FOURIER_GROUP, 2 * FOURIER_GROUP), _bf16),
        compiler_params=pltpu.CompilerParams(dimension_semantics=("arbitrary",),
                                             vmem_limit_bytes=VMEM_LIMIT_BYTES),
        name="fold_fourier",
    )(w_fmap, cc, sc)


PREP_ROWS = 2560


def _prep_main_kernel(wt_ref, o_ref):
    o_ref[0] = wt_ref[0].T.astype(_bf16)


def _prep_gate_kernel(wt_ref, o_ref):
    w = wt_ref[0]
    o_ref[0] = jnp.concatenate([w, w, w, jnp.zeros_like(w)], axis=0).T


def _prep_weights(w_in_t):
    depth, _, d = w_in_t.shape
    rank2 = 2 * GATE_RANK
    w_all = pl.pallas_call(
        _prep_main_kernel,
        grid=(depth, D_MAIN // PREP_ROWS),
        in_specs=[pl.BlockSpec((1, PREP_ROWS, d), lambda l, j: (l, j, 0))],
        out_specs=pl.BlockSpec((1, d, PREP_ROWS), lambda l, j: (l, 0, j)),
        out_shape=jax.ShapeDtypeStruct((depth, d, D_MAIN), _bf16),
        compiler_params=_params(),
        name="prep_main",
    )(w_in_t)
    w_gate = pl.pallas_call(
        _prep_gate_kernel,
        grid=(depth,),
        in_specs=[pl.BlockSpec((1, rank2, d), lambda l: (l, D_MAIN // rank2, 0))],
        out_specs=pl.BlockSpec((1, d, 4 * rank2), lambda l: (l, 0, 0)),
        out_shape=jax.ShapeDtypeStruct((depth, d, 4 * rank2), _f32),
        compiler_params=pltpu.CompilerParams(dimension_semantics=("arbitrary",),
                                             vmem_limit_bytes=VMEM_LIMIT_BYTES),
        name="prep_gate",
    )(w_in_t)
    return w_all, w_gate


def _inproj_kernel(x_ref, mod_ref, g_ref, w_ref, ab_ref, wg_ref, wc_ref, bc_ref,
                   p_ref, q_ref, z_ref, qk_ref, v_ref, r_ref, la_ref, u_sc):
    mod = mod_ref[0, 0]
    shift = mod[0:1, :]
    gain = g_ref[0] * (1.0 + mod[1:2, :])

    def normed(rows):
        x = x_ref[0, rows, :]
        ms = jnp.mean(x * x, axis=-1, keepdims=True)
        return (x * lax.rsqrt(ms + EPS) * gain + shift).astype(_bf16)

    rank2 = 2 * GATE_RANK
    half = x_ref.shape[1] // 2
    w_gate = wg_ref[0].astype(_bf16)
    halves, g3 = [], []
    for r0 in (0, half):
        h_half = normed(slice(r0, r0 + half))
        halves.append(h_half)
        g3.append(_dot(h_half, w_gate))
        u_f32 = _dot(h_half, w_ref[0, :, 0:_OFF_Z])
        for s in range(u_sc.shape[0]):
            u_sc[s, r0:r0 + half, :] = u_f32[:, s * LANES:(s + 1) * LANES]
    h = jnp.concatenate(halves, axis=0)
    g3 = jnp.concatenate(g3, axis=0)
    z_ref[0] = _dot(h, w_ref[0, :, _OFF_Z:_OFF_QK]).astype(_bf16)

    g_hi, g_lo = _split_bf16(g3)
    lane = lax.broadcasted_iota(jnp.int32, g3.shape, 1)
    lhs = jnp.where((lane >= rank2) & (lane < 2 * rank2), g_lo, g_hi)
    w_hi, w_lo = _split_bf16(wc_ref[0])
    rhs = jnp.concatenate([w_hi, w_hi, w_lo, jnp.zeros_like(w_hi)], axis=0)

    def log2_decay(k):
        cols = slice(k * GATE_COLS, (k + 1) * GATE_COLS)
        x_g = _dot(lhs, rhs[:, cols]) + bc_ref[0][:, cols]
        soft = jnp.log2(1.0 + jnp.exp2(jnp.abs(x_g) * -LOG2_E))
        la_ref[0, :, cols] = (jnp.minimum(x_g, 0.0) * (LOG2_E / GATE_LOGIT_NORMALIZER)
                              - soft * (1.0 / GATE_LOGIT_NORMALIZER)).astype(_bf16)

    log2_decay(0)
    qk_ref[0] = _dot(h, w_ref[0, :, _OFF_QK:_OFF_V]).astype(_bf16)
    log2_decay(1)
    sub_rows = u_sc.shape[1] // FFT_RADIX
    slabs_per_group = FOURIER_GROUP // LANES
    for g in range(N_FOURIER_GROUPS):
        u_g = jnp.concatenate(
            [jnp.concatenate([u_sc[g * slabs_per_group + s, pl.ds(n1, sub_rows, stride=FFT_RADIX), :]
                              for s in range(slabs_per_group)], axis=1)
             for n1 in range(FFT_RADIX)], axis=0).astype(_bf16)
        pq = _dot(u_g, ab_ref[0, g])
        for n1 in range(FFT_RADIX):
            rows = slice(n1 * sub_rows, (n1 + 1) * sub_rows)
            cols = slice(n1 * FOURIER_GROUP, (n1 + 1) * FOURIER_GROUP)
            p_ref[0, g, :, cols] = pq[rows, :FOURIER_GROUP].astype(_bf16)
            q_ref[0, g, :, cols] = pq[rows, FOURIER_GROUP:].astype(_bf16)
    log2_decay(2)
    v_ref[0] = _dot(h, w_ref[0, :, _OFF_V:_OFF_R]).astype(_bf16)
    log2_decay(3)
    r_ref[0] = _dot(h, w_ref[0, :, _OFF_R:D_MAIN]).astype(_bf16)


def _inproj(x, mod, norm_g3, w_all, ab, w_gate, wc, bc, layer, tm):
    b, seq, d = x.shape
    row = lambda i, j: (i, j, 0)
    lay3 = lambda i, j: (layer, 0, 0)
    lay4 = lambda i, j: (layer, 0, 0, 0)
    ng = N_FOURIER_GROUPS

    def out(width, dtype=_bf16):
        return (pl.BlockSpec((1, tm, width), row),
                jax.ShapeDtypeStruct((b, seq, width), dtype))

    def grouped():
        width = FFT_RADIX * FOURIER_GROUP
        return (pl.BlockSpec((1, ng, tm // FFT_RADIX, width), lambda i, j: (i, 0, j, 0)),
                jax.ShapeDtypeStruct((b, ng, seq // FFT_RADIX, width), _bf16))

    outs = [grouped(), grouped(), out(1024), out(1024), out(1024), out(1024), out(1024)]
    return pl.pallas_call(
        _inproj_kernel,
        grid=(b, seq // tm),
        in_specs=[
            pl.BlockSpec((1, tm, d), row),
            pl.BlockSpec((1, 1, 3, d), lambda i, j: (layer, i, 0, 0)),
            pl.BlockSpec((1, 1, d), lay3),
            pl.BlockSpec((1, d, D_MAIN), lay3, pipeline_mode=pl.Buffered(1)),
            pl.BlockSpec((1, ng, FOURIER_GROUP, 2 * FOURIER_GROUP), lay4),
            pl.BlockSpec((1, d, 8 * GATE_RANK), lay3),
            pl.BlockSpec((1, 2 * GATE_RANK, 2 * D_GLA_KEY), lay3),
            pl.BlockSpec((1, 1, 2 * D_GLA_KEY), lay3),
        ],
        out_specs=[o[0] for o in outs],
        out_shape=[o[1] for o in outs],
        scratch_shapes=[pltpu.VMEM((D_FOURIER // LANES, tm, LANES), _f32)],
        compiler_params=_params(),
        name="inproj",
    )(x, mod, norm_g3, w_all, ab, w_gate, wc, bc)


def _fft_kernel(lm_ref, p_ref, q_ref, o_ref, b_ref):
    ng = p_ref.shape[1]
    sub = p_ref.shape[2]
    grp = o_ref.shape[2] // ng
    half = sub // FFT_HALVES
    for g in range(ng):
        for h in range(FFT_HALVES):
            for n1 in range(FFT_RADIX):
                cols = slice(n1 * grp, (n1 + 1) * grp)
                b_ref[g, n1, h] = (_dot(lm_ref[n1, h, :, 0:sub], p_ref[0, g, :, cols])
                                   + _dot(lm_ref[n1, h, :, sub:2 * sub], q_ref[0, g, :, cols]))

    rt = np.float32(np.sqrt(0.5))

    def combine(g, h, c):
        r0 = c * FFT_ROWS
        for t in range(grp // LANES):
            lanes = slice(t * LANES, (t + 1) * LANES)
            out_lanes = slice(g * grp + t * LANES, g * grp + (t + 1) * LANES)
            a = [b_ref[g, n, h, r0:r0 + FFT_ROWS, lanes] for n in range(FFT_RADIX)]
            bi = [b_ref[g, n, h, half + r0:half + r0 + FFT_ROWS, lanes] for n in range(FFT_RADIX)]
            e0, e1 = a[0] + a[4], a[0] - a[4]
            p1, p2, p3 = a[1] + a[7], a[2] + a[6], a[3] + a[5]
            s13, rd, t0 = p1 + p3, rt * (p1 - p3), e0 + p2
            c0, c4, c2, c1, c3 = t0 + s13, t0 - s13, e0 - p2, e1 + rd, e1 - rd
            m1, m2, m3 = bi[1] - bi[7], bi[2] - bi[6], bi[3] - bi[5]
            rs = rt * (m1 + m3)
            s1, s3, s2 = m2 + rs, rs - m2, m1 - m3
            ys = (c0, c1 - s1, c2 - s2, c3 - s3, c4, c3 + s3, c2 + s2, c1 + s1)
            for k1, yk in enumerate(ys):
                row = k1 * sub + h * half + r0
                o_ref[0, row:row + FFT_ROWS, out_lanes] = yk.astype(_bf16)

    for g in range(ng):
        for h in range(FFT_HALVES):
            for c in range(half // FFT_ROWS):
                combine(g, h, c)


def _fft_matrices(seq):
    sub = seq // FFT_RADIX
    half = sub // FFT_HALVES
    n2 = np.arange(sub)[None, :]
    mats = []
    for n1 in range(FFT_RADIX):
        halves = []
        for h in range(FFT_HALVES):
            k2 = np.arange(h * half, (h + 1) * half)[:, None]
            ang = 2.0 * np.pi * ((k2 * (n1 + FFT_RADIX * n2)) % seq) / seq
            c, s = np.cos(ang), np.sin(ang)
            halves.append(np.block([[c, -s], [s, c]]))
        mats.append(np.stack(halves))
    return jnp.asarray(np.stack(mats), _f32).astype(_bf16)


def _seqfft(lm, p, q):
    b, ng, sub, width = p.shape
    grp = width // FFT_RADIX
    seq = sub * FFT_RADIX
    dec = pl.BlockSpec((1, ng, sub, FFT_RADIX * grp), lambda i: (i, 0, 0, 0))
    return pl.pallas_call(
        _fft_kernel,
        grid=(b,),
        in_specs=[
            pl.BlockSpec((FFT_RADIX, FFT_HALVES, 2 * sub // FFT_HALVES, 2 * sub),
                         lambda i: (0, 0, 0, 0), pipeline_mode=pl.Buffered(1)),
            dec, dec,
        ],
        out_specs=pl.BlockSpec((1, seq, ng * grp), lambda i: (i, 0, 0)),
        out_shape=jax.ShapeDtypeStruct((b, seq, ng * grp), _bf16),
        scratch_shapes=[pltpu.VMEM((ng, FFT_RADIX, FFT_HALVES, 2 * sub // FFT_HALVES, grp), _f32)],
        compiler_params=pltpu.CompilerParams(dimension_semantics=("arbitrary",),
                                             vmem_limit_bytes=VMEM_LIMIT_BYTES),
        name="seqfft",
    )(lm, p, q)


def _gla_kernel(tri_ref, q_ref, k_ref, v_ref, la_ref, o_ref,
                oacc_ref, u_ref, qd_ref, dec_ref, *, seq, heads):
    n_blocks = seq // GLA_BLOCK
    nb = CHUNKS_PER_BLOCK
    dk = GLA_HEAD_K
    scale = dk ** -0.5

    def chunked(a):
        return a.reshape(nb, CHUNK, a.shape[-1])

    def flat(a):
        return a.reshape(a.shape[0] * a.shape[1], a.shape[2])

    lane = lax.broadcasted_iota(jnp.int32, (CHUNK, 2 * CHUNK), 1)
    row_l = lax.broadcasted_iota(jnp.int32, (CHUNK, 2 * CHUNK), 0)
    keep = (((lane >= CHUNK) | (lane <= row_l), (lane < CHUNK) | (lane - CHUNK <= row_l)),
            ((lane >= CHUNK) | (lane > row_l), (lane < CHUNK) | (lane - CHUNK > row_l)))

    dv = GLA_HEAD_V

    def prefix(hh, blk):
        rows = pl.ds(blk * GLA_BLOCK, GLA_BLOCK)
        la = la_ref[0, rows, hh * 2 * dk:(hh + 1) * 2 * dk]
        return hh, blk, rows, _dot(tri_ref[...], la)

    def operands(pre_stage):
        hh, blk, rows, pre = pre_stage
        la_b = la_ref[0, rows, hh * 2 * dk + dk:(hh + 1) * 2 * dk].astype(_f32)
        pre_b = pre[:, dk:]
        g_f = chunked(pre[:, :dk])
        g_b = chunked(pre_b[GLA_BLOCK - 1:GLA_BLOCK, :] - pre_b + la_b)
        zero = jnp.zeros((1, 1, dk), _f32)
        edge_f = g_f[:, CHUNK - 1:CHUNK, :]
        edge_b = g_b[:, 0:1, :]
        dirs = ((g_f, edge_f, jnp.concatenate([zero, edge_f[:nb - 1]], axis=0),
                 CHUNK // 2, list(range(nb))),
                (g_b, edge_b, jnp.concatenate([edge_b[1:], zero], axis=0),
                 CHUNK // 2 - 1, list(range(nb - 1, -1, -1))))

        qs = chunked(q_ref[0, rows, hh * dk:(hh + 1) * dk].astype(_f32) * scale)
        kk = chunked(k_ref[0, rows, hh * dk:(hh + 1) * dk].astype(_f32))
        zeros_chunk = jnp.zeros((CHUNK, dk), _f32)
        per_dir = []
        for d, (g, edge, base, ref_i, order) in enumerate(dirs):
            tot = edge[order[-1]:order[-1] + 1]
            cum = g - base
            ref = cum[:, ref_i:ref_i + 1, :]
            a = cum - ref
            qi = qs * jnp.exp2(a)
            ki = kk * jnp.exp2(-a)
            last_ref = (edge - base) - ref
            qd_ref[hh, d, rows, :] = flat(qi * jnp.exp2(ref + base)).astype(_bf16)
            kd_t = flat(ki * jnp.exp2(last_ref + (tot - edge))).T.astype(_bf16)
            dec_ref[hh, d, blk] = jnp.broadcast_to(jnp.exp2(tot).reshape(1, dk), (dk, dk)).T
            rhs_t = []
            for i in range(nb):
                live = [c for c in range(nb) if order.index(c) <= order.index(i)]
                tiles = sorted({c // 2 for c in live})
                pieces = []
                for c in range(2 * tiles[0], 2 * tiles[-1] + 2):
                    if c == i:
                        pieces.append(ki[c])
                    elif c in live:
                        pieces.append(ki[c] * jnp.exp2(last_ref[c] + ref[i] + base[i] - edge[c]))
                    else:
                        pieces.append(zeros_chunk)
                rhs_t.append((tiles, jnp.concatenate(pieces, axis=0).T.astype(_bf16)))
            per_dir.append((flat(qi).astype(_bf16), rhs_t, kd_t))
        return hh, blk, rows, per_dir

    def scores(ops):
        per_dir = ops[3]
        return [[_dot(qi[i * CHUNK:(i + 1) * CHUNK], rhs_t[i][1]) for i in range(nb)]
                for qi, rhs_t, _ in per_dir]

    def outputs(ops, mats):
        hh, blk, rows, per_dir = ops
        tile_w = 2 * CHUNK
        zero_tile = jnp.zeros((CHUNK, tile_w), _f32)
        score_parts = []
        for d, s_rows in enumerate(mats):
            for i, s_i in enumerate(s_rows):
                tiles = per_dir[d][1][i][0]
                own, half = divmod(i, 2)
                cols = []
                for tt in range(nb // 2):
                    if tt not in tiles:
                        cols.append(zero_tile)
                        continue
                    lo = tiles.index(tt) * tile_w
                    piece = s_i[:, lo:lo + tile_w]
                    cols.append(jnp.where(keep[d][half], piece, 0.0) if tt == own else piece)
                score_parts.append(jnp.concatenate(cols, axis=1).astype(_bf16))
        lhs = jnp.concatenate(score_parts + [p[2] for p in per_dir], axis=0)
        res = _dot(lhs, v_ref[0, rows, hh * dv:(hh + 1) * dv])
        oacc_ref[hh, rows, :] = res[0:GLA_BLOCK] + res[GLA_BLOCK:2 * GLA_BLOCK]
        u_ref[hh, 0, blk] = res[2 * GLA_BLOCK:2 * GLA_BLOCK + dk]
        u_ref[hh, 1, blk] = res[2 * GLA_BLOCK + dk:2 * GLA_BLOCK + 2 * dk]

    def recur(hh, d, blk, state, last):
        if state is None:
            return u_ref[hh, d, blk]
        rows = pl.ds(blk * GLA_BLOCK, GLA_BLOCK)
        oacc_ref[hh, rows, :] += _dot(qd_ref[hh, d, rows, :], state.astype(_bf16))
        if last:
            return state
        dec = dec_ref[hh, d, blk]
        return jnp.concatenate([dec, dec], axis=1) * state + u_ref[hh, d, blk]

    def finish(hh, blk):
        rows = pl.ds(blk * GLA_BLOCK, GLA_BLOCK)
        o_ref[0, rows, hh * dv:(hh + 1) * dv] = oacc_ref[hh, rows, :].astype(_bf16)

    n_pairs = n_blocks // 2
    items = [(hh, t) for hh in range(heads) for t in range(n_pairs)]
    pre_q, ops_q = {}, {}
    state = {}
    for s in range(len(items) + 2):
        if s < len(items):
            hh, t = items[s]
            pre_q[s] = [prefix(hh, t), prefix(hh, n_blocks - 1 - t)]
        if 0 <= s - 1 < len(items):
            ops_q[s - 1] = [operands(p) for p in pre_q.pop(s - 1)]
        if 0 <= s - 2 < len(items):
            hh, t = items[s - 2]
            pair = ops_q.pop(s - 2)
            mats = [scores(o) for o in pair]
            for d, (o, m) in enumerate(zip(pair, mats)):
                outputs(o, m)
                state[hh, d] = recur(hh, d, o[1], state.get((hh, d)), last=False)
            if t == n_pairs - 1:
                for j in range(n_pairs, n_blocks):
                    for d, blk in enumerate((j, n_blocks - 1 - j)):
                        state[hh, d] = recur(hh, d, blk, state[hh, d], last=(j == n_blocks - 1))
                        finish(hh, blk)


def _gla(tri, qk, v, la):
    b, seq, _ = v.shape
    hp = GLA_HEADS_PER_STEP
    steps = N_GLA_HEADS // hp
    n_blocks = seq // GLA_BLOCK
    return pl.pallas_call(
        functools.partial(_gla_kernel, seq=seq, heads=hp),
        grid=(b, steps),
        in_specs=[
            pl.BlockSpec((GLA_BLOCK, GLA_BLOCK), lambda i, j: (0, 0)),
            pl.BlockSpec((1, seq, hp * GLA_HEAD_K), lambda i, j: (i, 0, j)),
            pl.BlockSpec((1, seq, hp * GLA_HEAD_K), lambda i, j: (i, 0, steps + j)),
            pl.BlockSpec((1, seq, hp * GLA_HEAD_V), lambda i, j: (i, 0, j)),
            pl.BlockSpec((1, seq, hp * 2 * GLA_HEAD_K), lambda i, j: (i, 0, j)),
        ],
        out_specs=pl.BlockSpec((1, seq, hp * GLA_HEAD_V), lambda i, j: (i, 0, j)),
        out_shape=jax.ShapeDtypeStruct((b, seq, D_GLA), _bf16),
        scratch_shapes=[
            pltpu.VMEM((hp, seq, GLA_HEAD_V), _f32),
            pltpu.VMEM((hp, 2, n_blocks, GLA_HEAD_K, GLA_HEAD_V), _f32),
            pltpu.VMEM((hp, 2, seq, GLA_HEAD_K), _bf16),
            pltpu.VMEM((hp, 2, n_blocks, GLA_HEAD_K, GLA_HEAD_K), _f32),
        ],
        compiler_params=_params(),
        name="gla",
    )(tri, qk, qk, v, la)


def _outproj_kernel(x_ref, yf_ref, z_ref, yg_ref, r_ref, gn_ref, w_ref, mod_ref, fg_ref, o_ref, *, final):
    assert OUTPROJ_K_CHUNK == GLA_HEAD_V
    y = None
    for is_gla, y_ref, gate_ref, w0 in ((False, yf_ref, z_ref, 0), (True, yg_ref, r_ref, D_FOURIER)):
        for c0 in range(0, y_ref.shape[2], OUTPROJ_K_CHUNK):
            cols = slice(c0, c0 + OUTPROJ_K_CHUNK)
            branch = y_ref[0, :, cols].astype(_f32)
            if is_gla:
                ms = jnp.mean(branch * branch, axis=-1, keepdims=True)
                branch = branch * lax.rsqrt(ms + EPS) * gn_ref[0][:, cols]
            a = (branch * _silu(gate_ref[0, :, cols].astype(_f32))).astype(_bf16)
            part = _dot(a, w_ref[0, w0 + c0:w0 + c0 + OUTPROJ_K_CHUNK, :].astype(_bf16))
            y = part if y is None else y + part
    gate = mod_ref[0, 0][2:3, :]
    xn = x_ref[0] + gate * y
    if final:
        ms = jnp.mean(xn * xn, axis=-1, keepdims=True)
        xn = xn * lax.rsqrt(ms + EPS) * fg_ref[...]
    o_ref[0] = xn


def _outproj(x, yf, z, yg, r, gla_norm_g3, w_out, mod, final_g, layer, tm, final):
    b, seq, d = x.shape
    row = lambda i, j: (i, j, 0)
    return pl.pallas_call(
        functools.partial(_outproj_kernel, final=final),
        grid=(b, seq // tm),
        in_specs=[
            pl.BlockSpec((1, tm, d), row),
            pl.BlockSpec((1, tm, D_FOURIER), row),
            pl.BlockSpec((1, tm, D_FOURIER), row),
            pl.BlockSpec((1, tm, D_GLA), row),
            pl.BlockSpec((1, tm, D_GLA), row),
            pl.BlockSpec((1, 1, D_GLA), lambda i, j: (layer, 0, 0)),
            pl.BlockSpec((1, D_FOURIER + D_GLA, d), lambda i, j: (layer, 0, 0),
                         pipeline_mode=pl.Buffered(1)),
            pl.BlockSpec((1, 1, 3, d), lambda i, j: (layer, i, 0, 0)),
            pl.BlockSpec((1, d), lambda i, j: (0, 0)),
        ],
        out_specs=pl.BlockSpec((1, tm, d), row),
        out_shape=jax.ShapeDtypeStruct((b, seq, d), _f32),
        compiler_params=_params(),
        name="outproj",
    )(x, yf, z, yg, r, gla_norm_g3, w_out, mod, final_g)


def _block_tri():
    idx = np.arange(GLA_BLOCK)
    return jnp.asarray((idx[None, :] <= idx[:, None]).astype(np.float32), _bf16)


def kernel(x, c, norm_g, w_ada, b_ada, w_in, w_fmap, w_af, b_af, w_ab, b_ab, gla_norm_g, w_out, final_g):
    depth = w_in.shape[0]
    b, seq, d = x.shape
    assert d == D_MODEL and seq % (2 * GLA_BLOCK) == 0 and seq % (FFT_RADIX * FFT_ROWS) == 0

    mod = _ada_mod(c, w_ada, b_ada).reshape(depth, b, 3, d)
    ab = _fold_fourier_weights(w_fmap, seq)
    w_in_t = jnp.swapaxes(w_in, 1, 2)
    w_all, w_gate = _prep_weights(w_in_t)

    zeros = jnp.zeros_like(w_af)
    wc = jnp.concatenate([jnp.concatenate([w_af, zeros], axis=1),
                          jnp.concatenate([zeros, w_ab], axis=1)], axis=-1)
    bc = jnp.concatenate([b_af, b_ab], axis=-1)

    def per_head(a):
        lead = a.shape[:-1]
        a = a.reshape(lead + (2, N_GLA_HEADS, GLA_HEAD_K))
        return jnp.swapaxes(a, -3, -2).reshape(lead + (2 * D_GLA_KEY,))

    wc = per_head(wc)
    bc = per_head(bc).reshape(depth, 1, 2 * D_GLA_KEY)

    lm = _fft_matrices(seq)
    tri = _block_tri()
    norm_g3 = norm_g.reshape(depth, 1, d)
    gla_norm_g3 = gla_norm_g.reshape(depth, 1, D_GLA)
    final_g2 = final_g.reshape(1, d)

    for l in range(depth):
        p, q, z, qk, v, r, la = _inproj(x, mod, norm_g3, w_all, ab, w_gate, wc, bc, layer=l, tm=1024)
        y_f = _seqfft(lm, p, q)
        y_g = _gla(tri, qk, v, la)
        x = _outproj(x, y_f, z, y_g, r, gla_norm_g3, w_out, mod, final_g2, layer=l, tm=512,
                     final=(l == depth - 1))
    return x
```

```python
import functools

import numpy as np
import jax
import jax.numpy as jnp
from jax import lax
from jax.experimental import pallas as pl
from jax.experimental.pallas import tpu as pltpu

D_MODEL = 1024
D_FOURIER = 1024
N_FOURIER_GROUPS = 4
FOURIER_GROUP = 256
N_GLA_HEADS = 4
D_GLA_KEY = 512
D_GLA = 1024
GLA_HEAD_K = 128
GLA_HEAD_V = 256
GATE_RANK = 16
GATE_LOGIT_NORMALIZER = 16.0
CHUNK = 64
EPS = 1e-6
LOG2_E = 1.4426950408889634

_OFF_Z = D_FOURIER
_OFF_QK = 2 * D_FOURIER
_OFF_V = _OFF_QK + 2 * D_GLA_KEY
_OFF_R = _OFF_V + D_GLA
D_MAIN = _OFF_R + D_GLA

GLA_BLOCK = 256
CHUNKS_PER_BLOCK = GLA_BLOCK // CHUNK
GLA_HEADS_PER_STEP = 2

FFT_RADIX = 8
FFT_ROWS = 16
FFT_HALVES = 2
LANES = 128
OUTPROJ_K_CHUNK = 256
GATE_COLS = 256

VMEM_LIMIT_BYTES = 56 * 1024 * 1024

_f32 = jnp.float32
_bf16 = jnp.bfloat16


def _dot(a, b):
    return jnp.dot(a, b, preferred_element_type=_f32)


def _split_bf16(a):
    hi = a.astype(_bf16)
    lo = (a - hi.astype(_f32)).astype(_bf16)
    return hi, lo


def _silu(v):
    return v * (1.0 / (1.0 + jnp.exp(-v)))


def _params():
    return pltpu.CompilerParams(dimension_semantics=("arbitrary", "arbitrary"),
                                vmem_limit_bytes=VMEM_LIMIT_BYTES)


def _ada_kernel(c_ref, w_ref, b_ref, o_ref):
    c_act = _silu(c_ref[...]).astype(_bf16)
    o_ref[0] = _dot(c_act, w_ref[0].astype(_bf16)) + b_ref[0]


def _ada_mod(c, w_ada, b_ada):
    depth = w_ada.shape[0]
    b = c.shape[0]
    n_tiles = 3
    return pl.pallas_call(
        _ada_kernel,
        grid=(depth, n_tiles),
        in_specs=[
            pl.BlockSpec((b, D_MODEL), lambda l, j: (0, 0)),
            pl.BlockSpec((1, D_MODEL, D_MODEL), lambda l, j: (l, 0, j)),
            pl.BlockSpec((1, 1, D_MODEL), lambda l, j: (l, 0, j)),
        ],
        out_specs=pl.BlockSpec((1, b, D_MODEL), lambda l, j: (l, 0, j)),
        out_shape=jax.ShapeDtypeStruct((depth, b, 3 * D_MODEL), _f32),
        compiler_params=_params(),
        name="ada_mod",
    )(c, w_ada, b_ada.reshape(depth, 1, 3 * D_MODEL))


def _fold_kernel(wf_ref, cc_ref, sc_ref, ab_ref, *, norm):
    hp = lax.Precision.HIGHEST
    for g in range(N_FOURIER_GROUPS):
        wf = wf_ref[0, g]
        a = jnp.dot(cc_ref[...], wf, precision=hp, preferred_element_type=_f32) * norm
        bm = jnp.dot(sc_ref[...], wf, precision=hp, preferred_element_type=_f32) * norm
        ab_ref[0, g] = jnp.concatenate([a, bm], axis=1).astype(_bf16)


def _fold_fourier_weights(w_fmap, seq_len):
    depth = w_fmap.shape[0]
    idx = np.arange(FOURIER_GROUP)
    ang = 2.0 * np.pi * ((idx[:, None] * idx[None, :]) % FOURIER_GROUP) / FOURIER_GROUP
    cc = jnp.asarray(np.cos(ang), _f32)
    sc = jnp.asarray(np.sin(ang), _f32)
    norm = float(1.0 / np.sqrt(seq_len * FOURIER_GROUP))
    ng = N_FOURIER_GROUPS
    gspec = pl.BlockSpec((FOURIER_GROUP, FOURIER_GROUP), lambda l: (0, 0))
    return pl.pallas_call(
        functools.partial(_fold_kernel, norm=norm),
        grid=(depth,),
        in_specs=[
            pl.BlockSpec((1, ng, FOURIER_GROUP, FOURIER_GROUP), lambda l: (l, 0, 0, 0)),
            gspec, gspec,
        ],
        out_specs=pl.BlockSpec((1, ng, FOURIER_GROUP, 2 * FOURIER_GROUP), lambda l: (l, 0, 0, 0)),
        out_shape=jax.ShapeDtypeStruct((depth, ng, FOURIER_GROUP, 2 * FOURIER_GROUP), _bf16),
        compiler_params=pltpu.CompilerParams(dimension_semantics=("arbitrary",),
                                             vmem_limit_bytes=VMEM_LIMIT_BYTES),
        name="fold_fourier",
    )(w_fmap, cc, sc)


PREP_ROWS = 2560


def _prep_main_kernel(wt_ref, o_ref):
    o_ref[0] = wt_ref[0].T.astype(_bf16)


def _prep_gate_kernel(wt_ref, o_ref):
    w = wt_ref[0]
    o_ref[0] = jnp.concatenate([w, w, w, jnp.zeros_like(w)], axis=0).T


def _prep_weights(w_in_t):
    depth, _, d = w_in_t.shape
    rank2 = 2 * GATE_RANK
    w_all = pl.pallas_call(
        _prep_main_kernel,
        grid=(depth, D_MAIN // PREP_ROWS),
        in_specs=[pl.BlockSpec((1, PREP_ROWS, d), lambda l, j: (l, j, 0))],
        out_specs=pl.BlockSpec((1, d, PREP_ROWS), lambda l, j: (l, 0, j)),
        out_shape=jax.ShapeDtypeStruct((depth, d, D_MAIN), _bf16),
        compiler_params=_params(),
        name="prep_main",
    )(w_in_t)
    w_gate = pl.pallas_call(
        _prep_gate_kernel,
        grid=(depth,),
        in_specs=[pl.BlockSpec((1, rank2, d), lambda l: (l, D_MAIN // rank2, 0))],
        out_specs=pl.BlockSpec((1, d, 4 * rank2), lambda l: (l, 0, 0)),
        out_shape=jax.ShapeDtypeStruct((depth, d, 4 * rank2), _f32),
        compiler_params=pltpu.CompilerParams(dimension_semantics=("arbitrary",),
                                             vmem_limit_bytes=VMEM_LIMIT_BYTES),
        name="prep_gate",
    )(w_in_t)
    return w_all, w_gate


def _inproj_kernel(x_ref, mod_ref, g_ref, w_ref, ab_ref, wg_ref, wc_ref, bc_ref,
                   p_ref, q_ref, z_ref, qk_ref, v_ref, r_ref, la_ref, u_sc):
    mod = mod_ref[0, 0]
    shift = mod[0:1, :]
    gain = g_ref[0] * (1.0 + mod[1:2, :])

    def normed(rows):
        x = x_ref[0, rows, :]
        ms = jnp.mean(x * x, axis=-1, keepdims=True)
        return (x * lax.rsqrt(ms + EPS) * gain + shift).astype(_bf16)

    rank2 = 2 * GATE_RANK
    half = x_ref.shape[1] // 2
    w_gate = wg_ref[0].astype(_bf16)
    halves, g3 = [], []
    for r0 in (0, half):
        h_half = normed(slice(r0, r0 + half))
        halves.append(h_half)
        g3.append(_dot(h_half, w_gate))
        u_f32 = _dot(h_half, w_ref[0, :, 0:_OFF_Z])
        for s in range(u_sc.shape[0]):
            u_sc[s, r0:r0 + half, :] = u_f32[:, s * LANES:(s + 1) * LANES]
    h = jnp.concatenate(halves, axis=0)
    g3 = jnp.concatenate(g3, axis=0)
    z_ref[0] = _dot(h, w_ref[0, :, _OFF_Z:_OFF_QK]).astype(_bf16)

    g_hi, g_lo = _split_bf16(g3)
    lane = lax.broadcasted_iota(jnp.int32, g3.shape, 1)
    lhs = jnp.where((lane >= rank2) & (lane < 2 * rank2), g_lo, g_hi)
    w_hi, w_lo = _split_bf16(wc_ref[0])
    rhs = jnp.concatenate([w_hi, w_hi, w_lo, jnp.zeros_like(w_hi)], axis=0)

    def log2_decay(k):
        cols = slice(k * GATE_COLS, (k + 1) * GATE_COLS)
        x_g = _dot(lhs, rhs[:, cols]) + bc_ref[0][:, cols]
        soft = jnp.log2(1.0 + jnp.exp2(jnp.abs(x_g) * -LOG2_E))
        la_ref[0, :, cols] = (jnp.minimum(x_g, 0.0) * (LOG2_E / GATE_LOGIT_NORMALIZER)
                              - soft * (1.0 / GATE_LOGIT_NORMALIZER)).astype(_bf16)

    log2_decay(0)
    qk_ref[0] = _dot(h, w_ref[0, :, _OFF_QK:_OFF_V]).astype(_bf16)
    log2_decay(1)
    sub_rows = u_sc.shape[1] // FFT_RADIX
    slabs_per_group = FOURIER_GROUP // LANES
    for g in range(N_FOURIER_GROUPS):
        u_g = jnp.concatenate(
            [jnp.concatenate([u_sc[g * slabs_per_group + s, pl.ds(n1, sub_rows, stride=FFT_RADIX), :]
                              for s in range(slabs_per_group)], axis=1)
             for n1 in range(FFT_RADIX)], axis=0).astype(_bf16)
        pq = _dot(u_g, ab_ref[0, g])
        for n1 in range(FFT_RADIX):
            rows = slice(n1 * sub_rows, (n1 + 1) * sub_rows)
            cols = slice(n1 * FOURIER_GROUP, (n1 + 1) * FOURIER_GROUP)
            p_ref[0, g, :, cols] = pq[rows, :FOURIER_GROUP].astype(_bf16)
            q_ref[0, g, :, cols] = pq[rows, FOURIER_GROUP:].astype(_bf16)
    log2_decay(2)
    v_ref[0] = _dot(h, w_ref[0, :, _OFF_V:_OFF_R]).astype(_bf16)
    log2_decay(3)
    r_ref[0] = _dot(h, w_ref[0, :, _OFF_R:D_MAIN]).astype(_bf16)


def _inproj(x, mod, norm_g3, w_all, ab, w_gate, wc, bc, layer, tm):
    b, seq, d = x.shape
    row = lambda i, j: (i, j, 0)
    lay3 = lambda i, j: (layer, 0, 0)
    lay4 = lambda i, j: (layer, 0, 0, 0)
    ng = N_FOURIER_GROUPS

    def out(width, dtype=_bf16):
        return (pl.BlockSpec((1, tm, width), row),
                jax.ShapeDtypeStruct((b, seq, width), dtype))

    def grouped():
        width = FFT_RADIX * FOURIER_GROUP
        return (pl.BlockSpec((1, ng, tm // FFT_RADIX, width), lambda i, j: (i, 0, j, 0)),
                jax.ShapeDtypeStruct((b, ng, seq // FFT_RADIX, width), _bf16))

    outs = [grouped(), grouped(), out(1024), out(1024), out(1024), out(1024), out(1024)]
    return pl.pallas_call(
        _inproj_kernel,
        grid=(b, seq // tm),
        in_specs=[
            pl.BlockSpec((1, tm, d), row),
            pl.BlockSpec((1, 1, 3, d), lambda i, j: (layer, i, 0, 0)),
            pl.BlockSpec((1, 1, d), lay3),
            pl.BlockSpec((1, d, D_MAIN), lay3, pipeline_mode=pl.Buffered(1)),
            pl.BlockSpec((1, ng, FOURIER_GROUP, 2 * FOURIER_GROUP), lay4),
            pl.BlockSpec((1, d, 8 * GATE_RANK), lay3),
            pl.BlockSpec((1, 2 * GATE_RANK, 2 * D_GLA_KEY), lay3),
            pl.BlockSpec((1, 1, 2 * D_GLA_KEY), lay3),
        ],
        out_specs=[o[0] for o in outs],
        out_shape=[o[1] for o in outs],
        scratch_shapes=[pltpu.VMEM((D_FOURIER // LANES, tm, LANES), _f32)],
        compiler_params=_params(),
        name="inproj",
    )(x, mod, norm_g3, w_all, ab, w_gate, wc, bc)


def _fft_kernel(lm_ref, p_ref, q_ref, o_ref, b_ref):
    ng = p_ref.shape[1]
    sub = p_ref.shape[2]
    grp = o_ref.shape[2] // ng
    half = sub // FFT_HALVES
    for g in range(ng):
        for h in range(FFT_HALVES):
            for n1 in range(FFT_RADIX):
                cols = slice(n1 * grp, (n1 + 1) * grp)
                b_ref[g, n1, h] = (_dot(lm_ref[n1, h, :, 0:sub], p_ref[0, g, :, cols])
                                   + _dot(lm_ref[n1, h, :, sub:2 * sub], q_ref[0, g, :, cols]))

    rt = np.float32(np.sqrt(0.5))

    def combine(g, h, c):
        r0 = c * FFT_ROWS
        for t in range(grp // LANES):
            lanes = slice(t * LANES, (t + 1) * LANES)
            out_lanes = slice(g * grp + t * LANES, g * grp + (t + 1) * LANES)
            a = [b_ref[g, n, h, r0:r0 + FFT_ROWS, lanes] for n in range(FFT_RADIX)]
            bi = [b_ref[g, n, h, half + r0:half + r0 + FFT_ROWS, lanes] for n in range(FFT_RADIX)]
            e0, e1 = a[0] + a[4], a[0] - a[4]
            p1, p2, p3 = a[1] + a[7], a[2] + a[6], a[3] + a[5]
            s13, rd, t0 = p1 + p3, rt * (p1 - p3), e0 + p2
            c0, c4, c2, c1, c3 = t0 + s13, t0 - s13, e0 - p2, e1 + rd, e1 - rd
            m1, m2, m3 = bi[1] - bi[7], bi[2] - bi[6], bi[3] - bi[5]
            rs = rt * (m1 + m3)
            s1, s3, s2 = m2 + rs, rs - m2, m1 - m3
            ys = (c0, c1 - s1, c2 - s2, c3 - s3, c4, c3 + s3, c2 + s2, c1 + s1)
            for k1, yk in enumerate(ys):
                row = k1 * sub + h * half + r0
                o_ref[0, row:row + FFT_ROWS, out_lanes] = yk.astype(_bf16)

    for g in range(ng):
        for h in range(FFT_HALVES):
            for c in range(half // FFT_ROWS):
                combine(g, h, c)


def _fft_matrices(seq):
    sub = seq // FFT_RADIX
    half = sub // FFT_HALVES
    n2 = np.arange(sub)[None, :]
    mats = []
    for n1 in range(FFT_RADIX):
        halves = []
        for h in range(FFT_HALVES):
            k2 = np.arange(h * half, (h + 1) * half)[:, None]
            ang = 2.0 * np.pi * ((k2 * (n1 + FFT_RADIX * n2)) % seq) / seq
            c, s = np.cos(ang), np.sin(ang)
            halves.append(np.block([[c, -s], [s, c]]))
        mats.append(np.stack(halves))
    return jnp.asarray(np.stack(mats), _f32).astype(_bf16)


def _seqfft(lm, p, q):
    b, ng, sub, width = p.shape
    grp = width // FFT_RADIX
    seq = sub * FFT_RADIX
    dec = pl.BlockSpec((1, ng, sub, FFT_RADIX * grp), lambda i: (i, 0, 0, 0))
    return pl.pallas_call(
        _fft_kernel,
        grid=(b,),
        in_specs=[
            pl.BlockSpec((FFT_RADIX, FFT_HALVES, 2 * sub // FFT_HALVES, 2 * sub),
                         lambda i: (0, 0, 0, 0), pipeline_mode=pl.Buffered(1)),
            dec, dec,
        ],
        out_specs=pl.BlockSpec((1, seq, ng * grp), lambda i: (i, 0, 0)),
        out_shape=jax.ShapeDtypeStruct((b, seq, ng * grp), _bf16),
        scratch_shapes=[pltpu.VMEM((ng, FFT_RADIX, FFT_HALVES, 2 * sub // FFT_HALVES, grp), _f32)],
        compiler_params=pltpu.CompilerParams(dimension_semantics=("arbitrary",),
                                             vmem_limit_bytes=VMEM_LIMIT_BYTES),
        name="seqfft",
    )(lm, p, q)


def _gla_kernel(tri_ref, q_ref, k_ref, v_ref, la_ref, o_ref,
                oacc_ref, u_ref, qd_ref, dec_ref, *, seq, heads):
    n_blocks = seq // GLA_BLOCK
    nb = CHUNKS_PER_BLOCK
    dk = GLA_HEAD_K
    scale = dk ** -0.5

    def chunked(a):
        return a.reshape(nb, CHUNK, a.shape[-1])

    def flat(a):
        return a.reshape(a.shape[0] * a.shape[1], a.shape[2])

    lane = lax.broadcasted_iota(jnp.int32, (CHUNK, 2 * CHUNK), 1)
    row_l = lax.broadcasted_iota(jnp.int32, (CHUNK, 2 * CHUNK), 0)
    keep = (((lane >= CHUNK) | (lane <= row_l), (lane < CHUNK) | (lane - CHUNK <= row_l)),
            ((lane >= CHUNK) | (lane > row_l), (lane < CHUNK) | (lane - CHUNK > row_l)))

    dv = GLA_HEAD_V

    def prefix(hh, blk):
        rows = pl.ds(blk * GLA_BLOCK, GLA_BLOCK)
        la = la_ref[0, rows, hh * 2 * dk:(hh + 1) * 2 * dk]
        return hh, blk, rows, _dot(tri_ref[...], la)

    def operands(pre_stage):
        hh, blk, rows, pre = pre_stage
        la_b = la_ref[0, rows, hh * 2 * dk + dk:(hh + 1) * 2 * dk].astype(_f32)
        pre_b = pre[:, dk:]
        g_f = chunked(pre[:, :dk])
        g_b = chunked(pre_b[GLA_BLOCK - 1:GLA_BLOCK, :] - pre_b + la_b)
        zero = jnp.zeros((1, 1, dk), _f32)
        edge_f = g_f[:, CHUNK - 1:CHUNK, :]
        edge_b = g_b[:, 0:1, :]
        dirs = ((g_f, edge_f, jnp.concatenate([zero, edge_f[:nb - 1]], axis=0),
                 CHUNK // 2, list(range(nb))),
                (g_b, edge_b, jnp.concatenate([edge_b[1:], zero], axis=0),
                 CHUNK // 2 - 1, list(range(nb - 1, -1, -1))))

        qs = chunked(q_ref[0, rows, hh * dk:(hh + 1) * dk].astype(_f32) * scale)
        kk = chunked(k_ref[0, rows, hh * dk:(hh + 1) * dk].astype(_f32))
        zeros_chunk = jnp.zeros((CHUNK, dk), _f32)
        per_dir = []
        for d, (g, edge, base, ref_i, order) in enumerate(dirs):
            tot = edge[order[-1]:order[-1] + 1]
            cum = g - base
            ref = cum[:, ref_i:ref_i + 1, :]
            a = cum - ref
            qi = qs * jnp.exp2(a)
            ki = kk * jnp.exp2(-a)
            last_ref = (edge - base) - ref
            qd_ref[hh, d, rows, :] = flat(qi * jnp.exp2(ref + base)).astype(_bf16)
            kd_t = flat(ki * jnp.exp2(last_ref + (tot - edge))).T.astype(_bf16)
            dec_ref[hh, d, blk] = jnp.broadcast_to(jnp.exp2(tot).reshape(1, dk), (dk, dk)).T
            rhs_t = []
            for i in range(nb):
                live = [c for c in range(nb) if order.index(c) <= order.index(i)]
                tiles = sorted({c // 2 for c in live})
                pieces = []
                for c in range(2 * tiles[0], 2 * tiles[-1] + 2):
                    if c == i:
                        pieces.append(ki[c])
                    elif c in live:
                        pieces.append(ki[c] * jnp.exp2(last_ref[c] + ref[i] + base[i] - edge[c]))
                    else:
                        pieces.append(zeros_chunk)
                rhs_t.append((tiles, jnp.concatenate(pieces, axis=0).T.astype(_bf16)))
            per_dir.append((flat(qi).astype(_bf16), rhs_t, kd_t))
        return hh, blk, rows, per_dir

    def scores(ops):
        per_dir = ops[3]
        return [[_dot(qi[i * CHUNK:(i + 1) * CHUNK], rhs_t[i][1]) for i in range(nb)]
                for qi, rhs_t, _ in per_dir]

    def outputs(ops, mats):
        hh, blk, rows, per_dir = ops
        tile_w = 2 * CHUNK
        zero_tile = jnp.zeros((CHUNK, tile_w), _f32)
        score_parts = []
        for d, s_rows in enumerate(mats):
            for i, s_i in enumerate(s_rows):
                tiles = per_dir[d][1][i][0]
                own, half = divmod(i, 2)
                cols = []
                for tt in range(nb // 2):
                    if tt not in tiles:
                        cols.append(zero_tile)
                        continue
                    lo = tiles.index(tt) * tile_w
                    piece = s_i[:, lo:lo + tile_w]
                    cols.append(jnp.where(keep[d][half], piece, 0.0) if tt == own else piece)
                score_parts.append(jnp.concatenate(cols, axis=1).astype(_bf16))
        lhs = jnp.concatenate(score_parts + [p[2] for p in per_dir], axis=0)
        res = _dot(lhs, v_ref[0, rows, hh * dv:(hh + 1) * dv])
        oacc_ref[hh, rows, :] = res[0:GLA_BLOCK] + res[GLA_BLOCK:2 * GLA_BLOCK]
        u_ref[hh, 0, blk] = res[2 * GLA_BLOCK:2 * GLA_BLOCK + dk]
        u_ref[hh, 1, blk] = res[2 * GLA_BLOCK + dk:2 * GLA_BLOCK + 2 * dk]

    def recur(hh, d, blk, state, last):
        if state is None:
            return u_ref[hh, d, blk]
        rows = pl.ds(blk * GLA_BLOCK, GLA_BLOCK)
        oacc_ref[hh, rows, :] += _dot(qd_ref[hh, d, rows, :], state.astype(_bf16))
        if last:
            return state
        dec = dec_ref[hh, d, blk]
        return jnp.concatenate([dec, dec], axis=1) * state + u_ref[hh, d, blk]

    def finish(hh, blk):
        rows = pl.ds(blk * GLA_BLOCK, GLA_BLOCK)
        o_ref[0, rows, hh * dv:(hh + 1) * dv] = oacc_ref[hh, rows, :].astype(_bf16)

    n_pairs = n_blocks // 2
    items = [(hh, t) for hh in range(heads) for t in range(n_pairs)]
    ops_q = {}
    state = {}
    for s in range(len(items) + 1):
        if s < len(items):
            hh, t = items[s]
            pres = [prefix(hh, t), prefix(hh, n_blocks - 1 - t)]
            ops_q[s] = [operands(p) for p in pres]
        if 0 <= s - 1 < len(items):
            hh, t = items[s - 1]
            pair = ops_q.pop(s - 1)
            mats = [scores(o) for o in pair]
            for d, (o, m) in enumerate(zip(pair, mats)):
                outputs(o, m)
                state[hh, d] = recur(hh, d, o[1], state.get((hh, d)), last=False)
            if t == n_pairs - 1:
                for j in range(n_pairs, n_blocks):
                    for d, blk in enumerate((j, n_blocks - 1 - j)):
                        state[hh, d] = recur(hh, d, blk, state[hh, d], last=(j == n_blocks - 1))
                        finish(hh, blk)


def _gla(tri, qk, v, la):
    b, seq, _ = v.shape
    hp = GLA_HEADS_PER_STEP
    steps = N_GLA_HEADS // hp
    n_blocks = seq // GLA_BLOCK
    return pl.pallas_call(
        functools.partial(_gla_kernel, seq=seq, heads=hp),
        grid=(b, steps),
        in_specs=[
            pl.BlockSpec((GLA_BLOCK, GLA_BLOCK), lambda i, j: (0, 0)),
            pl.BlockSpec((1, seq, hp * GLA_HEAD_K), lambda i, j: (i, 0, j)),
            pl.BlockSpec((1, seq, hp * GLA_HEAD_K), lambda i, j: (i, 0, steps + j)),
            pl.BlockSpec((1, seq, hp * GLA_HEAD_V), lambda i, j: (i, 0, j)),
            pl.BlockSpec((1, seq, hp * 2 * GLA_HEAD_K), lambda i, j: (i, 0, j)),
        ],
        out_specs=pl.BlockSpec((1, seq, hp * GLA_HEAD_V), lambda i, j: (i, 0, j)),
        out_shape=jax.ShapeDtypeStruct((b, seq, D_GLA), _bf16),
        scratch_shapes=[
            pltpu.VMEM((hp, seq, GLA_HEAD_V), _f32),
            pltpu.VMEM((hp, 2, n_blocks, GLA_HEAD_K, GLA_HEAD_V), _f32),
            pltpu.VMEM((hp, 2, seq, GLA_HEAD_K), _bf16),
            pltpu.VMEM((hp, 2, n_blocks, GLA_HEAD_K, GLA_HEAD_K), _f32),
        ],
        compiler_params=_params(),
        name="gla",
    )(tri, qk, qk, v, la)


def _outproj_kernel(x_ref, yf_ref, z_ref, yg_ref, r_ref, gn_ref, w_ref, mod_ref, fg_ref, o_ref, *, final):
    assert OUTPROJ_K_CHUNK == GLA_HEAD_V
    y = None
    for is_gla, y_ref, gate_ref, w0 in ((False, yf_ref, z_ref, 0), (True, yg_ref, r_ref, D_FOURIER)):
        for c0 in range(0, y_ref.shape[2], OUTPROJ_K_CHUNK):
            cols = slice(c0, c0 + OUTPROJ_K_CHUNK)
            branch = y_ref[0, :, cols].astype(_f32)
            if is_gla:
                ms = jnp.mean(branch * branch, axis=-1, keepdims=True)
                branch = branch * lax.rsqrt(ms + EPS) * gn_ref[0][:, cols]
            a = (branch * _silu(gate_ref[0, :, cols].astype(_f32))).astype(_bf16)
            part = _dot(a, w_ref[0, w0 + c0:w0 + c0 + OUTPROJ_K_CHUNK, :].astype(_bf16))
            y = part if y is None else y + part
    gate = mod_ref[0, 0][2:3, :]
    xn = x_ref[0] + gate * y
    if final:
        ms = jnp.mean(xn * xn, axis=-1, keepdims=True)
        xn = xn * lax.rsqrt(ms + EPS) * fg_ref[...]
    o_ref[0] = xn


def _outproj(x, yf, z, yg, r, gla_norm_g3, w_out, mod, final_g, layer, tm, final):
    b, seq, d = x.shape
    row = lambda i, j: (i, j, 0)
    return pl.pallas_call(
        functools.partial(_outproj_kernel, final=final),
        grid=(b, seq // tm),
        in_specs=[
            pl.BlockSpec((1, tm, d), row),
            pl.BlockSpec((1, tm, D_FOURIER), row),
            pl.BlockSpec((1, tm, D_FOURIER), row),
            pl.BlockSpec((1, tm, D_GLA), row),
            pl.BlockSpec((1, tm, D_GLA), row),
            pl.BlockSpec((1, 1, D_GLA), lambda i, j: (layer, 0, 0)),
            pl.BlockSpec((1, D_FOURIER + D_GLA, d), lambda i, j: (layer, 0, 0),
                         pipeline_mode=pl.Buffered(1)),
            pl.BlockSpec((1, 1, 3, d), lambda i, j: (layer, i, 0, 0)),
            pl.BlockSpec((1, d), lambda i, j: (0, 0)),
        ],
        out_specs=pl.BlockSpec((1, tm, d), row),
        out_shape=jax.ShapeDtypeStruct((b, seq, d), _f32),
        compiler_params=_params(),
        name="outproj",
    )(x, yf, z, yg, r, gla_norm_g3, w_out, mod, final_g)


def _block_tri():
    idx = np.arange(GLA_BLOCK)
    return jnp.asarray((idx[None, :] <= idx[:, None]).astype(np.float32), _bf16)


def kernel(x, c, norm_g, w_ada, b_ada, w_in, w_fmap, w_af, b_af, w_ab, b_ab, gla_norm_g, w_out, final_g):
    depth = w_in.shape[0]
    b, seq, d = x.shape
    assert d == D_MODEL and seq % (2 * GLA_BLOCK) == 0 and seq % (FFT_RADIX * FFT_ROWS) == 0

    mod = _ada_mod(c, w_ada, b_ada).reshape(depth, b, 3, d)
    ab = _fold_fourier_weights(w_fmap, seq)
    w_in_t = jnp.swapaxes(w_in, 1, 2)
    w_all, w_gate = _prep_weights(w_in_t)

    zeros = jnp.zeros_like(w_af)
    wc = jnp.concatenate([jnp.concatenate([w_af, zeros], axis=1),
                          jnp.concatenate([zeros, w_ab], axis=1)], axis=-1)
    bc = jnp.concatenate([b_af, b_ab], axis=-1)

    def per_head(a):
        lead = a.shape[:-1]
        a = a.reshape(lead + (2, N_GLA_HEADS, GLA_HEAD_K))
        return jnp.swapaxes(a, -3, -2).reshape(lead + (2 * D_GLA_KEY,))

    wc = per_head(wc)
    bc = per_head(bc).reshape(depth, 1, 2 * D_GLA_KEY)

    lm = _fft_matrices(seq)
    tri = _block_tri()
    norm_g3 = norm_g.reshape(depth, 1, d)
    gla_norm_g3 = gla_norm_g.reshape(depth, 1, D_GLA)
    final_g2 = final_g.reshape(1, d)

    for l in range(depth):
        p, q, z, qk, v, r, la = _inproj(x, mod, norm_g3, w_all, ab, w_gate, wc, bc, layer=l, tm=1024)
        y_f = _seqfft(lm, p, q)
        y_g = _gla(tri, qk, v, la)
        x = _outproj(x, y_f, z, y_g, r, gla_norm_g3, w_out, mod, final_g2, layer=l, tm=512,
                     final=(l == depth - 1))
    return x
```
